```python
import functools
import jax, jax.numpy as jnp
from jax import lax
import numpy as np

D_MODEL = 2048
BATCH = 2
SEQ = 4096
DEPTH = 4
DEC_BATCH = 8
DEC_SEQ = 1
PAST_LEN = 16384
PAGE_SIZE = 128

HEAD_DIM = 128
FOX_WIDTH = D_MODEL // 2
N_FOX_HEADS = FOX_WIDTH // HEAD_DIM
Q_BLOCK = 128
GMLP_WIDTH = D_MODEL // 4
GMLP_GROUP = 128
N_GMLP_GROUPS = GMLP_WIDTH // GMLP_GROUP
GMLP_CHUNK = 128
GLA_WIDTH = D_MODEL // 4
N_GLA_HEADS = 4
GLA_DV = GLA_WIDTH // N_GLA_HEADS
GLA_DK = GLA_DV // 2
GLA_KW = N_GLA_HEADS * GLA_DK
GLA_RANK = 16
GLA_TAU = 16.0
GLA_CHUNK = 64
MIX_WIDTH = FOX_WIDTH + GMLP_WIDTH + GLA_WIDTH
D_FF = ((8 * D_MODEL // 3 + 255) // 256) * 256
ALPHA = (2 * DEPTH) ** 0.25
BETA = (8 * DEPTH) ** -0.25
LN_EPS = 1e-5
SPLIT_SIZES = (FOX_WIDTH, FOX_WIDTH, FOX_WIDTH, N_FOX_HEADS, GMLP_WIDTH, GMLP_WIDTH, GLA_KW, GLA_KW, GLA_WIDTH, GLA_RANK, GLA_WIDTH)
SPLIT_POINTS = tuple(int(p) for p in np.cumsum(SPLIT_SIZES)[:-1])
IN_WIDTH = sum(SPLIT_SIZES)

kernel_name = 'hybrid_fox_gmlp_gla_decoder_step'


def layer_norm(x, g, b):
    xf = x.astype(jnp.float32)
    mu = xf.mean(-1, keepdims=True)
    var = jnp.mean(jnp.square(xf - mu), -1, keepdims=True)
    return ((xf - mu) * lax.rsqrt(var + LN_EPS) * g + b).astype(x.dtype)


def fox_prompt(q, k, v, logf):
    b_, s_, h_, d_ = q.shape
    cum = jnp.cumsum(logf, axis=1)
    cum_k = cum.transpose(0, 2, 1)[:, :, None, :]
    kpos = jnp.arange(s_)
    scale = d_ ** -0.5

    def block(i):
        start = i * Q_BLOCK
        qb = lax.dynamic_slice_in_dim(q, start, Q_BLOCK, axis=1)
        cq = lax.dynamic_slice_in_dim(cum, start, Q_BLOCK, axis=1).transpose(0, 2, 1)[..., None]
        s = jnp.einsum('bqhd,bkhd->bhqk', qb, k).astype(jnp.float32) * scale + (cq - cum_k)
        qpos = start + jnp.arange(Q_BLOCK)
        s = jnp.where(kpos[None, :] <= qpos[:, None], s, -jnp.inf)
        p = jax.nn.softmax(s, axis=-1).astype(v.dtype)
        return jnp.einsum('bhqk,bkhd->bqhd', p, v)

    out = lax.map(block, jnp.arange(s_ // Q_BLOCK))
    return out.transpose(1, 0, 2, 3, 4).reshape(b_, s_, h_, d_)


def fox_sample(q, k, v, logf, k_pool, v_pool, f_pool, page_table):
    bd, t_, h_, d_ = q.shape
    past = page_table.shape[1] * PAGE_SIZE
    k_past = k_pool[page_table].reshape(bd, past, h_, d_)
    v_past = v_pool[page_table].reshape(bd, past, h_, d_)
    f_past = f_pool[page_table].reshape(bd, past, h_).astype(jnp.float32)
    cum = jnp.cumsum(jnp.concatenate([f_past, logf], axis=1), axis=1)
    cq = cum[:, past:].transpose(0, 2, 1)[..., None]
    ck = cum.transpose(0, 2, 1)[:, :, None, :]
    s = jnp.concatenate([jnp.einsum('bqhd,bkhd->bhqk', q, k_past),
                         jnp.einsum('bqhd,bkhd->bhqk', q, k)], axis=-1)
    s = s.astype(jnp.float32) * (d_ ** -0.5) + (cq - ck)
    mask = jnp.arange(past + t_)[None, :] <= (past + jnp.arange(t_))[:, None]
    p = jax.nn.softmax(jnp.where(mask, s, -jnp.inf), axis=-1).astype(v.dtype)
    return (jnp.einsum('bhqk,bkhd->bqhd', p[..., :past], v_past)
            + jnp.einsum('bhqk,bkhd->bqhd', p[..., past:], v))


def gla_chunk(state, q, k, v, la):
    l_ = q.shape[1]
    b = jnp.cumsum(la, axis=1)
    o_inter = jnp.einsum('blhk,bhkv->blhv', q * jnp.exp(b), state)
    tril = jnp.tril(jnp.ones((l_, l_), bool))
    diff = b[:, :, None] - b[:, None, :]
    decay = jnp.exp(jnp.where(tril[None, :, :, None, None], diff, -jnp.inf))
    a = jnp.einsum('bthk,bshk,btshk->bhts', q, k, decay)
    o_intra = jnp.einsum('bhts,bshv->bthv', a, v)
    b_last = b[:, -1]
    new_state = (jnp.exp(b_last)[..., None] * state
                 + jnp.einsum('bshk,bshv->bhkv', k * jnp.exp(b_last[:, None] - b), v))
    return new_state, o_inter + o_intra


def gla_prompt(q, k, v, la):
    b_, s_, h_, k_ = q.shape
    nc = s_ // GLA_CHUNK

    def to_chunks(a):
        return a.reshape(b_, nc, GLA_CHUNK, *a.shape[2:]).swapaxes(0, 1)

    s0 = jnp.zeros((b_, h_, k_, GLA_DV), jnp.float32)
    s_fin, o = lax.scan(lambda st, xs: gla_chunk(st, *xs), s0,
                        (to_chunks(q), to_chunks(k), to_chunks(v), to_chunks(la)))
    return o.swapaxes(0, 1).reshape(b_, s_, h_, GLA_DV), s_fin


def gla_sample(q, k, v, la, s0):
    s_new, o = gla_chunk(s0.astype(jnp.float32), q, k, v, la)
    return o, s_new


def gmlp_mix(u, v, ln_g, ln_b, ws, bs):
    b_, l_, _ = v.shape
    t_ = min(l_, GMLP_CHUNK)
    vn = layer_norm(v.reshape(b_, l_, N_GMLP_GROUPS, GMLP_GROUP),
                    ln_g.reshape(N_GMLP_GROUPS, GMLP_GROUP), ln_b.reshape(N_GMLP_GROUPS, GMLP_GROUP))
    w = jnp.tril(ws[:, :t_, :t_])
    vc = vn.reshape(b_, l_ // t_, t_, N_GMLP_GROUPS, GMLP_GROUP)
    mixed = jnp.einsum('gts,bnsgc->bntgc', w, vc) + bs[:, :t_].T[None, None, :, :, None]
    return u * mixed.reshape(b_, l_, GMLP_WIDTH), vn.reshape(b_, l_, GMLP_WIDTH)


def mix_layer(h, w_in, b_f, gm_ln_g, gm_ln_b, gm_ws, gm_bs, gla_w2, gla_b2, gla_norm_g, w_o,
              fox_attend, gla_run):
    (fq, fk, fv, ff, mu, mv, gq, gk, gv, gr, gg) = jnp.split(h @ w_in, SPLIT_POINTS, axis=-1)
    b_, l_, _ = h.shape
    fheads = lambda a: a.reshape(b_, l_, N_FOX_HEADS, HEAD_DIM)
    logf = jax.nn.log_sigmoid((ff + b_f).astype(jnp.float32))
    fk_h, fv_h = fheads(fk), fheads(fv)
    fox_o = fox_attend(fheads(fq), fk_h, fv_h, logf).reshape(b_, l_, FOX_WIDTH)
    gm_o, gm_v = gmlp_mix(jax.nn.gelu(mu), jax.nn.gelu(mv), gm_ln_g, gm_ln_b, gm_ws, gm_bs)
    gheads = lambda a, d: a.reshape(b_, l_, N_GLA_HEADS, d).astype(jnp.float32)
    q = gheads(gq, GLA_DK) * (GLA_DK ** -0.5)
    k = gheads(gk, GLA_DK)
    v = gheads(gv, GLA_DV)
    la = gheads(jax.nn.log_sigmoid((gr @ gla_w2 + gla_b2).astype(jnp.float32)), GLA_DK) / GLA_TAU
    o, s_new = gla_run(q, k, v, la)
    o = o * lax.rsqrt(jnp.mean(jnp.square(o), -1, keepdims=True) + LN_EPS)
    gla_o = (o.reshape(b_, l_, GLA_WIDTH) * gla_norm_g * jax.nn.silu(gg.astype(jnp.float32))).astype(h.dtype)
    y = jnp.concatenate([fox_o.astype(h.dtype), gm_o.astype(h.dtype), gla_o], axis=-1) @ w_o
    return y, (fk_h, fv_h, logf, gm_v, s_new)


def trunk_layer(x, c, lw, fox_attend, gla_run):
    (w_ada, b_ada, w_in, b_f, gm_ln_g, gm_ln_b, gm_ws, gm_bs, gla_w2, gla_b2, gla_norm_g, w_o,
     ln1_g, ln1_b, w_gu, w_down, ln2_g, ln2_b) = lw
    sh1, sc1, g1, sh2, sc2, g2 = jnp.split((jax.nn.silu(c) @ w_ada + b_ada)[:, None, :], 6, axis=-1)
    y, new = mix_layer(x * (1 + sc1) + sh1, w_in, b_f, gm_ln_g, gm_ln_b, gm_ws, gm_bs,
                       gla_w2, gla_b2, gla_norm_g, w_o, fox_attend, gla_run)
    x = layer_norm(ALPHA * x + g1 * y, ln1_g, ln1_b)
    gate, up = jnp.split((x * (1 + sc2) + sh2) @ w_gu, 2, axis=-1)
    x = layer_norm(ALPHA * x + g2 * ((jax.nn.silu(gate) * up) @ w_down), ln2_g, ln2_b)
    return x, new


def setup_inputs(seed: int = 0) -> dict:
    key = jax.random.key(seed)
    ks = jax.random.split(key, 40)
    nrm = lambda i, shape: jax.random.normal(ks[i], shape, jnp.float32)
    n_pages = PAST_LEN // PAGE_SIZE
    n_pool = (DEC_BATCH * n_pages * 5) // 4
    page_table = jax.random.permutation(ks[0], n_pool)[:DEC_BATCH * n_pages].reshape(DEC_BATCH, n_pages).astype(jnp.int32)
    d = D_MODEL
    return {
        'x_prompt': nrm(1, (BATCH, SEQ, d)),
        'x_sample': nrm(2, (DEC_BATCH, DEC_SEQ, d)),
        'cache_k': nrm(3, (DEPTH, n_pool, PAGE_SIZE, N_FOX_HEADS, HEAD_DIM)),
        'cache_v': nrm(4, (DEPTH, n_pool, PAGE_SIZE, N_FOX_HEADS, HEAD_DIM)),
        'cache_logf': jax.nn.log_sigmoid(2.0 + nrm(5, (DEPTH, n_pool, PAGE_SIZE, N_FOX_HEADS))),
        'state_gla': 0.5 * nrm(6, (DEPTH, DEC_BATCH, N_GLA_HEADS, GLA_DK, GLA_DV)),
        'page_table': page_table,
        'c_prompt': nrm(7, (BATCH, d)),
        'c_sample': nrm(8, (DEC_BATCH, d)),
        'ln_in_g': 1.0 + 0.05 * nrm(9, (d,)),
        'ln_in_b': 0.01 * nrm(10, (d,)),
        'w_ada': 0.5 * d ** -0.5 * nrm(11, (DEPTH, d, 6 * d)),
        'b_ada': 0.01 * nrm(12, (DEPTH, 6 * d)),
        'w_in': d ** -0.5 * nrm(13, (DEPTH, d, IN_WIDTH)),
        'b_f': 2.0 + 0.1 * nrm(14, (DEPTH, N_FOX_HEADS)),
        'gm_ln_g': 1.0 + 0.05 * nrm(15, (DEPTH, GMLP_WIDTH)),
        'gm_ln_b': 0.01 * nrm(16, (DEPTH, GMLP_WIDTH)),
        'gm_ws': GMLP_CHUNK ** -0.5 * nrm(17, (DEPTH, N_GMLP_GROUPS, GMLP_CHUNK, GMLP_CHUNK)),
        'gm_bs': 1.0 + 0.05 * nrm(18, (DEPTH, N_GMLP_GROUPS, GMLP_CHUNK)),
        'gla_w2': GLA_RANK ** -0.5 * nrm(19, (DEPTH, GLA_RANK, GLA_KW)),
        'gla_b2': 0.01 * nrm(20, (DEPTH, GLA_KW)),
        'gla_norm_g': 1.0 + 0.05 * nrm(21, (DEPTH, GLA_WIDTH)),
        'w_o': BETA * MIX_WIDTH ** -0.5 * nrm(22, (DEPTH, MIX_WIDTH, d)),
        'ln1_g': 1.0 + 0.05 * nrm(23, (DEPTH, d)),
        'ln1_b': 0.01 * nrm(24, (DEPTH, d)),
        'w_gu': d ** -0.5 * nrm(25, (DEPTH, d, 2 * D_FF)),
        'w_down': BETA * D_FF ** -0.5 * nrm(26, (DEPTH, D_FF, d)),
        'ln2_g': 1.0 + 0.05 * nrm(27, (DEPTH, d)),
        'ln2_b': 0.01 * nrm(28, (DEPTH, d)),
    }


def reference(x_prompt, x_sample, cache_k, cache_v, cache_logf, state_gla, page_table, c_prompt, c_sample,
              ln_in_g, ln_in_b, w_ada, b_ada, w_in, b_f, gm_ln_g, gm_ln_b, gm_ws, gm_bs, gla_w2, gla_b2,
              gla_norm_g, w_o, ln1_g, ln1_b, w_gu, w_down, ln2_g, ln2_b):
    xp = layer_norm(x_prompt, ln_in_g, ln_in_b)
    xs = layer_norm(x_sample, ln_in_g, ln_in_b)
    kp, vp, fp, sp = [], [], [], []
    kd, vd, fd, sd, gd = [], [], [], [], []
    for l in range(DEPTH):
        lw = (w_ada[l], b_ada[l], w_in[l], b_f[l], gm_ln_g[l], gm_ln_b[l], gm_ws[l], gm_bs[l],
              gla_w2[l], gla_b2[l], gla_norm_g[l], w_o[l], ln1_g[l], ln1_b[l], w_gu[l], w_down[l],
              ln2_g[l], ln2_b[l])
        xp, (k_, v_, f_, _, s_) = trunk_layer(xp, c_prompt, lw, fox_prompt, gla_prompt)
        kp.append(k_); vp.append(v_); fp.append(f_); sp.append(s_)
        fox_s = functools.partial(fox_sample, k_pool=cache_k[l], v_pool=cache_v[l],
                                  f_pool=cache_logf[l], page_table=page_table)
        gla_s = functools.partial(gla_sample, s0=state_gla[l])
        xs, (k_, v_, f_, g_, s_) = trunk_layer(xs, c_sample, lw, fox_s, gla_s)
        kd.append(k_); vd.append(v_); fd.append(f_); gd.append(g_); sd.append(s_)
    return (xp, xs, jnp.stack(kp), jnp.stack(vp), jnp.stack(fp), jnp.stack(sp),
            jnp.stack(kd), jnp.stack(vd), jnp.stack(fd), jnp.stack(sd), jnp.stack(gd))
```

```python
import functools

import numpy as np
import jax
import jax.numpy as jnp
from jax import lax
from jax.experimental import pallas as pl
from jax.experimental.pallas import tpu as pltpu

F32 = jnp.float32
BF16 = jnp.bfloat16

LANES = 128
HEAD_DIM = 128
N_FOX_HEADS = 8
GMLP_GROUPS = 4
GLA_HEADS = 4
GLA_DK = 64
GLA_KW = GLA_HEADS * GLA_DK
GLA_WIDTH = GLA_HEADS * LANES
GLA_RANK = 16
GLA_TAU = 16.0
CHUNK = 128
PAGE = 128
LN_EPS = 1e-5
VMEM_LIMIT = 48 * 1024 * 1024

COL_Q, COL_K, COL_V, COL_M, COL_G, COL_T, COL_END = 0, 1024, 2048, 3072, 4096, 5632, 5760


def _cparams(*sem):
    return pltpu.CompilerParams(dimension_semantics=sem, vmem_limit_bytes=VMEM_LIMIT)


def _log_sigmoid(x):
    return jnp.minimum(x, 0.0) - jnp.log1p(jnp.exp(-jnp.abs(x)))


def _silu(x):
    return x * (1.0 / (1.0 + jnp.exp(-x)))


def _split3(x):
    hi = x.astype(BF16)
    r = x - hi.astype(F32)
    mid = r.astype(BF16)
    lo = (r - mid.astype(F32)).astype(BF16)
    return hi, mid, lo


def _dot(a, b):
    return jnp.dot(a, b, preferred_element_type=F32)


def _dot_nt(a, b):
    return lax.dot_general(a, b, (((1,), (1,)), ((), ())), preferred_element_type=F32)


def _dot_exact(x, w):
    hi, mid, lo = _split3(x)
    return _dot(hi, w) + _dot(mid, w) + _dot(lo, w)


def _ada_kernel(c_ref, w_ref, b_ref, o_ref):
    a = _silu(c_ref[...]).astype(BF16)
    o_ref[...] = _dot(a, w_ref[...].astype(BF16)) + b_ref[...]


def _ada(c16, w_ada, b_ada):
    depth, d, n = w_ada.shape
    tn = 1024
    return pl.pallas_call(
        _ada_kernel,
        grid=(depth, n // tn),
        in_specs=[pl.BlockSpec((16, d), lambda l, j: (0, 0)),
                  pl.BlockSpec((None, d, tn), lambda l, j: (l, 0, j)),
                  pl.BlockSpec((None, 1, tn), lambda l, j: (l, 0, j))],
        out_specs=pl.BlockSpec((None, 16, tn), lambda l, j: (l, 0, j)),
        out_shape=jax.ShapeDtypeStruct((depth, 16, n), F32),
        compiler_params=_cparams("parallel", "parallel"),
        name="ada",
    )(c16, w_ada, b_ada.reshape(depth, 1, n))


def _ln_kernel(*refs, has_res, has_mod, alpha):
    it = iter(refs)
    x = next(it)[...]
    if has_res:
        y = next(it)[...]
        gate = next(it)[...]
        x = alpha * x + gate * y
    g = next(it)[...]
    b = next(it)[...]
    if has_mod:
        sc = next(it)[...]
        sh = next(it)[...]
    xn_ref = next(it)
    mu = jnp.mean(x, axis=-1, keepdims=True)
    xc = x - mu
    var = jnp.mean(xc * xc, axis=-1, keepdims=True)
    xn = xc * lax.rsqrt(var + LN_EPS) * g + b
    xn_ref[...] = xn
    if has_mod:
        h_ref = next(it)
        h_ref[...] = (xn * (1.0 + sc) + sh).astype(BF16)


def _ln_mod(x, g, b, *, res=None, mod=None, alpha=1.0, tr):
    bsz, s, d = x.shape
    tr = min(tr, s)
    row = pl.BlockSpec((None, tr, d), lambda i, j: (i, j, 0))
    per_b = pl.BlockSpec((None, 1, d), lambda i, j: (i, 0, 0))
    vec = pl.BlockSpec((1, d), lambda i, j: (0, 0))
    args, specs = [x], [row]
    if res is not None:
        y, gate = res
        args += [y, gate]
        specs += [row, per_b]
    args += [g.reshape(1, d), b.reshape(1, d)]
    specs += [vec, vec]
    out_shape = [jax.ShapeDtypeStruct((bsz, s, d), F32)]
    out_specs = [row]
    if mod is not None:
        sc, sh = mod
        args += [sc, sh]
        specs += [per_b, per_b]
        out_shape.append(jax.ShapeDtypeStruct((bsz, s, d), BF16))
        out_specs.append(row)
    out = pl.pallas_call(
        functools.partial(_ln_kernel, has_res=res is not None, has_mod=mod is not None, alpha=alpha),
        grid=(bsz, s // tr),
        in_specs=specs, out_specs=out_specs, out_shape=out_shape,
        compiler_params=_cparams("parallel", "parallel"),
        name="ln_mod",
    )(*args)
    return out if mod is not None else (out[0], None)


def _mm_kernel(x_ref, w_ref, *o_refs):
    acc = _dot(x_ref[...], w_ref[...])
    for o in o_refs:
        o[...] = acc.astype(o.dtype)


def _matmul(x, w, layer, col0, ncols, out_dtypes, *, tm, tn):
    m, k = x.shape
    tm = min(tm, m)
    c0 = col0 // tn
    blk = pl.BlockSpec((tm, tn), lambda j, i: (i, j))
    return pl.pallas_call(
        _mm_kernel,
        grid=(ncols // tn, m // tm),
        in_specs=[pl.BlockSpec((tm, k), lambda j, i: (i, 0)),
                  pl.BlockSpec((None, k, tn), lambda j, i: (layer, 0, c0 + j))],
        out_specs=[blk] * len(out_dtypes),
        out_shape=[jax.ShapeDtypeStruct((m, ncols), dt) for dt in out_dtypes],
        compiler_params=_cparams("parallel", "arbitrary"),
        name="matmul",
    )(x, w)


def _swiglu_kernel(x_ref, wg_ref, wu_ref, o_ref):
    x = x_ref[...]
    gate = _dot(x, wg_ref[...])
    up = _dot(x, wu_ref[...])
    o_ref[...] = (_silu(gate) * up).astype(BF16)


def _swiglu(x, w_gu, layer, *, tm, tn):
    m, k = x.shape
    tm = min(tm, m)
    dff = w_gu.shape[2] // 2
    up0 = dff // tn
    return pl.pallas_call(
        _swiglu_kernel,
        grid=(dff // tn, m // tm),
        in_specs=[pl.BlockSpec((tm, k), lambda j, i: (i, 0)),
                  pl.BlockSpec((None, k, tn), lambda j, i: (layer, 0, j)),
                  pl.BlockSpec((None, k, tn), lambda j, i: (layer, 0, up0 + j))],
        out_specs=pl.BlockSpec((tm, tn), lambda j, i: (i, j)),
        out_shape=jax.ShapeDtypeStruct((m, dff), BF16),
        compiler_params=_cparams("parallel", "arbitrary"),
        name="swiglu",
    )(x, w_gu, w_gu)


def _fox_cum_kernel(t_ref, bfr_ref, bfc_ref, lf_ref, cum_ref, carry, *, tb):
    @pl.when(pl.program_id(1) == 0)
    def _():
        carry[...] = jnp.zeros(carry.shape, F32)

    tail = t_ref[...]
    lf_ref[...] = _log_sigmoid(tail[:, 0:N_FOX_HEADS] + bfr_ref[...])
    lft = _log_sigmoid(tail.T[0:N_FOX_HEADS, :] + bfc_ref[...])
    r = lax.broadcasted_iota(jnp.int32, (tb, tb), 0)
    c = lax.broadcasted_iota(jnp.int32, (tb, tb), 1)
    upper = (r <= c).astype(BF16)
    ones = jnp.ones((tb, LANES), BF16)
    base = carry[...]
    cum_ref[...] = _dot_exact(lft, upper) + jnp.concatenate([base] * (tb // LANES), axis=1)
    carry[...] = base + _dot_exact(lft, ones)


def _fox_cum(tail, b_f):
    bsz, s, _ = tail.shape
    tb = min(512, s)
    return pl.pallas_call(
        functools.partial(_fox_cum_kernel, tb=tb),
        grid=(bsz, s // tb),
        in_specs=[pl.BlockSpec((None, tb, LANES), lambda b, i: (b, i, 0)),
                  pl.BlockSpec((1, N_FOX_HEADS), lambda b, i: (0, 0)),
                  pl.BlockSpec((N_FOX_HEADS, 1), lambda b, i: (0, 0))],
        out_specs=[pl.BlockSpec((None, tb, N_FOX_HEADS), lambda b, i: (b, i, 0)),
                   pl.BlockSpec((None, N_FOX_HEADS, tb), lambda b, i: (b, 0, i))],
        out_shape=[jax.ShapeDtypeStruct((bsz, s, N_FOX_HEADS), F32),
                   jax.ShapeDtypeStruct((bsz, N_FOX_HEADS, s), F32)],
        scratch_shapes=[pltpu.VMEM((N_FOX_HEADS, LANES), F32)],
        compiler_params=_cparams("parallel", "arbitrary"),
        name="fox_cum",
    )(tail, b_f.reshape(1, N_FOX_HEADS), b_f.reshape(N_FOX_HEADS, 1))


def _fox_kernel(qi_ref, kj_ref, q_ref, k_ref, v_ref, cq_ref, ck_ref, o_ref, m_sc, l_sc, acc_sc, cqb_sc, *, tq, tk):
    h = pl.program_id(1)
    p = pl.program_id(2)
    qi = qi_ref[p]
    kj = kj_ref[p]
    reps = tk // LANES

    @pl.when(kj == 0)
    def _():
        m_sc[...] = jnp.full(m_sc.shape, -jnp.inf, F32)
        l_sc[...] = jnp.zeros(l_sc.shape, F32)
        acc_sc[...] = jnp.zeros(acc_sc.shape, F32)
        row = cq_ref[pl.ds(h, 1), :]
        cqb_sc[...] = jnp.broadcast_to(row, (LANES, tq)).T

    s = _dot_nt(q_ref[...], k_ref[...]) * (HEAD_DIM ** -0.5)
    ck = ck_ref[pl.ds(h, 1), :]
    s = s + (jnp.concatenate([cqb_sc[...]] * reps, axis=1) - ck)
    rows = qi * tq + lax.broadcasted_iota(jnp.int32, (tq, tk), 0)
    cols = kj * tk + lax.broadcasted_iota(jnp.int32, (tq, tk), 1)
    s = jnp.where(cols <= rows, s, -jnp.inf)
    m_prev = m_sc[...]
    m_new = jnp.maximum(m_prev, jnp.max(s, axis=1, keepdims=True))
    alpha = jnp.exp(m_prev - m_new)
    pr = jnp.exp(s - jnp.concatenate([m_new] * reps, axis=1))
    l_new = alpha * l_sc[...] + jnp.sum(pr, axis=1, keepdims=True)
    acc = alpha * acc_sc[...] + _dot(pr.astype(BF16), v_ref[...])
    m_sc[...] = m_new
    l_sc[...] = l_new
    acc_sc[...] = acc

    @pl.when(kj == qi)
    def _():
        o_ref[...] = (acc / l_new).astype(o_ref.dtype)


def _fox_prompt(q, kv, cum):
    bsz, s, _ = q.shape
    tq = tk = min(512, s)
    nq = s // tq
    pairs = [(i, j) for i in range(nq) for j in range(i + 1)]
    qi_tbl = jnp.asarray([p[0] for p in pairs], jnp.int32)
    kj_tbl = jnp.asarray([p[1] for p in pairs], jnp.int32)
    grid_spec = pltpu.PrefetchScalarGridSpec(
        num_scalar_prefetch=2,
        grid=(bsz, N_FOX_HEADS, len(pairs)),
        in_specs=[pl.BlockSpec((None, tq, LANES), lambda b, h, p, qi, kj: (b, qi[p], h)),
                  pl.BlockSpec((None, tk, LANES), lambda b, h, p, qi, kj: (b, kj[p], h)),
                  pl.BlockSpec((None, tk, LANES), lambda b, h, p, qi, kj: (b, kj[p], N_FOX_HEADS + h)),
                  pl.BlockSpec((None, N_FOX_HEADS, tq), lambda b, h, p, qi, kj: (b, 0, qi[p])),
                  pl.BlockSpec((None, N_FOX_HEADS, tk), lambda b, h, p, qi, kj: (b, 0, kj[p]))],
        out_specs=pl.BlockSpec((None, tq, LANES), lambda b, h, p, qi, kj: (b, qi[p], h)),
        scratch_shapes=[pltpu.VMEM((tq, LANES), F32)] * 4,
    )
    return pl.pallas_call(
        functools.partial(_fox_kernel, tq=tq, tk=tk),
        grid_spec=grid_spec,
        out_shape=jax.ShapeDtypeStruct((bsz, s, N_FOX_HEADS * LANES), BF16),
        compiler_params=_cparams("parallel", "parallel", "arbitrary"),
        name="fox_prompt",
    )(qi_tbl, kj_tbl, q, kv, kv, cum, cum)


def _gmlp_kernel(uv_ref, g_ref, b_ref, ws_ref, bs_ref, o_ref, *vn_ref):
    width = GMLP_GROUPS * LANES
    uv = uv_ref[...]
    r = lax.broadcasted_iota(jnp.int32, (CHUNK, CHUNK), 0)
    c = lax.broadcasted_iota(jnp.int32, (CHUNK, CHUNK), 1)
    for grp in range(GMLP_GROUPS):
        lo = grp * LANES
        u = jax.nn.gelu(uv[:, lo:lo + LANES])
        v = jax.nn.gelu(uv[:, width + lo:width + lo + LANES])
        mu = jnp.mean(v, axis=-1, keepdims=True)
        vc = v - mu
        var = jnp.mean(vc * vc, axis=-1, keepdims=True)
        vn = vc * lax.rsqrt(var + LN_EPS) * g_ref[:, lo:lo + LANES] + b_ref[:, lo:lo + LANES]
        w = jnp.where(c <= r, ws_ref[grp], 0.0).astype(BF16)
        mixed = _dot(w, vn.astype(BF16)) + bs_ref[grp]
        o_ref[:, lo:lo + LANES] = (u * mixed).astype(o_ref.dtype)
        if vn_ref:
            vn_ref[0][:, lo:lo + LANES] = vn


def _gmlp(uv, ln_g, ln_b, ws, bs, *, emit_vn):
    bsz, s, _ = uv.shape
    width = GMLP_GROUPS * LANES
    blk = pl.BlockSpec((None, CHUNK, width), lambda b, i: (b, i, 0))
    vec = pl.BlockSpec((1, width), lambda b, i: (0, 0))
    cube = pl.BlockSpec((GMLP_GROUPS, CHUNK, CHUNK), lambda b, i: (0, 0, 0))
    out_shape = [jax.ShapeDtypeStruct((bsz, s, width), BF16)]
    if emit_vn:
        out_shape.append(jax.ShapeDtypeStruct((bsz, s, width), F32))
    bs_b = jnp.broadcast_to(bs[:, :, None], (GMLP_GROUPS, CHUNK, LANES))
    out = pl.pallas_call(
        _gmlp_kernel,
        grid=(bsz, s // CHUNK),
        in_specs=[pl.BlockSpec((None, CHUNK, 2 * width), lambda b, i: (b, i, 0)), vec, vec, cube, cube],
        out_specs=[blk] * len(out_shape),
        out_shape=out_shape,
        compiler_params=_cparams("parallel", "parallel"),
        name="gmlp",
    )(uv, ln_g.reshape(1, width), ln_b.reshape(1, width), ws, bs_b)
    return out if emit_vn else (out[0], None)


_GLA_LEVELS = (64, 32, 16, 8, 4, 2, 1)
_N_LEVEL = len(_GLA_LEVELS)


def _gla_tables():
    x = np.arange(CHUNK)[:, None]
    j = np.arange(CHUNK)[None, :]
    groups = []
    lv = np.full((CHUNK, CHUNK), -1, np.int32)
    for i, half in enumerate(_GLA_LEVELS):
        anchor = (x // (2 * half)) * (2 * half) + half - 1
        groups.append((j > np.minimum(x, anchor)) & (j <= np.maximum(x, anchor)))
        same = (x // (2 * half)) == (j // (2 * half))
        upper_t = (x % (2 * half)) >= half
        lower_s = (j % (2 * half)) < half
        lv[same & upper_t & lower_s] = i
    lv[np.arange(CHUNK), np.arange(CHUNK)] = _N_LEVEL
    groups.append(j <= x)
    groups.append(j > x)
    mall = np.concatenate(groups, axis=0).astype(np.float32)
    return jnp.asarray(mall, BF16), jnp.asarray(np.tile(lv, (GLA_HEADS, 1)))


def _gla_kernel(gl_ref, t_ref, w2_ref, b2_ref, ng_ref, s0_ref, mall_ref, lv_ref, o_ref, st_ref, state,
                *, valid_len, n_chunks):
    ci = pl.program_id(1)

    @pl.when(ci == 0)
    def _():
        state[...] = s0_ref[...].T

    gl = gl_ref[...]
    q = gl[:, 0:GLA_KW] * (GLA_DK ** -0.5)
    k = gl[:, GLA_KW:2 * GLA_KW]
    v = gl[:, 2 * GLA_KW:2 * GLA_KW + GLA_WIDTH]
    gate = gl[:, 2 * GLA_KW + GLA_WIDTH:]
    la = _log_sigmoid(_dot(t_ref[...].astype(BF16), w2_ref[...].astype(BF16)) + b2_ref[...]) * (1.0 / GLA_TAU)
    if valid_len < CHUNK:
        live = lax.broadcasted_iota(jnp.int32, (CHUNK, GLA_KW), 0) < valid_len
        la = jnp.where(live, la, 0.0)
        k = jnp.where(live, k, 0.0)

    la_hi = la.astype(BF16)
    la_lo = (la - la_hi.astype(F32)).astype(BF16)
    mall = mall_ref[...]
    w = jnp.exp(_dot(mall, la_hi) + _dot(mall, la_lo))

    lane_head = lax.broadcasted_iota(jnp.int32, (CHUNK, GLA_KW), 1) // GLA_DK
    heads = [lane_head == h for h in range(GLA_HEADS)]
    q_stack = jnp.concatenate([jnp.where(hm, q, 0.0) for hm in heads], axis=0)
    lv = lv_ref[...]

    def level_w(i):
        return w[i * CHUNK:(i + 1) * CHUNK]

    a = jnp.zeros((GLA_HEADS * CHUNK, CHUNK), F32)
    for i in range(_N_LEVEL):
        wi = level_w(i)
        qi = q_stack * jnp.concatenate([wi] * GLA_HEADS, axis=0)
        a = jnp.where(lv == i, _dot_nt(qi.astype(BF16), (k * wi).astype(BF16)), a)
    a = jnp.where(lv == _N_LEVEL, _dot_nt(q_stack.astype(BF16), k.astype(BF16)), a)

    w_pre = level_w(_N_LEVEL)
    w_suf = level_w(_N_LEVEL + 1)
    st = state[...]
    q_pre = q_stack * jnp.concatenate([w_pre] * GLA_HEADS, axis=0)
    o_inter = _dot_nt(q_pre.astype(BF16), st.astype(BF16))
    v_b = v.astype(BF16)
    for h in range(GLA_HEADS):
        rows = slice(h * CHUNK, (h + 1) * CHUNK)
        cols = slice(h * LANES, (h + 1) * LANES)
        o = _dot(a[rows].astype(BF16), v_b[:, cols]) + o_inter[rows]
        o = o * lax.rsqrt(jnp.mean(o * o, axis=-1, keepdims=True) + LN_EPS)
        o_ref[:, cols] = (o * ng_ref[:, cols] * _silu(gate[:, cols])).astype(o_ref.dtype)

    kvt = _dot(v.T.astype(BF16), (k * w_suf).astype(BF16))
    upd = jnp.zeros((LANES, GLA_KW), F32)
    for h in range(GLA_HEADS):
        upd = upd + jnp.where(heads[h], kvt[h * LANES:(h + 1) * LANES], 0.0)
    st_new = st * w_pre[CHUNK - 1:CHUNK, :] + upd
    state[...] = st_new

    @pl.when(ci == n_chunks - 1)
    def _():
        st_ref[...] = st_new.T


def _gla(gl, tail, w2, b2, norm_g, s0, tables, *, valid_len=CHUNK):
    bsz, s, width = gl.shape
    n_chunks = s // CHUNK
    mall, lv = tables
    w2pad = jnp.zeros((LANES, GLA_KW), F32).at[N_FOX_HEADS:N_FOX_HEADS + GLA_RANK].set(w2)
    const2 = lambda b, i: (0, 0)
    o, st = pl.pallas_call(
        functools.partial(_gla_kernel, valid_len=valid_len, n_chunks=n_chunks),
        grid=(bsz, n_chunks),
        in_specs=[pl.BlockSpec((None, CHUNK, width), lambda b, i: (b, i, 0)),
                  pl.BlockSpec((None, CHUNK, LANES), lambda b, i: (b, i, 0)),
                  pl.BlockSpec((LANES, GLA_KW), const2),
                  pl.BlockSpec((1, GLA_KW), const2),
                  pl.BlockSpec((1, GLA_WIDTH), const2),
                  pl.BlockSpec((None, GLA_KW, LANES), lambda b, i: (b, 0, 0)),
                  pl.BlockSpec(mall.shape, const2),
                  pl.BlockSpec(lv.shape, const2)],
        out_specs=[pl.BlockSpec((None, CHUNK, GLA_WIDTH), lambda b, i: (b, i, 0)),
                   pl.BlockSpec((None, GLA_KW, LANES), lambda b, i: (b, 0, 0))],
        out_shape=[jax.ShapeDtypeStruct((bsz, s, GLA_WIDTH), BF16),
                   jax.ShapeDtypeStruct((bsz, GLA_KW, LANES), F32)],
        scratch_shapes=[pltpu.VMEM((LANES, GLA_KW), F32)],
        compiler_params=_cparams("parallel", "arbitrary"),
        name="gla",
    )(gl, tail, w2pad, b2.reshape(1, GLA_KW), norm_g.reshape(1, GLA_WIDTH),
      s0.reshape(bsz, GLA_KW, LANES), mall, lv)
    return o, st.reshape(bsz, GLA_HEADS, GLA_DK, LANES)


PAGES_PER_STEP = 8


def _dec_tables():
    rc = np.arange(N_FOX_HEADS * LANES)
    tok = 16 * (rc // LANES) + (rc % LANES) // N_FOX_HEADS
    t = np.arange(PAGE)[None, :]
    after = (tok[:, None] > t).astype(np.float32)
    return jnp.asarray(after, BF16)


def _dec_bias_kernel(pt_ref, *refs, n_steps):
    f_refs = refs[:PAGES_PER_STEP]
    new_ref, after_ref, o_ref, carry = refs[PAGES_PER_STEP:]
    i = pl.program_id(1)

    @pl.when(i == 0)
    def _():
        carry[...] = new_ref[...]

    lane_head = lax.broadcasted_iota(jnp.int32, (N_FOX_HEADS, LANES), 1) % N_FOX_HEADS
    mine = lane_head == lax.broadcasted_iota(jnp.int32, (N_FOX_HEADS, LANES), 0)
    rows = []
    for f_ref in f_refs:
        g = f_ref[...]
        rows.append(jnp.concatenate(
            [jnp.where(mine, jnp.broadcast_to(g[r:r + 1, :], (N_FOX_HEADS, LANES)), 0.0)
             for r in range(N_FOX_HEADS)], axis=1))
    lhs = jnp.concatenate(rows, axis=0)
    after = after_ref[...]
    suffix = _dot_exact(lhs, after)
    total = _dot_exact(lhs, jnp.ones((N_FOX_HEADS * LANES, LANES), BF16))
    run = carry[...]
    for j in reversed(range(PAGES_PER_STEP)):
        sl = slice(j * N_FOX_HEADS, (j + 1) * N_FOX_HEADS)
        o_ref[:, j * PAGE:(j + 1) * PAGE] = suffix[sl] + run
        run = run + total[sl]
    carry[...] = run


def _dec_bias(page_table, logf_pool, layer, logf_new_b, after):
    bd, n_pages = page_table.shape
    n_steps = n_pages // PAGES_PER_STEP
    pool_view = logf_pool.reshape(logf_pool.shape[0], logf_pool.shape[1], N_FOX_HEADS, LANES)

    def page_spec(j):
        return pl.BlockSpec((None, None, N_FOX_HEADS, LANES),
                            lambda b, i, pt: (layer, pt[b, (n_steps - 1 - i) * PAGES_PER_STEP + j], 0, 0))

    grid_spec = pltpu.PrefetchScalarGridSpec(
        num_scalar_prefetch=1,
        grid=(bd, n_steps),
        in_specs=[page_spec(j) for j in range(PAGES_PER_STEP)]
        + [pl.BlockSpec((None, N_FOX_HEADS, LANES), lambda b, i, pt: (b, 0, 0)),
           pl.BlockSpec(after.shape, lambda b, i, pt: (0, 0))],
        out_specs=pl.BlockSpec((None, N_FOX_HEADS, PAGES_PER_STEP * PAGE), lambda b, i, pt: (b, 0, n_steps - 1 - i)),
        scratch_shapes=[pltpu.VMEM((N_FOX_HEADS, LANES), F32)],
    )
    return pl.pallas_call(
        functools.partial(_dec_bias_kernel, n_steps=n_steps),
        grid_spec=grid_spec,
        out_shape=jax.ShapeDtypeStruct((bd, N_FOX_HEADS, n_pages * PAGE), F32),
        compiler_params=_cparams("parallel", "arbitrary"),
        name="fox_dec_bias",
    )(page_table, *([pool_view] * PAGES_PER_STEP), logf_new_b, after)


def _dec_kernel(pt_ref, *refs, n_steps):
    k_refs = refs[:PAGES_PER_STEP]
    v_refs = refs[PAGES_PER_STEP:2 * PAGES_PER_STEP]
    bias_ref, q_ref, kn_ref, vn_ref, o_ref, m_sc, l_sc, acc_sc = refs[2 * PAGES_PER_STEP:]
    i = pl.program_id(1)
    scale = HEAD_DIM ** -0.5
    q = q_ref[...]

    @pl.when(i == 0)
    def _():
        s_self = jnp.sum(q.astype(BF16).astype(F32) * kn_ref[...].astype(BF16).astype(F32), axis=-1, keepdims=True) * scale
        m_sc[...] = jnp.broadcast_to(s_self, m_sc.shape)
        l_sc[...] = jnp.ones(l_sc.shape, F32)
        acc_sc[...] = vn_ref[...].astype(BF16).astype(F32)

    row = lax.broadcasted_iota(jnp.int32, (N_FOX_HEADS, LANES), 0)
    q_b = q.astype(BF16)
    q_rows = [jnp.where(row == h, q_b, jnp.zeros_like(q_b)) for h in range(N_FOX_HEADS)]
    scores = []
    for k_ref in k_refs:
        s = jnp.zeros((N_FOX_HEADS, PAGE), F32)
        for h in range(N_FOX_HEADS):
            k_h = k_ref[pl.ds(h, PAGE, stride=N_FOX_HEADS), :]
            s = s + _dot_nt(q_rows[h], k_h.astype(BF16))
        scores.append(s)
    s = jnp.concatenate(scores, axis=1) * scale + bias_ref[...]
    m_prev = m_sc[...]
    m_new = jnp.maximum(m_prev, jnp.max(s, axis=1, keepdims=True))
    alpha = jnp.exp(m_prev - m_new)
    p = jnp.exp(s - jnp.concatenate([m_new] * PAGES_PER_STEP, axis=1))
    l_new = alpha * l_sc[...] + jnp.sum(p, axis=1, keepdims=True)
    acc = alpha * acc_sc[...]
    p_b = p.astype(BF16)
    for j, v_ref in enumerate(v_refs):
        p_j = p_b[:, j * PAGE:(j + 1) * PAGE]
        for h in range(N_FOX_HEADS):
            v_h = v_ref[pl.ds(h, PAGE, stride=N_FOX_HEADS), :]
            acc = acc + _dot(jnp.where(row == h, p_j, jnp.zeros_like(p_j)), v_h.astype(BF16))
    m_sc[...] = m_new
    l_sc[...] = l_new
    acc_sc[...] = acc

    @pl.when(i == n_steps - 1)
    def _():
        o_ref[...] = acc / l_new


def _fox_decode(page_table, k_pool, v_pool, layer, bias, q, k_new, v_new):
    bd, n_pages = page_table.shape
    n_steps = n_pages // PAGES_PER_STEP
    depth, n_pool = k_pool.shape[:2]
    k_view = k_pool.reshape(depth, n_pool, PAGE * N_FOX_HEADS, HEAD_DIM)
    v_view = v_pool.reshape(depth, n_pool, PAGE * N_FOX_HEADS, HEAD_DIM)

    def page_spec(j):
        return pl.BlockSpec((None, None, PAGE * N_FOX_HEADS, HEAD_DIM),
                            lambda b, i, pt: (layer, pt[b, i * PAGES_PER_STEP + j], 0, 0))

    vec = pl.BlockSpec((None, N_FOX_HEADS, HEAD_DIM), lambda b, i, pt: (b, 0, 0))
    grid_spec = pltpu.PrefetchScalarGridSpec(
        num_scalar_prefetch=1,
        grid=(bd, n_steps),
        in_specs=[page_spec(j) for j in range(PAGES_PER_STEP)] * 2
        + [pl.BlockSpec((None, N_FOX_HEADS, PAGES_PER_STEP * PAGE), lambda b, i, pt: (b, 0, i)), vec, vec, vec],
        out_specs=vec,
        scratch_shapes=[pltpu.VMEM((N_FOX_HEADS, LANES), F32)] * 3,
    )
    return pl.pallas_call(
        functools.partial(_dec_kernel, n_steps=n_steps),
        grid_spec=grid_spec,
        out_shape=jax.ShapeDtypeStruct((bd, N_FOX_HEADS, HEAD_DIM), F32),
        compiler_params=_cparams("parallel", "arbitrary"),
        name="fox_decode",
    )(page_table, *([k_view] * PAGES_PER_STEP), *([v_view] * PAGES_PER_STEP), bias, q, k_new, v_new)


def _pack_w_in(w_in):
    d = w_in.shape[1]
    fox = 3 * N_FOX_HEADS * HEAD_DIM
    mu0 = fox + N_FOX_HEADS
    gq0 = mu0 + 2 * GMLP_GROUPS * LANES
    gr0 = gq0 + 2 * GLA_KW + GLA_WIDTH
    gg0 = gr0 + GLA_RANK
    parts = [w_in[:, :, :fox], w_in[:, :, mu0:gr0], w_in[:, :, gg0:gg0 + GLA_WIDTH],
             w_in[:, :, fox:mu0], w_in[:, :, gr0:gg0],
             jnp.zeros((w_in.shape[0], d, LANES - N_FOX_HEADS - GLA_RANK), w_in.dtype)]
    return jnp.concatenate(parts, axis=-1).astype(BF16)


def _pad_rows(x, rows):
    return jnp.pad(x[:, None, :], ((0, 0), (0, rows - 1), (0, 0)))


def kernel(x_prompt, x_sample, cache_k, cache_v, cache_logf, state_gla, page_table, c_prompt, c_sample,
           ln_in_g, ln_in_b, w_ada, b_ada, w_in, b_f, gm_ln_g, gm_ln_b, gm_ws, gm_bs, gla_w2, gla_b2,
           gla_norm_g, w_o, ln1_g, ln1_b, w_gu, w_down, ln2_g, ln2_b):
    bp, seq, d = x_prompt.shape
    bd = x_sample.shape[0]
    depth = w_ada.shape[0]
    assert d == 4 * GLA_WIDTH and x_sample.shape[1] == 1 and bp + bd <= 16
    assert w_in.shape[2] == COL_END - (LANES - N_FOX_HEADS - GLA_RANK)
    alpha = (2 * depth) ** 0.25
    m_p = bp * seq
    srows = 16

    w_in_p = _pack_w_in(w_in)
    w_o_b = w_o.astype(BF16)
    w_gu_b = w_gu.astype(BF16)
    w_down_b = w_down.astype(BF16)
    gla_tables = _gla_tables()
    dec_after = _dec_tables()

    c16 = jnp.zeros((16, d), F32).at[:bp].set(c_prompt).at[bp:bp + bd].set(c_sample)
    mods = _ada(c16, w_ada, b_ada)

    def mod_of(l, which, sample):
        v = mods[l, :, which * d:(which + 1) * d]
        return (v[bp:bp + bd] if sample else v[:bp])[:, None, :]

    def dense(h, l, m_rows):
        tm = 512
        (q,) = _matmul(h, w_in_p, l, COL_Q, 1024, [BF16 if m_rows > srows else F32], tm=tm, tn=512)
        k32, k16 = _matmul(h, w_in_p, l, COL_K, 1024, [F32, BF16], tm=tm, tn=512)
        v32, v16 = _matmul(h, w_in_p, l, COL_V, 1024, [F32, BF16], tm=tm, tn=512)
        (uv,) = _matmul(h, w_in_p, l, COL_M, 1024, [F32], tm=tm, tn=512)
        (gl,) = _matmul(h, w_in_p, l, COL_G, 1536, [F32], tm=tm, tn=512)
        (tail,) = _matmul(h, w_in_p, l, COL_T, 128, [F32], tm=tm, tn=128)
        return q, k32, k16, v32, v16, uv, gl, tail

    def ffn(x1, h1, l, gate2, mod_next, tr):
        bsz, s, _ = x1.shape
        act = _swiglu(h1.reshape(bsz * s, d) if s > 1 else _rows16(h1), w_gu_b, l, tm=512, tn=512)
        (y2,) = _matmul(act, w_down_b, l, 0, d, [F32], tm=512, tn=512)
        y2 = y2.reshape(bsz, s, d) if s > 1 else y2[:bsz, None, :]
        return _ln_mod(x1, ln2_g[l], ln2_b[l], res=(y2, gate2), mod=mod_next, alpha=alpha, tr=tr)

    def _rows16(h):
        return jnp.pad(h[:, 0, :], ((0, srows - h.shape[0]), (0, 0)))

    xp, hp = _ln_mod(x_prompt, ln_in_g, ln_in_b, mod=(mod_of(0, 1, False), mod_of(0, 0, False)), tr=256)
    xs, hs = _ln_mod(x_sample, ln_in_g, ln_in_b, mod=(mod_of(0, 1, True), mod_of(0, 0, True)), tr=1)

    zero_state = jnp.zeros((bp, GLA_HEADS, GLA_DK, LANES), F32)
    kp, vp, fp, sp = [], [], [], []
    kd, vd, fd, sd, gd = [], [], [], [], []
    for l in range(depth):
        last = l == depth - 1
        nxt_p = None if last else (mod_of(l + 1, 1, False), mod_of(l + 1, 0, False))
        nxt_s = None if last else (mod_of(l + 1, 1, True), mod_of(l + 1, 0, True))

        q, k32, k16, v32, v16, uv, gl, tail = dense(hp.reshape(m_p, d), l, m_p)
        tail3 = tail.reshape(bp, seq, LANES)
        logf, cum = _fox_cum(tail3, b_f[l])
        kv16 = jnp.concatenate([k16, v16], axis=-1).reshape(bp, seq, 2 * N_FOX_HEADS * HEAD_DIM)
        fox_o = _fox_prompt(q.reshape(bp, seq, -1), kv16, cum)
        gm_o, _ = _gmlp(uv.reshape(bp, seq, -1), gm_ln_g[l], gm_ln_b[l], gm_ws[l], gm_bs[l], emit_vn=False)
        gla_o, st = _gla(gl.reshape(bp, seq, -1), tail3, gla_w2[l], gla_b2[l], gla_norm_g[l], zero_state, gla_tables)
        mix = jnp.concatenate([fox_o, gm_o, gla_o], axis=-1).reshape(m_p, d)
        (y1,) = _matmul(mix, w_o_b, l, 0, d, [F32], tm=512, tn=512)
        x1, h1 = _ln_mod(xp, ln1_g[l], ln1_b[l], res=(y1.reshape(bp, seq, d), mod_of(l, 2, False)),
                         mod=(mod_of(l, 4, False), mod_of(l, 3, False)), alpha=alpha, tr=256)
        xp, hp = ffn(x1, h1, l, mod_of(l, 5, False), nxt_p, 256)
        kp.append(k32.reshape(bp, seq, N_FOX_HEADS, HEAD_DIM))
        vp.append(v32.reshape(bp, seq, N_FOX_HEADS, HEAD_DIM))
        fp.append(logf)
        sp.append(st)

        q, k32, _, v32, _, uv, gl, tail = dense(_rows16(hs), l, srows)
        heads3 = lambda a: a[:bd].reshape(bd, N_FOX_HEADS, HEAD_DIM)
        tail_s = _pad_rows(tail[:bd], CHUNK)
        logf_s, _ = _fox_cum(tail_s, b_f[l])
        logf_s = logf_s[:, 0, :]
        bias = _dec_bias(page_table, cache_logf, l,
                         jnp.broadcast_to(logf_s[:, :, None], (bd, N_FOX_HEADS, LANES)), dec_after)
        fox_s = _fox_decode(page_table, cache_k, cache_v, l, bias, heads3(q), heads3(k32), heads3(v32))
        gm_s, vn_s = _gmlp(_pad_rows(uv[:bd], CHUNK), gm_ln_g[l], gm_ln_b[l], gm_ws[l], gm_bs[l], emit_vn=True)
        gla_s, st_s = _gla(_pad_rows(gl[:bd], CHUNK), tail_s, gla_w2[l], gla_b2[l], gla_norm_g[l],
                           state_gla[l], gla_tables, valid_len=1)
        mix = jnp.concatenate([fox_s.reshape(bd, -1).astype(BF16), gm_s[:, 0, :], gla_s[:, 0, :]], axis=-1)
        (y1,) = _matmul(jnp.pad(mix, ((0, srows - bd), (0, 0))), w_o_b, l, 0, d, [F32], tm=512, tn=512)
        x1, h1 = _ln_mod(xs, ln1_g[l], ln1_b[l], res=(y1[:bd, None, :], mod_of(l, 2, True)),
                         mod=(mod_of(l, 4, True), mod_of(l, 3, True)), alpha=alpha, tr=1)
        xs, hs = ffn(x1, h1, l, mod_of(l, 5, True), nxt_s, 1)
        kd.append(heads3(k32)[:, None])
        vd.append(heads3(v32)[:, None])
        fd.append(logf_s[:, None, :])
        sd.append(st_s)
        gd.append(vn_s[:, 0:1, :])

    return (xp, xs, jnp.stack(kp), jnp.stack(vp), jnp.stack(fp), jnp.stack(sp),
            jnp.stack(kd), jnp.stack(vd), jnp.stack(fd), jnp.stack(sd), jnp.stack(gd))
```

```python
import functools
import math

import numpy as np
import jax
import jax.numpy as jnp
from jax import lax
from jax.experimental import pallas as pl
from jax.experimental.pallas import tpu as pltpu

F32 = jnp.float32
BF16 = jnp.bfloat16

LANES = 128
HEAD_DIM = 128
N_FOX_HEADS = 8
FOX_WIDTH = N_FOX_HEADS * HEAD_DIM
GMLP_GROUPS = 4
GMLP_WIDTH = GMLP_GROUPS * LANES
GLA_HEADS = 4
GLA_DK = 64
GLA_KW = GLA_HEADS * GLA_DK
GLA_WIDTH = GLA_HEADS * LANES
GLA_RANK = 16
GLA_TAU = 16.0
CHUNK = 128
PAGE = 128
LN_EPS = 1e-5
LOG2E = math.log2(math.e)
VMEM_LIMIT = 56 * 1024 * 1024
SROWS = 16

TN = 512
OFF_FF = 3 * FOX_WIDTH
OFF_MU = OFF_FF + N_FOX_HEADS
OFF_GR = OFF_MU + 2 * GMLP_WIDTH + 2 * GLA_KW + GLA_WIDTH
OFF_GG = OFF_GR + GLA_RANK
IN_WIDTH = OFF_GG + GLA_WIDTH
N_PROJ_TILES = 12
UG_WIDTH = 5 * TN


def _cparams(*sem):
    return pltpu.CompilerParams(dimension_semantics=sem, vmem_limit_bytes=VMEM_LIMIT)


def _log_sigmoid(x):
    return jnp.minimum(x, 0.0) - jnp.log1p(jnp.exp(-jnp.abs(x)))


def _silu(x):
    return x * (1.0 / (1.0 + jnp.exp(-x)))


def _split3(x):
    hi = x.astype(BF16)
    r = x - hi.astype(F32)
    mid = r.astype(BF16)
    lo = (r - mid.astype(F32)).astype(BF16)
    return hi, mid, lo


def _dot(a, b):
    return jnp.dot(a, b, preferred_element_type=F32)


def _dot_nt(a, b):
    return lax.dot_general(a, b, (((1,), (1,)), ((), ())), preferred_element_type=F32)


def _dot_exact(x, w):
    hi, mid, lo = _split3(x)
    return _dot(hi, w) + _dot(mid, w) + _dot(lo, w)


def _dot_exact_l(w, x):
    hi, mid, lo = _split3(x)
    return _dot(w, hi) + _dot(w, mid) + _dot(w, lo)


def _ada_kernel(c_ref, w_ref, b_ref, o_ref):
    a = _silu(c_ref[...]).astype(BF16)
    o_ref[...] = _dot(a, w_ref[...].astype(BF16)) + b_ref[...]


def _ada(c16, w_ada, b_ada):
    depth, d, n = w_ada.shape
    tn = 1024
    return pl.pallas_call(
        _ada_kernel,
        grid=(depth, n // tn),
        in_specs=[pl.BlockSpec((16, d), lambda l, j: (0, 0)),
                  pl.BlockSpec((None, d, tn), lambda l, j: (l, 0, j)),
                  pl.BlockSpec((None, 1, tn), lambda l, j: (l, 0, j))],
        out_specs=pl.BlockSpec((None, 16, tn), lambda l, j: (l, 0, j)),
        out_shape=jax.ShapeDtypeStruct((depth, 16, n), F32),
        compiler_params=_cparams("parallel", "parallel"),
        name="ada",
    )(c16, w_ada, b_ada.reshape(depth, 1, n))


def _ln_kernel(*refs, has_res, has_mod, alpha):
    it = iter(refs)
    x = next(it)[...]
    if has_res:
        y = next(it)[...]
        gate = next(it)[...]
        x = alpha * x + gate * y
    g = next(it)[...]
    b = next(it)[...]
    if has_mod:
        sc = next(it)[...]
        sh = next(it)[...]
    xn_ref = next(it)
    mu = jnp.mean(x, axis=-1, keepdims=True)
    xc = x - mu
    var = jnp.mean(xc * xc, axis=-1, keepdims=True)
    xn = xc * lax.rsqrt(var + LN_EPS) * g + b
    xn_ref[...] = xn
    if has_mod:
        h_ref = next(it)
        h_ref[...] = (xn * (1.0 + sc) + sh).astype(BF16)


def _ln_mod(x, g, b, *, res=None, mod=None, alpha=1.0, tr):
    bsz, s, d = x.shape
    tr = min(tr, s)
    row = pl.BlockSpec((None, tr, d), lambda i, j: (i, j, 0))
    per_b = pl.BlockSpec((None, 1, d), lambda i, j: (i, 0, 0))
    vec = pl.BlockSpec((1, d), lambda i, j: (0, 0))
    args, specs = [x], [row]
    if res is not None:
        y, gate = res
        args += [y, gate]
        specs += [row, per_b]
    args += [g.reshape(1, d), b.reshape(1, d)]
    specs += [vec, vec]
    out_shape = [jax.ShapeDtypeStruct((bsz, s, d), F32)]
    out_specs = [row]
    if mod is not None:
        sc, sh = mod
        args += [sc, sh]
        specs += [per_b, per_b]
        out_shape.append(jax.ShapeDtypeStruct((bsz, s, d), BF16))
        out_specs.append(row)
    out = pl.pallas_call(
        functools.partial(_ln_kernel, has_res=res is not None, has_mod=mod is not None, alpha=alpha),
        grid=(bsz, s // tr),
        in_specs=specs, out_specs=out_specs, out_shape=out_shape,
        compiler_params=_cparams("parallel", "parallel"),
        name="ln_mod",
    )(*args)
    return out if mod is not None else (out[0], None)


def _proj_kernel(*refs, aliased, q_scale):
    x_ref, xs_ref, wm_ref, wn_ref, wg_ref = refs[:5]
    rest = refs[5 + (2 if aliased else 0):]
    q_ref, k32_ref, k16_ref, v32_ref, vt_ref, ug_ref, t_ref, s_ref, wb = rest
    j = pl.program_id(0)
    i = pl.program_id(1)
    k_dim = wm_ref.shape[0]

    @pl.when(i == 0)
    def _():
        @pl.when(j < 6)
        def _():
            wb[...] = wm_ref[...].astype(BF16)

        def shifted(lanes):
            cat = jnp.concatenate([wm_ref[...], wn_ref[...]], axis=1)
            return pltpu.roll(cat, TN + LANES - lanes, axis=1)[:, :TN].astype(BF16)

        @pl.when((j >= 6) & (j < 10))
        def _():
            wb[...] = shifted(OFF_MU - OFF_FF)

        @pl.when(j == 10)
        def _():
            wb[...] = shifted(OFF_GG - 10 * TN)

        @pl.when(j == N_PROJ_TILES - 1)
        def _():
            lane = lax.broadcasted_iota(jnp.int32, (k_dim, LANES), 1)
            t = jnp.where(lane < N_FOX_HEADS, wn_ref[...],
                          jnp.where(lane < N_FOX_HEADS + GLA_RANK, wg_ref[...], 0.0))
            wb[:, 0:LANES] = t.astype(BF16)

        s_ref[...] = _dot(xs_ref[...], wb[...])

    @pl.when(j < N_PROJ_TILES - 1)
    def _():
        acc = _dot(x_ref[...], wb[...])

        @pl.when(j < 2)
        def _():
            q_ref[...] = (acc * q_scale).astype(BF16)

        @pl.when((j >= 2) & (j < 4))
        def _():
            k32_ref[...] = acc
            k16_ref[...] = acc.astype(BF16)

        @pl.when((j >= 4) & (j < 6))
        def _():
            v32_ref[...] = acc
            vt_ref[...] = acc.T.astype(BF16)

        @pl.when(j >= 6)
        def _():
            ug_ref[...] = acc

    @pl.when(j == N_PROJ_TILES - 1)
    def _():
        t_ref[...] = _dot(x_ref[...], wb[:, 0:LANES])


def _proj(x, xs, w_in, layer, k_prev, v_prev, depth, *, tm):
    m, k = x.shape
    tm = min(tm, m)
    n_i = m // tm
    last_i = n_i - 1

    def frozen(lo, hi, transpose=False, lead=None):
        def index_map(j, i):
            jj = jnp.clip(j, lo, hi - 1) - lo
            ii = jnp.where(j < lo, 0, jnp.where(j >= hi, last_i, i))
            idx = (jj, ii) if transpose else (ii, jj)
            return idx if lead is None else (lead,) + idx
        return index_map

    nb = lambda j: jnp.where((j >= 6) & (j < N_PROJ_TILES - 1), 4 * (j + 1), OFF_FF // LANES)
    in_specs = [pl.BlockSpec((tm, k), lambda j, i: (i, 0)),
                pl.BlockSpec((SROWS, k), lambda j, i: (0, 0)),
                pl.BlockSpec((None, k, TN), lambda j, i: (layer, 0, jnp.minimum(j, 10))),
                pl.BlockSpec((None, k, LANES), lambda j, i: (layer, 0, nb(j))),
                pl.BlockSpec((None, k, LANES), lambda j, i: (layer, 0, (OFF_GR - N_FOX_HEADS) // LANES))]
    args = [x, xs, w_in, w_in, w_in]
    aliases = {}
    if k_prev is not None:
        in_specs += [pl.BlockSpec(memory_space=pl.ANY)] * 2
        args += [k_prev, v_prev]
        aliases = {5: 1, 6: 3}
    out_shape = [jax.ShapeDtypeStruct((m, FOX_WIDTH), BF16),
                 jax.ShapeDtypeStruct((depth, m, FOX_WIDTH), F32),
                 jax.ShapeDtypeStruct((m, FOX_WIDTH), BF16),
                 jax.ShapeDtypeStruct((depth, m, FOX_WIDTH), F32),
                 jax.ShapeDtypeStruct((FOX_WIDTH, m), BF16),
                 jax.ShapeDtypeStruct((m, UG_WIDTH), F32),
                 jax.ShapeDtypeStruct((m, LANES), F32),
                 jax.ShapeDtypeStruct((SROWS, N_PROJ_TILES * TN), F32)]
    out_specs = [pl.BlockSpec((tm, TN), frozen(0, 2)),
                 pl.BlockSpec((None, tm, TN), frozen(2, 4, lead=layer)),
                 pl.BlockSpec((tm, TN), frozen(2, 4)),
                 pl.BlockSpec((None, tm, TN), frozen(4, 6, lead=layer)),
                 pl.BlockSpec((TN, tm), frozen(4, 6, transpose=True)),
                 pl.BlockSpec((tm, TN), frozen(6, 11)),
                 pl.BlockSpec((tm, LANES), frozen(11, 12)),
                 pl.BlockSpec((SROWS, TN), lambda j, i: (0, j))]
    return pl.pallas_call(
        functools.partial(_proj_kernel, aliased=k_prev is not None, q_scale=HEAD_DIM ** -0.5 * LOG2E),
        grid=(N_PROJ_TILES, n_i),
        in_specs=in_specs, out_specs=out_specs, out_shape=out_shape,
        scratch_shapes=[pltpu.VMEM((k, TN), BF16)],
        input_output_aliases=aliases,
        compiler_params=_cparams("arbitrary", "arbitrary"),
        name="proj",
    )(*args)


def _dense_kernel(*refs, n_lhs, k_bounds, swiglu):
    x_refs = refs[:n_lhs]
    xs_ref = refs[n_lhs]
    n_w = 2 if swiglu else 1
    w_refs = refs[n_lhs + 1:n_lhs + 1 + n_w]
    o_ref, os_ref = refs[n_lhs + 1 + n_w:n_lhs + 3 + n_w]
    wbs = refs[n_lhs + 3 + n_w:]

    def product(xs, wb):
        acc = None
        for x, (lo, hi) in zip(xs, k_bounds):
            part = _dot(x, wb[lo:hi, :])
            acc = part if acc is None else acc + part
        return acc

    def result(xs):
        if swiglu:
            return _silu(product(xs, wbs[0])) * product(xs, wbs[1])
        return product(xs, wbs[0])

    @pl.when(pl.program_id(1) == 0)
    def _():
        for w_ref, wb in zip(w_refs, wbs):
            wb[...] = w_ref[...].astype(BF16)
        acc_s = _dot(xs_ref[...], wbs[0][...])
        if swiglu:
            acc_s = _silu(acc_s) * _dot(xs_ref[...], wbs[1][...])
        os_ref[...] = acc_s.astype(os_ref.dtype)

    o_ref[...] = result([x[...] for x in x_refs]).astype(o_ref.dtype)


def _dense(xs_list, x_sample, w, layer, out_dtype, *, tm, tn, swiglu=False):
    m = xs_list[0].shape[0]
    tm = min(tm, m)
    k = w.shape[1]
    n = w.shape[2] // (2 if swiglu else 1)
    bounds, lo = [], 0
    for x in xs_list:
        bounds.append((lo, lo + x.shape[1]))
        lo += x.shape[1]
    assert lo == k and x_sample.shape == (SROWS, k)
    in_specs = [pl.BlockSpec((tm, x.shape[1]), lambda j, i: (i, 0)) for x in xs_list]
    in_specs.append(pl.BlockSpec((SROWS, k), lambda j, i: (0, 0)))
    in_specs.append(pl.BlockSpec((None, k, tn), lambda j, i: (layer, 0, j)))
    w_args = [w]
    if swiglu:
        up0 = n // tn
        in_specs.append(pl.BlockSpec((None, k, tn), lambda j, i: (layer, 0, up0 + j)))
        w_args.append(w)
    return pl.pallas_call(
        functools.partial(_dense_kernel, n_lhs=len(xs_list), k_bounds=tuple(bounds), swiglu=swiglu),
        grid=(n // tn, m // tm),
        in_specs=in_specs,
        out_specs=[pl.BlockSpec((tm, tn), lambda j, i: (i, j)), pl.BlockSpec((SROWS, tn), lambda j, i: (0, j))],
        out_shape=[jax.ShapeDtypeStruct((m, n), out_dtype), jax.ShapeDtypeStruct((SROWS, n), out_dtype)],
        scratch_shapes=[pltpu.VMEM((k, tn), BF16)] * len(w_args),
        compiler_params=_cparams("arbitrary", "arbitrary"),
        name="swiglu" if swiglu else "dense",
    )(*xs_list, x_sample, *w_args)


EXT_PER_HEAD = 6


def _ext_tables():
    pq = np.zeros((3 * LANES, LANES), np.float32)
    pk = np.zeros((3 * LANES, LANES), np.float32)
    oq = np.zeros((1, LANES), np.float32)
    ok = np.zeros((1, LANES), np.float32)
    for h in range(N_FOX_HEADS):
        for part in range(3):
            pq[part * LANES + h, EXT_PER_HEAD * h + part] = 1.0
            pk[part * LANES + h, EXT_PER_HEAD * h + 3 + part] = -1.0
            oq[0, EXT_PER_HEAD * h + 3 + part] = 1.0
            ok[0, EXT_PER_HEAD * h + part] = 1.0
    return jnp.asarray(pq, BF16), jnp.asarray(pk, BF16), jnp.asarray(oq), jnp.asarray(ok)


def _fox_cum_kernel(t_ref, bf_ref, pq_ref, pk_ref, oq_ref, ok_ref, lf_ref, eq_ref, ek_ref, carry, *, tb):
    @pl.when(pl.program_id(1) == 0)
    def _():
        carry[...] = jnp.zeros(carry.shape, F32)

    lf = _log_sigmoid(t_ref[...] + bf_ref[...])
    lf_ref[...] = lf[:, 0:N_FOX_HEADS]
    r = lax.broadcasted_iota(jnp.int32, (tb, tb), 0)
    c = lax.broadcasted_iota(jnp.int32, (tb, tb), 1)
    lower = (c <= r).astype(BF16)
    cum = _dot_exact_l(lower, lf) + carry[0:1, :]
    carry[...] = jnp.broadcast_to(cum[tb - 1:tb, :], carry.shape)
    hi, mid, lo = _split3(cum * LOG2E)
    cat = jnp.concatenate([hi, mid, lo], axis=1)
    eq_ref[...] = (_dot(cat, pq_ref[...]) + oq_ref[...]).astype(BF16)
    ek_ref[...] = (_dot(cat, pk_ref[...]) + ok_ref[...]).astype(BF16)


def _fox_cum(tail, b_f, tables):
    bsz, s, _ = tail.shape
    tb = min(512, s)
    pq, pk, oq, ok = tables
    bf = jnp.zeros((1, LANES), F32).at[0, :N_FOX_HEADS].set(b_f)
    blk = pl.BlockSpec((None, tb, LANES), lambda b, i: (b, i, 0))
    const = lambda shape: pl.BlockSpec(shape, lambda b, i: (0, 0))
    return pl.pallas_call(
        functools.partial(_fox_cum_kernel, tb=tb),
        grid=(bsz, s // tb),
        in_specs=[blk, const((1, LANES)), const(pq.shape), const(pk.shape), const((1, LANES)), const((1, LANES))],
        out_specs=[pl.BlockSpec((None, tb, N_FOX_HEADS), lambda b, i: (b, i, 0)), blk, blk],
        out_shape=[jax.ShapeDtypeStruct((bsz, s, N_FOX_HEADS), F32),
                   jax.ShapeDtypeStruct((bsz, s, LANES), BF16),
                   jax.ShapeDtypeStruct((bsz, s, LANES), BF16)],
        scratch_shapes=[pltpu.VMEM((8, LANES), F32)],
        compiler_params=_cparams("parallel", "arbitrary"),
        name="fox_cum",
    )(tail, bf, pq, pk, oq, ok)


HEADS_PER_STEP = 4


def _fox_kernel(qi_ref, kj_ref, q_ref, k_ref, vt_ref, eq_ref, ek_ref, o_ref, m_sc, l_sc, acc_sc, qx_sc, *, tq, tk):
    grp = pl.program_id(1)
    p = pl.program_id(2)
    qi = qi_ref[p]
    kj = kj_ref[p]

    @pl.when(kj == 0)
    def _():
        m_sc[...] = jnp.full(m_sc.shape, -jnp.inf, F32)
        l_sc[...] = jnp.zeros(l_sc.shape, F32)
        acc_sc[...] = jnp.zeros(acc_sc.shape, F32)
        lane = lax.broadcasted_iota(jnp.int32, (tq, LANES), 1)
        eq = eq_ref[...]
        for hh in range(HEADS_PER_STEP):
            lo = (grp * HEADS_PER_STEP + hh) * EXT_PER_HEAD
            own = (lane >= lo) & (lane < lo + EXT_PER_HEAD)
            qx_sc[hh] = jnp.concatenate([q_ref[:, hh * LANES:(hh + 1) * LANES],
                                         jnp.where(own, eq, jnp.zeros_like(eq))], axis=1)

    def step(diagonal):
        ek = ek_ref[...]
        for hh in range(HEADS_PER_STEP):
            cols = slice(hh * LANES, (hh + 1) * LANES)
            kx = jnp.concatenate([k_ref[:, cols], ek], axis=1)
            s = _dot_nt(kx, qx_sc[hh])
            if diagonal:
                key = lax.broadcasted_iota(jnp.int32, (tk, tq), 0)
                qry = lax.broadcasted_iota(jnp.int32, (tk, tq), 1)
                s = jnp.where(key <= qry, s, -jnp.inf)
            m_prev = m_sc[hh][0:1, :]
            m_new = jnp.maximum(m_prev, jnp.max(s, axis=0, keepdims=True))
            alpha = jnp.exp2(m_prev - m_new)
            pr = jnp.exp2(s - m_new)
            l_new = alpha * l_sc[hh][0:1, :] + jnp.sum(pr, axis=0, keepdims=True)
            acc = alpha * acc_sc[hh] + _dot(vt_ref[cols, :], pr.astype(BF16))
            m_sc[hh] = jnp.broadcast_to(m_new, (8, tq))
            l_sc[hh] = jnp.broadcast_to(l_new, (8, tq))
            acc_sc[hh] = acc
            if diagonal:
                o_ref[:, cols] = (acc / l_new).T.astype(o_ref.dtype)

    @pl.when(kj == qi)
    def _():
        step(True)

    @pl.when(kj != qi)
    def _():
        step(False)


def _fox_prompt(q, k, vt, ext_q, ext_k):
    bsz, s, _ = q.shape
    tq = tk = min(512, s)
    nq = s // tq
    width = HEADS_PER_STEP * LANES
    pairs = [(i, j) for i in range(nq) for j in range(i + 1)]
    qi_tbl = jnp.asarray([pr[0] for pr in pairs], jnp.int32)
    kj_tbl = jnp.asarray([pr[1] for pr in pairs], jnp.int32)
    grid_spec = pltpu.PrefetchScalarGridSpec(
        num_scalar_prefetch=2,
        grid=(bsz, N_FOX_HEADS // HEADS_PER_STEP, len(pairs)),
        in_specs=[pl.BlockSpec((None, tq, width), lambda b, g, p, qi, kj: (b, qi[p], g)),
                  pl.BlockSpec((None, tk, width), lambda b, g, p, qi, kj: (b, kj[p], g)),
                  pl.BlockSpec((width, tk), lambda b, g, p, qi, kj: (g, b * nq + kj[p])),
                  pl.BlockSpec((None, tq, LANES), lambda b, g, p, qi, kj: (b, qi[p], 0)),
                  pl.BlockSpec((None, tk, LANES), lambda b, g, p, qi, kj: (b, kj[p], 0))],
        out_specs=pl.BlockSpec((None, tq, width), lambda b, g, p, qi, kj: (b, qi[p], g)),
        scratch_shapes=[pltpu.VMEM((HEADS_PER_STEP, 8, tq), F32),
                        pltpu.VMEM((HEADS_PER_STEP, 8, tq), F32),
                        pltpu.VMEM((HEADS_PER_STEP, LANES, tq), F32),
                        pltpu.VMEM((HEADS_PER_STEP, tq, 2 * LANES), BF16)],
    )
    return pl.pallas_call(
        functools.partial(_fox_kernel, tq=tq, tk=tk),
        grid_spec=grid_spec,
        out_shape=jax.ShapeDtypeStruct((bsz, s, FOX_WIDTH), BF16),
        compiler_params=_cparams("parallel", "parallel", "arbitrary"),
        name="fox_prompt",
    )(qi_tbl, kj_tbl, q, k, vt, ext_q, ext_k)


def _gmlp_kernel(uv_ref, g_ref, b_ref, ws_ref, bs_ref, o_ref, *vn_ref):
    uv = uv_ref[...]
    r = lax.broadcasted_iota(jnp.int32, (CHUNK, CHUNK), 0)
    c = lax.broadcasted_iota(jnp.int32, (CHUNK, CHUNK), 1)
    for grp in range(GMLP_GROUPS):
        lo = grp * LANES
        u = jax.nn.gelu(uv[:, lo:lo + LANES])
        v = jax.nn.gelu(uv[:, GMLP_WIDTH + lo:GMLP_WIDTH + lo + LANES])
        mu = jnp.mean(v, axis=-1, keepdims=True)
        vc = v - mu
        var = jnp.mean(vc * vc, axis=-1, keepdims=True)
        vn = vc * lax.rsqrt(var + LN_EPS) * g_ref[:, lo:lo + LANES] + b_ref[:, lo:lo + LANES]
        w = jnp.where(c <= r, ws_ref[grp], 0.0).astype(BF16)
        mixed = _dot(w, vn.astype(BF16)) + bs_ref[grp]
        o_ref[:, lo:lo + LANES] = (u * mixed).astype(o_ref.dtype)
        if vn_ref:
            vn_ref[0][:, lo:lo + LANES] = vn


def _gmlp(ug, ln_g, ln_b, ws, bs, *, emit_vn):
    bsz, s, _ = ug.shape
    blk = pl.BlockSpec((None, CHUNK, GMLP_WIDTH), lambda b, i: (b, i, 0))
    vec = pl.BlockSpec((1, GMLP_WIDTH), lambda b, i: (0, 0))
    cube = pl.BlockSpec((GMLP_GROUPS, CHUNK, CHUNK), lambda b, i: (0, 0, 0))
    out_shape = [jax.ShapeDtypeStruct((bsz, s, GMLP_WIDTH), BF16)]
    if emit_vn:
        out_shape.append(jax.ShapeDtypeStruct((bsz, s, GMLP_WIDTH), F32))
    bs_b = jnp.broadcast_to(bs[:, :, None], (GMLP_GROUPS, CHUNK, LANES))
    out = pl.pallas_call(
        _gmlp_kernel,
        grid=(bsz, s // CHUNK),
        in_specs=[pl.BlockSpec((None, CHUNK, 2 * GMLP_WIDTH), lambda b, i: (b, i, 0)), vec, vec, cube, cube],
        out_specs=[blk] * len(out_shape),
        out_shape=out_shape,
        compiler_params=_cparams("parallel", "parallel"),
        name="gmlp",
    )(ug, ln_g.reshape(1, GMLP_WIDTH), ln_b.reshape(1, GMLP_WIDTH), ws, bs_b)
    return out if emit_vn else (out[0], None)


_GLA_LEVELS = (64, 32, 16, 8, 4, 2, 1)
_N_LEVEL = len(_GLA_LEVELS)


def _gla_tables():
    x = np.arange(CHUNK)[:, None]
    j = np.arange(CHUNK)[None, :]
    groups = []
    lv = np.full((CHUNK, CHUNK), -1, np.int32)
    for i, half in enumerate(_GLA_LEVELS):
        anchor = (x // (2 * half)) * (2 * half) + half - 1
        groups.append((j > np.minimum(x, anchor)) & (j <= np.maximum(x, anchor)))
        same = (x // (2 * half)) == (j // (2 * half))
        upper_t = (x % (2 * half)) >= half
        lower_s = (j % (2 * half)) < half
        lv[same & upper_t & lower_s] = i
    lv[np.arange(CHUNK), np.arange(CHUNK)] = _N_LEVEL
    groups.append(j <= x)
    groups.append(j > x)
    mall = np.concatenate(groups, axis=0).astype(np.float32)
    return jnp.asarray(mall, BF16), jnp.asarray(np.tile(lv, (GLA_HEADS, 1)))


def _gla_kernel(qk_ref, v_ref, g_ref, t_ref, w2_ref, b2_ref, ng_ref, s0_ref, mall_ref, lv_ref, o_ref, st_ref, state,
                *, valid_len, n_chunks):
    ci = pl.program_id(1)

    @pl.when(ci == 0)
    def _():
        state[...] = s0_ref[...].T

    qk = qk_ref[...]
    q = qk[:, 0:GLA_KW] * (GLA_DK ** -0.5)
    k = qk[:, GLA_KW:2 * GLA_KW]
    v = v_ref[...]
    gate = g_ref[...]
    la = _log_sigmoid(_dot(t_ref[...].astype(BF16), w2_ref[...].astype(BF16)) + b2_ref[...]) * (1.0 / GLA_TAU)
    if valid_len < CHUNK:
        live = lax.broadcasted_iota(jnp.int32, (CHUNK, GLA_KW), 0) < valid_len
        la = jnp.where(live, la, 0.0)
        k = jnp.where(live, k, 0.0)

    la_hi = la.astype(BF16)
    la_lo = (la - la_hi.astype(F32)).astype(BF16)
    mall = mall_ref[...]
    w = jnp.exp(_dot(mall, la_hi) + _dot(mall, la_lo))

    lane_head = lax.broadcasted_iota(jnp.int32, (CHUNK, GLA_KW), 1) // GLA_DK
    heads = [lane_head == h for h in range(GLA_HEADS)]
    q_stack = jnp.concatenate([jnp.where(hm, q, 0.0) for hm in heads], axis=0)
    lv = lv_ref[...]

    def level_w(i):
        return w[i * CHUNK:(i + 1) * CHUNK]

    a = jnp.zeros((GLA_HEADS * CHUNK, CHUNK), F32)
    for i in range(_N_LEVEL):
        wi = level_w(i)
        qi = q_stack * jnp.concatenate([wi] * GLA_HEADS, axis=0)
        a = jnp.where(lv == i, _dot_nt(qi.astype(BF16), (k * wi).astype(BF16)), a)
    a = jnp.where(lv == _N_LEVEL, _dot_nt(q_stack.astype(BF16), k.astype(BF16)), a)

    w_pre = level_w(_N_LEVEL)
    w_suf = level_w(_N_LEVEL + 1)
    st = state[...]
    q_pre = q_stack * jnp.concatenate([w_pre] * GLA_HEADS, axis=0)
    o_inter = _dot_nt(q_pre.astype(BF16), st.astype(BF16))
    v_b = v.astype(BF16)
    for h in range(GLA_HEADS):
        rows = slice(h * CHUNK, (h + 1) * CHUNK)
        cols = slice(h * LANES, (h + 1) * LANES)
        o = _dot(a[rows].astype(BF16), v_b[:, cols]) + o_inter[rows]
        o = o * lax.rsqrt(jnp.mean(o * o, axis=-1, keepdims=True) + LN_EPS)
        o_ref[:, cols] = (o * ng_ref[:, cols] * _silu(gate[:, cols])).astype(o_ref.dtype)

    kvt = _dot(v.T.astype(BF16), (k * w_suf).astype(BF16))
    upd = jnp.zeros((LANES, GLA_KW), F32)
    for h in range(GLA_HEADS):
        upd = upd + jnp.where(heads[h], kvt[h * LANES:(h + 1) * LANES], 0.0)
    st_new = st * w_pre[CHUNK - 1:CHUNK, :] + upd
    state[...] = st_new

    @pl.when(ci == n_chunks - 1)
    def _():
        st_ref[...] = st_new.T


def _gla(ug, tail, w2, b2, norm_g, s0, tables, *, valid_len=CHUNK):
    bsz, s, _ = ug.shape
    n_chunks = s // CHUNK
    mall, lv = tables
    w2pad = jnp.zeros((LANES, GLA_KW), F32).at[N_FOX_HEADS:N_FOX_HEADS + GLA_RANK].set(w2)
    const2 = lambda b, i: (0, 0)
    col = lambda c: pl.BlockSpec((None, CHUNK, TN), lambda b, i: (b, i, c))
    o, st = pl.pallas_call(
        functools.partial(_gla_kernel, valid_len=valid_len, n_chunks=n_chunks),
        grid=(bsz, n_chunks),
        in_specs=[col(2), col(3), col(4),
                  pl.BlockSpec((None, CHUNK, LANES), lambda b, i: (b, i, 0)),
                  pl.BlockSpec((LANES, GLA_KW), const2),
                  pl.BlockSpec((1, GLA_KW), const2),
                  pl.BlockSpec((1, GLA_WIDTH), const2),
                  pl.BlockSpec((None, GLA_KW, LANES), lambda b, i: (b, 0, 0)),
                  pl.BlockSpec(mall.shape, const2),
                  pl.BlockSpec(lv.shape, const2)],
        out_specs=[pl.BlockSpec((None, CHUNK, GLA_WIDTH), lambda b, i: (b, i, 0)),
                   pl.BlockSpec((None, GLA_KW, LANES), lambda b, i: (b, 0, 0))],
        out_shape=[jax.ShapeDtypeStruct((bsz, s, GLA_WIDTH), BF16),
                   jax.ShapeDtypeStruct((bsz, GLA_KW, LANES), F32)],
        scratch_shapes=[pltpu.VMEM((LANES, GLA_KW), F32)],
        compiler_params=_cparams("parallel", "arbitrary"),
        name="gla",
    )(ug, ug, ug, tail, w2pad, b2.reshape(1, GLA_KW), norm_g.reshape(1, GLA_WIDTH),
      s0.reshape(bsz, GLA_KW, LANES), mall, lv)
    return o, st.reshape(bsz, GLA_HEADS, GLA_DK, LANES)


PAGES_PER_STEP = 8
PAGE_ROWS = PAGE * N_FOX_HEADS


def _dec_tables():
    rc = np.arange(PAGE_ROWS)
    tok = 16 * (rc // LANES) + (rc % LANES) // N_FOX_HEADS
    t = np.arange(PAGE)
    after = (tok[:, None] > t[None, :]).astype(np.float32)
    spread = (t[:, None] == (rc // N_FOX_HEADS)[None, :]).astype(np.float32)
    return jnp.asarray(after, BF16), jnp.asarray(spread, BF16)


def _dec_kernel(pt_ref, *refs, n_steps):
    n = PAGES_PER_STEP
    f_refs, k_refs, v_refs = refs[:n], refs[n:2 * n], refs[2 * n:3 * n]
    q_ref, kn_ref, vn_ref, fn_ref, after_ref, spread_ref, o_ref, m_sc, l_sc, acc_sc, carry = refs[3 * n:]
    i = pl.program_id(1)
    scale = HEAD_DIM ** -0.5
    q = q_ref[...]

    @pl.when(i == 0)
    def _():
        s_self = jnp.sum(q.astype(BF16).astype(F32) * kn_ref[...].astype(BF16).astype(F32), axis=-1, keepdims=True) * scale
        m_sc[...] = jnp.broadcast_to(s_self, m_sc.shape)
        l_sc[...] = jnp.ones(l_sc.shape, F32)
        acc_sc[...] = vn_ref[...].astype(BF16).astype(F32)
        carry[...] = fn_ref[...]

    head_row = lax.broadcasted_iota(jnp.int32, (N_FOX_HEADS, LANES), 0)
    mine = lax.broadcasted_iota(jnp.int32, (N_FOX_HEADS, LANES), 1) % N_FOX_HEADS == head_row
    rows = []
    for f_ref in f_refs:
        g = f_ref[...]
        rows.append(jnp.concatenate(
            [jnp.where(mine, jnp.broadcast_to(g[r:r + 1, :], (N_FOX_HEADS, LANES)), 0.0)
             for r in range(N_FOX_HEADS)], axis=1))
    lhs = jnp.concatenate(rows, axis=0)
    suffix = _dot_exact(lhs, after_ref[...])
    total = _dot_exact(lhs, jnp.ones((PAGE_ROWS, LANES), BF16))
    run = carry[...]
    bias = [None] * n
    for j in reversed(range(n)):
        sl = slice(j * N_FOX_HEADS, (j + 1) * N_FOX_HEADS)
        bias[j] = suffix[sl] + run
        run = run + total[sl]
    carry[...] = run
    bias_x = _dot_exact(jnp.concatenate(bias, axis=0), spread_ref[...])

    q_b = q.astype(BF16)
    valid = lax.broadcasted_iota(jnp.int32, (N_FOX_HEADS, PAGE_ROWS), 1) % N_FOX_HEADS == \
        lax.broadcasted_iota(jnp.int32, (N_FOX_HEADS, PAGE_ROWS), 0)
    scores = []
    for j, k_ref in enumerate(k_refs):
        s = _dot_nt(q_b, k_ref[...].astype(BF16)) * scale + bias_x[j * N_FOX_HEADS:(j + 1) * N_FOX_HEADS]
        scores.append(jnp.where(valid, s, -jnp.inf))
    m_prev = m_sc[...]
    m_cur = scores[0]
    for s in scores[1:]:
        m_cur = jnp.maximum(m_cur, s)
    m_new = jnp.maximum(m_prev, jnp.max(m_cur, axis=1, keepdims=True))
    alpha = jnp.exp(m_prev - m_new)
    m_wide = jnp.concatenate([m_new] * (PAGE_ROWS // LANES), axis=1)
    acc = alpha * acc_sc[...]
    l_part = jnp.zeros((N_FOX_HEADS, PAGE_ROWS), F32)
    for s, v_ref in zip(scores, v_refs):
        p = jnp.exp(s - m_wide)
        l_part = l_part + p
        acc = acc + _dot(p.astype(BF16), v_ref[...].astype(BF16))
    l_new = alpha * l_sc[...] + jnp.sum(l_part, axis=1, keepdims=True)
    m_sc[...] = m_new
    l_sc[...] = l_new
    acc_sc[...] = acc

    @pl.when(i == n_steps - 1)
    def _():
        o_ref[...] = acc / l_new


def _fox_decode(page_table, k_pool, v_pool, f_pool, layer, q, k_new, v_new, logf_new_b, tables):
    bd, n_pages = page_table.shape
    n_steps = n_pages // PAGES_PER_STEP
    depth, n_pool = k_pool.shape[:2]
    k_view = k_pool.reshape(depth, n_pool, PAGE_ROWS, HEAD_DIM)
    v_view = v_pool.reshape(depth, n_pool, PAGE_ROWS, HEAD_DIM)
    f_view = f_pool.reshape(depth, n_pool, N_FOX_HEADS, LANES)
    after, spread = tables

    def page_of(b, i, pt, j):
        return pt[b, (n_steps - 1 - i) * PAGES_PER_STEP + j]

    def page_spec(j, rows):
        return pl.BlockSpec((None, None, rows, LANES), lambda b, i, pt: (layer, page_of(b, i, pt, j), 0, 0))

    vec = pl.BlockSpec((None, N_FOX_HEADS, HEAD_DIM), lambda b, i, pt: (b, 0, 0))
    const = lambda a: pl.BlockSpec(a.shape, lambda b, i, pt: (0, 0))
    pages = range(PAGES_PER_STEP)
    grid_spec = pltpu.PrefetchScalarGridSpec(
        num_scalar_prefetch=1,
        grid=(bd, n_steps),
        in_specs=[page_spec(j, N_FOX_HEADS) for j in pages] + [page_spec(j, PAGE_ROWS) for j in pages] * 2
        + [vec, vec, vec, vec, const(after), const(spread)],
        out_specs=vec,
        scratch_shapes=[pltpu.VMEM((N_FOX_HEADS, LANES), F32)] * 4,
    )
    return pl.pallas_call(
        functools.partial(_dec_kernel, n_steps=n_steps),
        grid_spec=grid_spec,
        out_shape=jax.ShapeDtypeStruct((bd, N_FOX_HEADS, HEAD_DIM), F32),
        compiler_params=_cparams("parallel", "arbitrary"),
        name="fox_decode",
    )(page_table, *([f_view] * PAGES_PER_STEP), *([k_view] * PAGES_PER_STEP), *([v_view] * PAGES_PER_STEP),
      q, k_new, v_new, logf_new_b, after, spread)


def _pad_rows(x, rows):
    return jnp.pad(x[:, None, :], ((0, 0), (0, rows - 1), (0, 0)))


def _rows16(x):
    return jnp.pad(x, ((0, SROWS - x.shape[0]), (0, 0)))


def kernel(x_prompt, x_sample, cache_k, cache_v, cache_logf, state_gla, page_table, c_prompt, c_sample,
           ln_in_g, ln_in_b, w_ada, b_ada, w_in, b_f, gm_ln_g, gm_ln_b, gm_ws, gm_bs, gla_w2, gla_b2,
           gla_norm_g, w_o, ln1_g, ln1_b, w_gu, w_down, ln2_g, ln2_b):
    bp, seq, d = x_prompt.shape
    bd = x_sample.shape[0]
    depth = w_ada.shape[0]
    assert d == 4 * GLA_WIDTH and x_sample.shape[1] == 1 and bp + bd <= 16 and bd <= SROWS
    assert w_in.shape[2] == IN_WIDTH
    alpha = (2 * depth) ** 0.25
    m_p = bp * seq

    gla_tables = _gla_tables()
    dec_tables = _dec_tables()
    ext_tables = _ext_tables()

    c16 = jnp.zeros((16, d), F32).at[:bp].set(c_prompt).at[bp:bp + bd].set(c_sample)
    mods = _ada(c16, w_ada, b_ada)

    def mod_of(l, which, sample):
        v = mods[l, :, which * d:(which + 1) * d]
        return (v[bp:bp + bd] if sample else v[:bp])[:, None, :]

    xp, hp = _ln_mod(x_prompt, ln_in_g, ln_in_b, mod=(mod_of(0, 1, False), mod_of(0, 0, False)), tr=256)
    xs, hs = _ln_mod(x_sample, ln_in_g, ln_in_b, mod=(mod_of(0, 1, True), mod_of(0, 0, True)), tr=1)

    zero_state = jnp.zeros((bp, GLA_HEADS, GLA_DK, LANES), F32)
    k_all = v_all = None
    fp, sp = [], []
    kd, vd, fd, sd, gd = [], [], [], [], []
    for l in range(depth):
        last = l == depth - 1
        nxt_p = None if last else (mod_of(l + 1, 1, False), mod_of(l + 1, 0, False))
        nxt_s = None if last else (mod_of(l + 1, 1, True), mod_of(l + 1, 0, True))

        q16, k_all, k16, v_all, vt16, ug, tail, samp = _proj(
            hp.reshape(m_p, d), _rows16(hs[:, 0, :]), w_in, l, k_all, v_all, depth, tm=512)

        samp = samp[:bd]
        heads3 = lambda a: a.reshape(bd, N_FOX_HEADS, HEAD_DIM)
        q_s = heads3(samp[:, 0:FOX_WIDTH])
        k_s = heads3(samp[:, FOX_WIDTH:2 * FOX_WIDTH])
        v_s = heads3(samp[:, 2 * FOX_WIDTH:3 * FOX_WIDTH])
        ug_s = _pad_rows(samp[:, 6 * TN:11 * TN], CHUNK)
        tail_s = _pad_rows(samp[:, 11 * TN:11 * TN + LANES], CHUNK)
        logf_s, _, _ = _fox_cum(tail_s, b_f[l], ext_tables)
        logf_s = logf_s[:, 0, :]
        fox_s = _fox_decode(page_table, cache_k, cache_v, cache_logf, l, q_s, k_s, v_s,
                            jnp.broadcast_to(logf_s[:, :, None], (bd, N_FOX_HEADS, LANES)), dec_tables)
        gm_s, vn_s = _gmlp(ug_s, gm_ln_g[l], gm_ln_b[l], gm_ws[l], gm_bs[l], emit_vn=True)
        gla_s, st_s = _gla(ug_s, tail_s, gla_w2[l], gla_b2[l], gla_norm_g[l], state_gla[l], gla_tables, valid_len=1)
        mix_s = jnp.concatenate([fox_s.reshape(bd, -1).astype(BF16), gm_s[:, 0, :], gla_s[:, 0, :]], axis=-1)

        tail3 = tail.reshape(bp, seq, LANES)
        ug3 = ug.reshape(bp, seq, UG_WIDTH)
        logf, ext_q, ext_k = _fox_cum(tail3, b_f[l], ext_tables)
        fox_o = _fox_prompt(q16.reshape(bp, seq, FOX_WIDTH), k16.reshape(bp, seq, FOX_WIDTH), vt16, ext_q, ext_k)
        gm_o, _ = _gmlp(ug3, gm_ln_g[l], gm_ln_b[l], gm_ws[l], gm_bs[l], emit_vn=False)
        gla_o, st = _gla(ug3, tail3, gla_w2[l], gla_b2[l], gla_norm_g[l], zero_state, gla_tables)

        y1, y1_s = _dense([fox_o.reshape(m_p, -1), gm_o.reshape(m_p, -1), gla_o.reshape(m_p, -1)],
                          _rows16(mix_s), w_o, l, F32, tm=512, tn=512)
        x1, h1 = _ln_mod(xp, ln1_g[l], ln1_b[l], res=(y1.reshape(bp, seq, d), mod_of(l, 2, False)),
                         mod=(mod_of(l, 4, False), mod_of(l, 3, False)), alpha=alpha, tr=256)
        x1_s, h1_s = _ln_mod(xs, ln1_g[l], ln1_b[l], res=(y1_s[:bd, None, :], mod_of(l, 2, True)),
                             mod=(mod_of(l, 4, True), mod_of(l, 3, True)), alpha=alpha, tr=1)
        act, act_s = _dense([h1.reshape(m_p, d)], _rows16(h1_s[:, 0, :]), w_gu, l, BF16, tm=1024, tn=512, swiglu=True)
        y2, y2_s = _dense([act], act_s, w_down, l, F32, tm=1024, tn=256)
        xp, hp = _ln_mod(x1, ln2_g[l], ln2_b[l], res=(y2.reshape(bp, seq, d), mod_of(l, 5, False)),
                         mod=nxt_p, alpha=alpha, tr=256)
        xs, hs = _ln_mod(x1_s, ln2_g[l], ln2_b[l], res=(y2_s[:bd, None, :], mod_of(l, 5, True)),
                         mod=nxt_s, alpha=alpha, tr=1)

        fp.append(logf)
        sp.append(st)
        kd.append(k_s[:, None])
        vd.append(v_s[:, None])
        fd.append(logf_s[:, None, :])
        sd.append(st_s)
        gd.append(vn_s[:, 0:1, :])

    kv_shape = (depth, bp, seq, N_FOX_HEADS, HEAD_DIM)
    return (xp, xs, k_all.reshape(kv_shape), v_all.reshape(kv_shape), jnp.stack(fp), jnp.stack(sp),
            jnp.stack(kd), jnp.stack(vd), jnp.stack(fd), jnp.stack(sd), jnp.stack(gd))
```

```python
import functools
import math

import numpy as np
import jax
import jax.numpy as jnp
from jax import lax
from jax.experimental import pallas as pl
from jax.experimental.pallas import tpu as pltpu

F32 = jnp.float32
BF16 = jnp.bfloat16

LANES = 128
HEAD_DIM = 128
N_FOX_HEADS = 8
FOX_WIDTH = N_FOX_HEADS * HEAD_DIM
GMLP_GROUPS = 4
GMLP_WIDTH = GMLP_GROUPS * LANES
GLA_HEADS = 4
GLA_DK = 64
GLA_KW = GLA_HEADS * GLA_DK
GLA_WIDTH = GLA_HEADS * LANES
GLA_RANK = 16
GLA_TAU = 16.0
CHUNK = 128
PAGE = 128
LN_EPS = 1e-5
LOG2E = math.log2(math.e)
VMEM_LIMIT = 56 * 1024 * 1024
SROWS = 16

TN = 512
OFF_FF = 3 * FOX_WIDTH
OFF_MU = OFF_FF + N_FOX_HEADS
OFF_GR = OFF_MU + 2 * GMLP_WIDTH + 2 * GLA_KW + GLA_WIDTH
OFF_GG = OFF_GR + GLA_RANK
IN_WIDTH = OFF_GG + GLA_WIDTH
N_PROJ_TILES = 12
UG_WIDTH = 5 * TN


def _cparams(*sem):
    return pltpu.CompilerParams(dimension_semantics=sem, vmem_limit_bytes=VMEM_LIMIT)


def _log_sigmoid(x):
    return jnp.minimum(x, 0.0) - jnp.log1p(jnp.exp(-jnp.abs(x)))


def _silu(x):
    return x * (1.0 / (1.0 + jnp.exp(-x)))


def _split3(x):
    hi = x.astype(BF16)
    r = x - hi.astype(F32)
    mid = r.astype(BF16)
    lo = (r - mid.astype(F32)).astype(BF16)
    return hi, mid, lo


def _dot(a, b):
    return jnp.dot(a, b, preferred_element_type=F32)


def _dot_nt(a, b):
    return lax.dot_general(a, b, (((1,), (1,)), ((), ())), preferred_element_type=F32)


def _dot_exact(x, w):
    hi, mid, lo = _split3(x)
    return _dot(hi, w) + _dot(mid, w) + _dot(lo, w)


def _dot_exact_l(w, x):
    hi, mid, lo = _split3(x)
    return _dot(w, hi) + _dot(w, mid) + _dot(w, lo)


def _ada_kernel(c_ref, w_ref, b_ref, o_ref):
    a = _silu(c_ref[...]).astype(BF16)
    o_ref[...] = _dot(a, w_ref[...].astype(BF16)) + b_ref[...]


def _ada(c16, w_ada, b_ada):
    depth, d, n = w_ada.shape
    tn = 1024
    return pl.pallas_call(
        _ada_kernel,
        grid=(depth, n // tn),
        in_specs=[pl.BlockSpec((16, d), lambda l, j: (0, 0)),
                  pl.BlockSpec((None, d, tn), lambda l, j: (l, 0, j)),
                  pl.BlockSpec((None, 1, tn), lambda l, j: (l, 0, j))],
        out_specs=pl.BlockSpec((None, 16, tn), lambda l, j: (l, 0, j)),
        out_shape=jax.ShapeDtypeStruct((depth, 16, n), F32),
        compiler_params=_cparams("parallel", "parallel"),
        name="ada",
    )(c16, w_ada, b_ada.reshape(depth, 1, n))


def _ln_kernel(*refs, has_res, has_mod, alpha):
    it = iter(refs)
    x = next(it)[...]
    if has_res:
        y = next(it)[...]
        gate = next(it)[...]
        x = alpha * x + gate * y
    g = next(it)[...]
    b = next(it)[...]
    if has_mod:
        sc = next(it)[...]
        sh = next(it)[...]
    xn_ref = next(it)
    mu = jnp.mean(x, axis=-1, keepdims=True)
    xc = x - mu
    var = jnp.mean(xc * xc, axis=-1, keepdims=True)
    xn = xc * lax.rsqrt(var + LN_EPS) * g + b
    xn_ref[...] = xn
    if has_mod:
        h_ref = next(it)
        h_ref[...] = (xn * (1.0 + sc) + sh).astype(BF16)


def _ln_mod(x, g, b, *, res=None, mod=None, alpha=1.0, tr):
    bsz, s, d = x.shape
    tr = min(tr, s)
    row = pl.BlockSpec((None, tr, d), lambda i, j: (i, j, 0))
    per_b = pl.BlockSpec((None, 1, d), lambda i, j: (i, 0, 0))
    vec = pl.BlockSpec((1, d), lambda i, j: (0, 0))
    args, specs = [x], [row]
    if res is not None:
        y, gate = res
        args += [y, gate]
        specs += [row, per_b]
    args += [g.reshape(1, d), b.reshape(1, d)]
    specs += [vec, vec]
    out_shape = [jax.ShapeDtypeStruct((bsz, s, d), F32)]
    out_specs = [row]
    if mod is not None:
        sc, sh = mod
        args += [sc, sh]
        specs += [per_b, per_b]
        out_shape.append(jax.ShapeDtypeStruct((bsz, s, d), BF16))
        out_specs.append(row)
    out = pl.pallas_call(
        functools.partial(_ln_kernel, has_res=res is not None, has_mod=mod is not None, alpha=alpha),
        grid=(bsz, s // tr),
        in_specs=specs, out_specs=out_specs, out_shape=out_shape,
        compiler_params=_cparams("parallel", "parallel"),
        name="ln_mod",
    )(*args)
    return out if mod is not None else (out[0], None)


def _pack_in_kernel(wm_ref, wn_ref, wg_ref, o_ref):
    j = pl.program_id(1)
    k_dim = wm_ref.shape[0]

    @pl.when(j < 6)
    def _():
        o_ref[...] = wm_ref[...].astype(BF16)

    def shifted(lanes):
        cat = jnp.concatenate([wm_ref[...], wn_ref[...]], axis=1)
        return pltpu.roll(cat, TN + LANES - lanes, axis=1)[:, :TN].astype(BF16)

    @pl.when((j >= 6) & (j < 10))
    def _():
        o_ref[...] = shifted(OFF_MU - OFF_FF)

    @pl.when(j == 10)
    def _():
        o_ref[...] = shifted(OFF_GG - 10 * TN)

    @pl.when(j == N_PROJ_TILES - 1)
    def _():
        lane = lax.broadcasted_iota(jnp.int32, (k_dim, LANES), 1)
        t = jnp.where(lane < N_FOX_HEADS, wn_ref[...],
                      jnp.where(lane < N_FOX_HEADS + GLA_RANK, wg_ref[...], 0.0))
        o_ref[...] = jnp.concatenate([t.astype(BF16), jnp.zeros((k_dim, TN - LANES), BF16)], axis=1)


def _pack_in(w_in):
    depth, k, _ = w_in.shape
    nb = lambda j: jnp.where((j >= 6) & (j < N_PROJ_TILES - 1), 4 * (j + 1), OFF_FF // LANES)
    return pl.pallas_call(
        _pack_in_kernel,
        grid=(depth, N_PROJ_TILES),
        in_specs=[pl.BlockSpec((None, k, TN), lambda l, j: (l, 0, jnp.minimum(j, 10))),
                  pl.BlockSpec((None, k, LANES), lambda l, j: (l, 0, nb(j))),
                  pl.BlockSpec((None, k, LANES), lambda l, j: (l, 0, (OFF_GR - N_FOX_HEADS) // LANES))],
        out_specs=pl.BlockSpec((None, k, TN), lambda l, j: (l, 0, j)),
        out_shape=jax.ShapeDtypeStruct((depth, k, N_PROJ_TILES * TN), BF16),
        compiler_params=_cparams("parallel", "parallel"),
        name="pack_in",
    )(w_in, w_in, w_in)


def _cast_kernel(w_ref, o_ref):
    o_ref[...] = w_ref[...].astype(BF16)


def _cast(w):
    depth, k, n = w.shape
    blk = pl.BlockSpec((None, k, TN), lambda l, j: (l, 0, j))
    return pl.pallas_call(
        _cast_kernel, grid=(depth, n // TN), in_specs=[blk], out_specs=blk,
        out_shape=jax.ShapeDtypeStruct(w.shape, BF16),
        compiler_params=_cparams("parallel", "parallel"), name="cast",
    )(w)


def _proj_kernel(*refs, aliased, q_scale, tm):
    x_ref, xs_ref, w_ref = refs[:3]
    q_ref, k32_ref, k16_ref, v32_ref, vt_ref, ug_ref, t_ref, s_ref = refs[3 + (2 if aliased else 0):]
    x = x_ref[...]
    q_ref[...] = (_dot(x, w_ref[:, 0:FOX_WIDTH]) * q_scale).astype(BF16)
    k = _dot(x, w_ref[:, FOX_WIDTH:2 * FOX_WIDTH])
    k16_ref[...] = k.astype(BF16)
    v = _dot(x, w_ref[:, 2 * FOX_WIDTH:3 * FOX_WIDTH])
    vt_ref[...] = v.T.astype(BF16)
    for h in range(N_FOX_HEADS):
        cols = slice(h * HEAD_DIM, (h + 1) * HEAD_DIM)
        k32_ref[pl.ds(h, tm, stride=N_FOX_HEADS), :] = k[:, cols]
        v32_ref[pl.ds(h, tm, stride=N_FOX_HEADS), :] = v[:, cols]
    ug_ref[...] = _dot(x, w_ref[:, 6 * TN:11 * TN])
    t_ref[...] = _dot(x, w_ref[:, 11 * TN:11 * TN + LANES])

    @pl.when(pl.program_id(0) == 0)
    def _():
        s_ref[...] = _dot(xs_ref[...], w_ref[...])


def _proj(x, xs, w_in, layer, k_prev, v_prev, depth, *, tm):
    m, k = x.shape
    tm = min(tm, m)
    n_all = w_in.shape[2]
    in_specs = [pl.BlockSpec((tm, k), lambda i: (i, 0)),
                pl.BlockSpec((SROWS, k), lambda i: (0, 0)),
                pl.BlockSpec((None, k, n_all), lambda i: (layer, 0, 0), pipeline_mode=pl.Buffered(1))]
    args = [x, xs, w_in]
    aliases = {}
    if k_prev is not None:
        in_specs += [pl.BlockSpec(memory_space=pl.ANY)] * 2
        args += [k_prev, v_prev]
        aliases = {3: 1, 4: 3}
    rows = pl.BlockSpec((tm, FOX_WIDTH), lambda i: (i, 0))
    pairs = pl.BlockSpec((None, tm * N_FOX_HEADS, HEAD_DIM), lambda i: (layer, i, 0))
    out_shape = [jax.ShapeDtypeStruct((m, FOX_WIDTH), BF16),
                 jax.ShapeDtypeStruct((depth, m * N_FOX_HEADS, HEAD_DIM), F32),
                 jax.ShapeDtypeStruct((m, FOX_WIDTH), BF16),
                 jax.ShapeDtypeStruct((depth, m * N_FOX_HEADS, HEAD_DIM), F32),
                 jax.ShapeDtypeStruct((FOX_WIDTH, m), BF16),
                 jax.ShapeDtypeStruct((m, UG_WIDTH), F32),
                 jax.ShapeDtypeStruct((m, LANES), F32),
                 jax.ShapeDtypeStruct((SROWS, n_all), F32)]
    out_specs = [rows, pairs, rows, pairs,
                 pl.BlockSpec((FOX_WIDTH, tm), lambda i: (0, i)),
                 pl.BlockSpec((tm, UG_WIDTH), lambda i: (i, 0)),
                 pl.BlockSpec((tm, LANES), lambda i: (i, 0)),
                 pl.BlockSpec((SROWS, n_all), lambda i: (0, 0))]
    return pl.pallas_call(
        functools.partial(_proj_kernel, aliased=k_prev is not None, q_scale=HEAD_DIM ** -0.5 * LOG2E, tm=tm),
        grid=(m // tm,),
        in_specs=in_specs, out_specs=out_specs, out_shape=out_shape,
        input_output_aliases=aliases,
        compiler_params=_cparams("arbitrary"),
        name="proj",
    )(*args)


def _swiglu_kernel(x_ref, xs_ref, wg_ref, wu_ref, o_ref, os_ref, wbg, wbu):
    def result(x):
        return (_silu(_dot(x, wbg[...])) * _dot(x, wbu[...])).astype(BF16)

    @pl.when(pl.program_id(1) == 0)
    def _():
        wbg[...] = wg_ref[...].astype(BF16)
        wbu[...] = wu_ref[...].astype(BF16)
        os_ref[...] = result(xs_ref[...])

    o_ref[...] = result(x_ref[...])


def _swiglu(x, x_sample, w_gu, layer, *, tm, tn):
    m, k = x.shape
    tm = min(tm, m)
    n = w_gu.shape[2] // 2
    up0 = n // tn
    return pl.pallas_call(
        _swiglu_kernel,
        grid=(n // tn, m // tm),
        in_specs=[pl.BlockSpec((tm, k), lambda j, i: (i, 0)),
                  pl.BlockSpec((SROWS, k), lambda j, i: (0, 0)),
                  pl.BlockSpec((None, k, tn), lambda j, i: (layer, 0, j)),
                  pl.BlockSpec((None, k, tn), lambda j, i: (layer, 0, up0 + j))],
        out_specs=[pl.BlockSpec((tm, tn), lambda j, i: (i, j)), pl.BlockSpec((SROWS, tn), lambda j, i: (0, j))],
        out_shape=[jax.ShapeDtypeStruct((m, n), BF16), jax.ShapeDtypeStruct((SROWS, n), BF16)],
        scratch_shapes=[pltpu.VMEM((k, tn), BF16)] * 2,
        compiler_params=_cparams("arbitrary", "arbitrary"),
        name="swiglu",
    )(x, x_sample, w_gu, w_gu)


DENSE_LN_SUB = 128


def _layer_norm(z, g, b):
    mu = jnp.mean(z, axis=-1, keepdims=True)
    zc = z - mu
    var = jnp.mean(zc * zc, axis=-1, keepdims=True)
    return zc * lax.rsqrt(var + LN_EPS) * g + b


def _dense_ln_kernel(*refs, n_lhs, has_mod, alpha):
    it = iter(refs)
    x_refs = [next(it) for _ in range(n_lhs)]
    xs_ref, w_ref, res_ref, gate_ref, g_ref, b_ref = [next(it) for _ in range(6)]
    sc_ref, sh_ref = (next(it), next(it)) if has_mod else (None, None)
    res_s_ref, mod_s_ref = next(it), next(it)
    xn_ref = next(it)
    h_ref = next(it) if has_mod else None
    xn_s_ref = next(it)
    h_s_ref = next(it) if has_mod else None
    d = xn_ref.shape[-1]

    def finish(lhs, res, gate, sc, sh):
        xn = _layer_norm(alpha * res + gate * _dot(lhs, w_ref[...]), g_ref[...], b_ref[...])
        return xn, ((xn * (1.0 + sc) + sh).astype(BF16) if has_mod else None)

    @pl.when(pl.program_id(0) == 0)
    def _():
        mod_s = mod_s_ref[...]
        xn, h = finish(xs_ref[...], res_s_ref[...], mod_s[:, 0:d], mod_s[:, d:2 * d], mod_s[:, 2 * d:3 * d])
        xn_s_ref[...] = xn
        if has_mod:
            h_s_ref[...] = h

    tm = xn_ref.shape[0]
    sub = min(tm, DENSE_LN_SUB)
    for r0 in range(0, tm, sub):
        rows = slice(r0, r0 + sub)
        parts = [x[rows, :] for x in x_refs]
        lhs = parts[0] if n_lhs == 1 else jnp.concatenate(parts, axis=1)
        xn, h = finish(lhs, res_ref[rows, :], gate_ref[...], sc_ref[...] if has_mod else None,
                       sh_ref[...] if has_mod else None)
        xn_ref[rows, :] = xn
        if has_mod:
            h_ref[rows, :] = h


def _dense_ln(xs_list, x_sample, w, layer, res, res_s, gate, ln_g, ln_b, mod, mod_s, *, alpha, rows_per_batch, tm):
    m = xs_list[0].shape[0]
    tm = min(tm, rows_per_batch)
    k, d = w.shape[1], w.shape[2]
    assert sum(x.shape[1] for x in xs_list) == k and x_sample.shape == (SROWS, k)
    steps_per_batch = rows_per_batch // tm
    has_mod = mod is not None
    row = pl.BlockSpec((tm, d), lambda i: (i, 0))
    per_b = pl.BlockSpec((None, 1, d), lambda i: (i // steps_per_batch, 0, 0))
    vec = pl.BlockSpec((1, d), lambda i: (0, 0))
    srow = pl.BlockSpec((SROWS, d), lambda i: (0, 0))
    in_specs = [pl.BlockSpec((tm, x.shape[1]), lambda i: (i, 0)) for x in xs_list]
    in_specs += [pl.BlockSpec((SROWS, k), lambda i: (0, 0)),
                 pl.BlockSpec((None, k, d), lambda i: (layer, 0, 0), pipeline_mode=pl.Buffered(1)),
                 row, per_b, vec, vec]
    args = list(xs_list) + [x_sample, w, res, gate, ln_g.reshape(1, d), ln_b.reshape(1, d)]
    if has_mod:
        in_specs += [per_b, per_b]
        args += list(mod)
    in_specs += [srow, pl.BlockSpec((SROWS, 3 * d), lambda i: (0, 0))]
    args += [res_s, mod_s]
    out_specs, out_shape = [row], [jax.ShapeDtypeStruct((m, d), F32)]
    if has_mod:
        out_specs.append(row)
        out_shape.append(jax.ShapeDtypeStruct((m, d), BF16))
    out_specs.append(srow)
    out_shape.append(jax.ShapeDtypeStruct((SROWS, d), F32))
    if has_mod:
        out_specs.append(srow)
        out_shape.append(jax.ShapeDtypeStruct((SROWS, d), BF16))
    out = pl.pallas_call(
        functools.partial(_dense_ln_kernel, n_lhs=len(xs_list), has_mod=has_mod, alpha=alpha),
        grid=(m // tm,),
        in_specs=in_specs, out_specs=out_specs, out_shape=out_shape,
        compiler_params=_cparams("arbitrary"),
        name="dense_ln",
    )(*args)
    return out if has_mod else (out[0], None, out[1], None)


EXT_PER_HEAD = 6


def _ext_tables():
    pq = np.zeros((3 * LANES, LANES), np.float32)
    pk = np.zeros((3 * LANES, LANES), np.float32)
    oq = np.zeros((1, LANES), np.float32)
    ok = np.zeros((1, LANES), np.float32)
    for h in range(N_FOX_HEADS):
        for part in range(3):
            pq[part * LANES + h, EXT_PER_HEAD * h + part] = 1.0
            pk[part * LANES + h, EXT_PER_HEAD * h + 3 + part] = -1.0
            oq[0, EXT_PER_HEAD * h + 3 + part] = 1.0
            ok[0, EXT_PER_HEAD * h + part] = 1.0
    return jnp.asarray(pq, BF16), jnp.asarray(pk, BF16), jnp.asarray(oq), jnp.asarray(ok)


def _fox_cum_kernel(t_ref, bf_ref, pq_ref, pk_ref, oq_ref, ok_ref, lf_ref, eq_ref, ek_ref, carry, *, tb):
    @pl.when(pl.program_id(1) == 0)
    def _():
        carry[...] = jnp.zeros(carry.shape, F32)

    lf = _log_sigmoid(t_ref[...] + bf_ref[...])
    lf_ref[...] = lf[:, 0:N_FOX_HEADS]
    r = lax.broadcasted_iota(jnp.int32, (tb, tb), 0)
    c = lax.broadcasted_iota(jnp.int32, (tb, tb), 1)
    lower = (c <= r).astype(BF16)
    cum = _dot_exact_l(lower, lf) + carry[0:1, :]
    carry[...] = jnp.broadcast_to(cum[tb - 1:tb, :], carry.shape)
    hi, mid, lo = _split3(cum * LOG2E)
    cat = jnp.concatenate([hi, mid, lo], axis=1)
    eq_ref[...] = (_dot(cat, pq_ref[...]) + oq_ref[...]).astype(BF16)
    ek_ref[...] = (_dot(cat, pk_ref[...]) + ok_ref[...]).astype(BF16)


def _fox_cum(tail, b_f, tables):
    bsz, s, _ = tail.shape
    tb = min(512, s)
    pq, pk, oq, ok = tables
    bf = jnp.zeros((1, LANES), F32).at[0, :N_FOX_HEADS].set(b_f)
    blk = pl.BlockSpec((None, tb, LANES), lambda b, i: (b, i, 0))
    const = lambda shape: pl.BlockSpec(shape, lambda b, i: (0, 0))
    return pl.pallas_call(
        functools.partial(_fox_cum_kernel, tb=tb),
        grid=(bsz, s // tb),
        in_specs=[blk, const((1, LANES)), const(pq.shape), const(pk.shape), const((1, LANES)), const((1, LANES))],
        out_specs=[pl.BlockSpec((None, tb, N_FOX_HEADS), lambda b, i: (b, i, 0)), blk, blk],
        out_shape=[jax.ShapeDtypeStruct((bsz, s, N_FOX_HEADS), F32),
                   jax.ShapeDtypeStruct((bsz, s, LANES), BF16),
                   jax.ShapeDtypeStruct((bsz, s, LANES), BF16)],
        scratch_shapes=[pltpu.VMEM((8, LANES), F32)],
        compiler_params=_cparams("parallel", "arbitrary"),
        name="fox_cum",
    )(tail, bf, pq, pk, oq, ok)


HEADS_PER_STEP = 4


def _fox_kernel(qi_ref, kj_ref, q_ref, k_ref, vt_ref, eq_ref, ek_ref, o_ref, m_sc, l_sc, acc_sc, qx_sc, *, tq, tk):
    grp = pl.program_id(1)
    p = pl.program_id(2)
    qi = qi_ref[p]
    kj = kj_ref[p]

    @pl.when(kj == 0)
    def _():
        m_sc[...] = jnp.full(m_sc.shape, -jnp.inf, F32)
        l_sc[...] = jnp.zeros(l_sc.shape, F32)
        acc_sc[...] = jnp.zeros(acc_sc.shape, F32)
        lane = lax.broadcasted_iota(jnp.int32, (tq, LANES), 1)
        eq = eq_ref[...]
        for hh in range(HEADS_PER_STEP):
            lo = (grp * HEADS_PER_STEP + hh) * EXT_PER_HEAD
            own = (lane >= lo) & (lane < lo + EXT_PER_HEAD)
            qx_sc[hh] = jnp.concatenate([q_ref[:, hh * LANES:(hh + 1) * LANES],
                                         jnp.where(own, eq, jnp.zeros_like(eq))], axis=1)

    def step(diagonal):
        ek = ek_ref[...]
        for hh in range(HEADS_PER_STEP):
            cols = slice(hh * LANES, (hh + 1) * LANES)
            kx = jnp.concatenate([k_ref[:, cols], ek], axis=1)
            s = _dot_nt(kx, qx_sc[hh])
            if diagonal:
                key = lax.broadcasted_iota(jnp.int32, (tk, tq), 0)
                qry = lax.broadcasted_iota(jnp.int32, (tk, tq), 1)
                s = jnp.where(key <= qry, s, -jnp.inf)
            m_prev = m_sc[hh][0:1, :]
            m_new = jnp.maximum(m_prev, jnp.max(s, axis=0, keepdims=True))
            alpha = jnp.exp2(m_prev - m_new)
            pr = jnp.exp2(s - m_new)
            l_new = alpha * l_sc[hh][0:1, :] + jnp.sum(pr, axis=0, keepdims=True)
            acc = alpha * acc_sc[hh] + _dot(vt_ref[cols, :], pr.astype(BF16))
            m_sc[hh] = jnp.broadcast_to(m_new, (8, tq))
            l_sc[hh] = jnp.broadcast_to(l_new, (8, tq))
            acc_sc[hh] = acc
            if diagonal:
                o_ref[:, cols] = (acc / l_new).T.astype(o_ref.dtype)

    @pl.when(kj == qi)
    def _():
        step(True)

    @pl.when(kj != qi)
    def _():
        step(False)


def _fox_prompt(q, k, vt, ext_q, ext_k):
    bsz, s, _ = q.shape
    tq = tk = min(512, s)
    nq = s // tq
    width = HEADS_PER_STEP * LANES
    pairs = [(i, j) for i in range(nq) for j in range(i + 1)]
    qi_tbl = jnp.asarray([pr[0] for pr in pairs], jnp.int32)
    kj_tbl = jnp.asarray([pr[1] for pr in pairs], jnp.int32)
    grid_spec = pltpu.PrefetchScalarGridSpec(
        num_scalar_prefetch=2,
        grid=(bsz, N_FOX_HEADS // HEADS_PER_STEP, len(pairs)),
        in_specs=[pl.BlockSpec((None, tq, width), lambda b, g, p, qi, kj: (b, qi[p], g)),
                  pl.BlockSpec((None, tk, width), lambda b, g, p, qi, kj: (b, kj[p], g)),
                  pl.BlockSpec((width, tk), lambda b, g, p, qi, kj: (g, b * nq + kj[p])),
                  pl.BlockSpec((None, tq, LANES), lambda b, g, p, qi, kj: (b, qi[p], 0)),
                  pl.BlockSpec((None, tk, LANES), lambda b, g, p, qi, kj: (b, kj[p], 0))],
        out_specs=pl.BlockSpec((None, tq, width), lambda b, g, p, qi, kj: (b, qi[p], g)),
        scratch_shapes=[pltpu.VMEM((HEADS_PER_STEP, 8, tq), F32),
                        pltpu.VMEM((HEADS_PER_STEP, 8, tq), F32),
                        pltpu.VMEM((HEADS_PER_STEP, LANES, tq), F32),
                        pltpu.VMEM((HEADS_PER_STEP, tq, 2 * LANES), BF16)],
    )
    return pl.pallas_call(
        functools.partial(_fox_kernel, tq=tq, tk=tk),
        grid_spec=grid_spec,
        out_shape=jax.ShapeDtypeStruct((bsz, s, FOX_WIDTH), BF16),
        compiler_params=_cparams("parallel", "parallel", "arbitrary"),
        name="fox_prompt",
    )(qi_tbl, kj_tbl, q, k, vt, ext_q, ext_k)


def _gmlp_kernel(uv_ref, g_ref, b_ref, ws_ref, bs_ref, o_ref, *vn_ref):
    uv = uv_ref[...]
    r = lax.broadcasted_iota(jnp.int32, (CHUNK, CHUNK), 0)
    c = lax.broadcasted_iota(jnp.int32, (CHUNK, CHUNK), 1)
    for grp in range(GMLP_GROUPS):
        lo = grp * LANES
        u = jax.nn.gelu(uv[:, lo:lo + LANES])
        v = jax.nn.gelu(uv[:, GMLP_WIDTH + lo:GMLP_WIDTH + lo + LANES])
        mu = jnp.mean(v, axis=-1, keepdims=True)
        vc = v - mu
        var = jnp.mean(vc * vc, axis=-1, keepdims=True)
        vn = vc * lax.rsqrt(var + LN_EPS) * g_ref[:, lo:lo + LANES] + b_ref[:, lo:lo + LANES]
        w = jnp.where(c <= r, ws_ref[grp], 0.0).astype(BF16)
        mixed = _dot(w, vn.astype(BF16)) + bs_ref[grp]
        o_ref[:, lo:lo + LANES] = (u * mixed).astype(o_ref.dtype)
        if vn_ref:
            vn_ref[0][:, lo:lo + LANES] = vn


def _gmlp(ug, ln_g, ln_b, ws, bs, *, emit_vn):
    bsz, s, _ = ug.shape
    blk = pl.BlockSpec((None, CHUNK, GMLP_WIDTH), lambda b, i: (b, i, 0))
    vec = pl.BlockSpec((1, GMLP_WIDTH), lambda b, i: (0, 0))
    cube = pl.BlockSpec((GMLP_GROUPS, CHUNK, CHUNK), lambda b, i: (0, 0, 0))
    out_shape = [jax.ShapeDtypeStruct((bsz, s, GMLP_WIDTH), BF16)]
    if emit_vn:
        out_shape.append(jax.ShapeDtypeStruct((bsz, s, GMLP_WIDTH), F32))
    bs_b = jnp.broadcast_to(bs[:, :, None], (GMLP_GROUPS, CHUNK, LANES))
    out = pl.pallas_call(
        _gmlp_kernel,
        grid=(bsz, s // CHUNK),
        in_specs=[pl.BlockSpec((None, CHUNK, 2 * GMLP_WIDTH), lambda b, i: (b, i, 0)), vec, vec, cube, cube],
        out_specs=[blk] * len(out_shape),
        out_shape=out_shape,
        compiler_params=_cparams("parallel", "parallel"),
        name="gmlp",
    )(ug, ln_g.reshape(1, GMLP_WIDTH), ln_b.reshape(1, GMLP_WIDTH), ws, bs_b)
    return out if emit_vn else (out[0], None)


_GLA_LEVELS = (64, 32, 16, 8, 4, 2, 1)
_N_LEVEL = len(_GLA_LEVELS)


def _gla_tables():
    x = np.arange(CHUNK)[:, None]
    j = np.arange(CHUNK)[None, :]
    groups = []
    lv = np.full((CHUNK, CHUNK), -1, np.int32)
    for i, half in enumerate(_GLA_LEVELS):
        anchor = (x // (2 * half)) * (2 * half) + half - 1
        groups.append((j > np.minimum(x, anchor)) & (j <= np.maximum(x, anchor)))
        same = (x // (2 * half)) == (j // (2 * half))
        upper_t = (x % (2 * half)) >= half
        lower_s = (j % (2 * half)) < half
        lv[same & upper_t & lower_s] = i
    lv[np.arange(CHUNK), np.arange(CHUNK)] = _N_LEVEL
    groups.append(j <= x)
    groups.append(j > x)
    mall = np.concatenate(groups, axis=0).astype(np.float32)
    return jnp.asarray(mall, BF16), jnp.asarray(np.tile(lv, (GLA_HEADS, 1)))


def _gla_kernel(qk_ref, v_ref, g_ref, t_ref, w2_ref, b2_ref, ng_ref, s0_ref, mall_ref, lv_ref, o_ref, st_ref, state,
                *, valid_len, n_chunks):
    ci = pl.program_id(1)

    @pl.when(ci == 0)
    def _():
        state[...] = s0_ref[...].T

    qk = qk_ref[...]
    q = qk[:, 0:GLA_KW] * (GLA_DK ** -0.5)
    k = qk[:, GLA_KW:2 * GLA_KW]
    v = v_ref[...]
    gate = g_ref[...]
    la = _log_sigmoid(_dot(t_ref[...].astype(BF16), w2_ref[...].astype(BF16)) + b2_ref[...]) * (1.0 / GLA_TAU)
    if valid_len < CHUNK:
        live = lax.broadcasted_iota(jnp.int32, (CHUNK, GLA_KW), 0) < valid_len
        la = jnp.where(live, la, 0.0)
        k = jnp.where(live, k, 0.0)

    la_hi = la.astype(BF16)
    la_lo = (la - la_hi.astype(F32)).astype(BF16)
    mall = mall_ref[...]
    w = jnp.exp(_dot(mall, la_hi) + _dot(mall, la_lo))

    lane_head = lax.broadcasted_iota(jnp.int32, (CHUNK, GLA_KW), 1) // GLA_DK
    heads = [lane_head == h for h in range(GLA_HEADS)]
    q_stack = jnp.concatenate([jnp.where(hm, q, 0.0) for hm in heads], axis=0)
    lv = lv_ref[...]

    def level_w(i):
        return w[i * CHUNK:(i + 1) * CHUNK]

    a = jnp.zeros((GLA_HEADS * CHUNK, CHUNK), F32)
    for i in range(_N_LEVEL):
        wi = level_w(i)
        qi = q_stack * jnp.concatenate([wi] * GLA_HEADS, axis=0)
        a = jnp.where(lv == i, _dot_nt(qi.astype(BF16), (k * wi).astype(BF16)), a)
    a = jnp.where(lv == _N_LEVEL, _dot_nt(q_stack.astype(BF16), k.astype(BF16)), a)

    w_pre = level_w(_N_LEVEL)
    w_suf = level_w(_N_LEVEL + 1)
    st = state[...]
    q_pre = q_stack * jnp.concatenate([w_pre] * GLA_HEADS, axis=0)
    o_inter = _dot_nt(q_pre.astype(BF16), st.astype(BF16))
    v_b = v.astype(BF16)
    for h in range(GLA_HEADS):
        rows = slice(h * CHUNK, (h + 1) * CHUNK)
        cols = slice(h * LANES, (h + 1) * LANES)
        o = _dot(a[rows].astype(BF16), v_b[:, cols]) + o_inter[rows]
        o = o * lax.rsqrt(jnp.mean(o * o, axis=-1, keepdims=True) + LN_EPS)
        o_ref[:, cols] = (o * ng_ref[:, cols] * _silu(gate[:, cols])).astype(o_ref.dtype)

    kvt = _dot(v.T.astype(BF16), (k * w_suf).astype(BF16))
    upd = jnp.zeros((LANES, GLA_KW), F32)
    for h in range(GLA_HEADS):
        upd = upd + jnp.where(heads[h], kvt[h * LANES:(h + 1) * LANES], 0.0)
    st_new = st * w_pre[CHUNK - 1:CHUNK, :] + upd
    state[...] = st_new

    @pl.when(ci == n_chunks - 1)
    def _():
        st_ref[...] = st_new.T


def _gla(ug, tail, w2, b2, norm_g, s0, tables, *, valid_len=CHUNK):
    bsz, s, _ = ug.shape
    n_chunks = s // CHUNK
    mall, lv = tables
    w2pad = jnp.zeros((LANES, GLA_KW), F32).at[N_FOX_HEADS:N_FOX_HEADS + GLA_RANK].set(w2)
    const2 = lambda b, i: (0, 0)
    col = lambda c: pl.BlockSpec((None, CHUNK, TN), lambda b, i: (b, i, c))
    o, st = pl.pallas_call(
        functools.partial(_gla_kernel, valid_len=valid_len, n_chunks=n_chunks),
        grid=(bsz, n_chunks),
        in_specs=[col(2), col(3), col(4),
                  pl.BlockSpec((None, CHUNK, LANES), lambda b, i: (b, i, 0)),
                  pl.BlockSpec((LANES, GLA_KW), const2),
                  pl.BlockSpec((1, GLA_KW), const2),
                  pl.BlockSpec((1, GLA_WIDTH), const2),
                  pl.BlockSpec((None, GLA_KW, LANES), lambda b, i: (b, 0, 0)),
                  pl.BlockSpec(mall.shape, const2),
                  pl.BlockSpec(lv.shape, const2)],
        out_specs=[pl.BlockSpec((None, CHUNK, GLA_WIDTH), lambda b, i: (b, i, 0)),
                   pl.BlockSpec((None, GLA_KW, LANES), lambda b, i: (b, 0, 0))],
        out_shape=[jax.ShapeDtypeStruct((bsz, s, GLA_WIDTH), BF16),
                   jax.ShapeDtypeStruct((bsz, GLA_KW, LANES), F32)],
        scratch_shapes=[pltpu.VMEM((LANES, GLA_KW), F32)],
        compiler_params=_cparams("parallel", "arbitrary"),
        name="gla",
    )(ug, ug, ug, tail, w2pad, b2.reshape(1, GLA_KW), norm_g.reshape(1, GLA_WIDTH),
      s0.reshape(bsz, GLA_KW, LANES), mall, lv)
    return o, st.reshape(bsz, GLA_HEADS, GLA_DK, LANES)


PAGES_PER_STEP = 8
PAGE_ROWS = PAGE * N_FOX_HEADS


def _dec_tables():
    rc = np.arange(PAGE_ROWS)
    tok = 16 * (rc // LANES) + (rc % LANES) // N_FOX_HEADS
    t = np.arange(PAGE)
    after = (tok[:, None] > t[None, :]).astype(np.float32)
    spread = (t[:, None] == (rc // N_FOX_HEADS)[None, :]).astype(np.float32)
    return jnp.asarray(after, BF16), jnp.asarray(spread, BF16)


def _dec_kernel(pt_ref, *refs, n_steps):
    n = PAGES_PER_STEP
    f_refs, k_refs, v_refs = refs[:n], refs[n:2 * n], refs[2 * n:3 * n]
    q_ref, kn_ref, vn_ref, fn_ref, after_ref, spread_ref, o_ref, m_sc, l_sc, acc_sc, carry = refs[3 * n:]
    i = pl.program_id(1)
    scale = HEAD_DIM ** -0.5
    q = q_ref[...]

    @pl.when(i == 0)
    def _():
        s_self = jnp.sum(q.astype(BF16).astype(F32) * kn_ref[...].astype(BF16).astype(F32), axis=-1, keepdims=True) * scale
        m_sc[...] = jnp.broadcast_to(s_self, m_sc.shape)
        l_sc[...] = jnp.ones(l_sc.shape, F32)
        acc_sc[...] = vn_ref[...].astype(BF16).astype(F32)
        carry[...] = fn_ref[...]

    head_row = lax.broadcasted_iota(jnp.int32, (N_FOX_HEADS, LANES), 0)
    mine = lax.broadcasted_iota(jnp.int32, (N_FOX_HEADS, LANES), 1) % N_FOX_HEADS == head_row
    rows = []
    for f_ref in f_refs:
        g = f_ref[...]
        rows.append(jnp.concatenate(
            [jnp.where(mine, jnp.broadcast_to(g[r:r + 1, :], (N_FOX_HEADS, LANES)), 0.0)
             for r in range(N_FOX_HEADS)], axis=1))
    lhs = jnp.concatenate(rows, axis=0)
    suffix = _dot_exact(lhs, after_ref[...])
    total = _dot_exact(lhs, jnp.ones((PAGE_ROWS, LANES), BF16))
    run = carry[...]
    bias = [None] * n
    for j in reversed(range(n)):
        sl = slice(j * N_FOX_HEADS, (j + 1) * N_FOX_HEADS)
        bias[j] = suffix[sl] + run
        run = run + total[sl]
    carry[...] = run
    bias_x = _dot_exact(jnp.concatenate(bias, axis=0), spread_ref[...])

    q_b = q.astype(BF16)
    valid = lax.broadcasted_iota(jnp.int32, (N_FOX_HEADS, PAGE_ROWS), 1) % N_FOX_HEADS == \
        lax.broadcasted_iota(jnp.int32, (N_FOX_HEADS, PAGE_ROWS), 0)
    scores = []
    for j, k_ref in enumerate(k_refs):
        s = _dot_nt(q_b, k_ref[...].astype(BF16)) * scale + bias_x[j * N_FOX_HEADS:(j + 1) * N_FOX_HEADS]
        scores.append(jnp.where(valid, s, -jnp.inf))
    m_prev = m_sc[...]
    m_cur = scores[0]
    for s in scores[1:]:
        m_cur = jnp.maximum(m_cur, s)
    m_new = jnp.maximum(m_prev, jnp.max(m_cur, axis=1, keepdims=True))
    alpha = jnp.exp(m_prev - m_new)
    m_wide = jnp.concatenate([m_new] * (PAGE_ROWS // LANES), axis=1)
    acc = alpha * acc_sc[...]
    l_part = jnp.zeros((N_FOX_HEADS, PAGE_ROWS), F32)
    for s, v_ref in zip(scores, v_refs):
        p = jnp.exp(s - m_wide)
        l_part = l_part + p
        acc = acc + _dot(p.astype(BF16), v_ref[...].astype(BF16))
    l_new = alpha * l_sc[...] + jnp.sum(l_part, axis=1, keepdims=True)
    m_sc[...] = m_new
    l_sc[...] = l_new
    acc_sc[...] = acc

    @pl.when(i == n_steps - 1)
    def _():
        o_ref[...] = acc / l_new


def _fox_decode(page_table, k_pool, v_pool, f_pool, layer, q, k_new, v_new, logf_new_b, tables):
    bd, n_pages = page_table.shape
    n_steps = n_pages // PAGES_PER_STEP
    depth, n_pool = k_pool.shape[:2]
    k_view = k_pool.reshape(depth, n_pool, PAGE_ROWS, HEAD_DIM)
    v_view = v_pool.reshape(depth, n_pool, PAGE_ROWS, HEAD_DIM)
    f_view = f_pool.reshape(depth, n_pool, N_FOX_HEADS, LANES)
    after, spread = tables

    def page_of(b, i, pt, j):
        return pt[b, (n_steps - 1 - i) * PAGES_PER_STEP + j]

    def page_spec(j, rows):
        return pl.BlockSpec((None, None, rows, LANES), lambda b, i, pt: (layer, page_of(b, i, pt, j), 0, 0))

    vec = pl.BlockSpec((None, N_FOX_HEADS, HEAD_DIM), lambda b, i, pt: (b, 0, 0))
    const = lambda a: pl.BlockSpec(a.shape, lambda b, i, pt: (0, 0))
    pages = range(PAGES_PER_STEP)
    grid_spec = pltpu.PrefetchScalarGridSpec(
        num_scalar_prefetch=1,
        grid=(bd, n_steps),
        in_specs=[page_spec(j, N_FOX_HEADS) for j in pages] + [page_spec(j, PAGE_ROWS) for j in pages] * 2
        + [vec, vec, vec, vec, const(after), const(spread)],
        out_specs=vec,
        scratch_shapes=[pltpu.VMEM((N_FOX_HEADS, LANES), F32)] * 4,
    )
    return pl.pallas_call(
        functools.partial(_dec_kernel, n_steps=n_steps),
        grid_spec=grid_spec,
        out_shape=jax.ShapeDtypeStruct((bd, N_FOX_HEADS, HEAD_DIM), F32),
        compiler_params=_cparams("parallel", "arbitrary"),
        name="fox_decode",
    )(page_table, *([f_view] * PAGES_PER_STEP), *([k_view] * PAGES_PER_STEP), *([v_view] * PAGES_PER_STEP),
      q, k_new, v_new, logf_new_b, after, spread)


def _pad_rows(x, rows):
    return jnp.pad(x[:, None, :], ((0, 0), (0, rows - 1), (0, 0)))


def _rows16(x):
    return jnp.pad(x, ((0, SROWS - x.shape[0]), (0, 0)))


def kernel(x_prompt, x_sample, cache_k, cache_v, cache_logf, state_gla, page_table, c_prompt, c_sample,
           ln_in_g, ln_in_b, w_ada, b_ada, w_in, b_f, gm_ln_g, gm_ln_b, gm_ws, gm_bs, gla_w2, gla_b2,
           gla_norm_g, w_o, ln1_g, ln1_b, w_gu, w_down, ln2_g, ln2_b):
    bp, seq, d = x_prompt.shape
    bd = x_sample.shape[0]
    depth = w_ada.shape[0]
    assert d == 4 * GLA_WIDTH and x_sample.shape[1] == 1 and bp + bd <= 16 and bd <= SROWS
    assert w_in.shape[2] == IN_WIDTH
    alpha = (2 * depth) ** 0.25
    m_p = bp * seq

    gla_tables = _gla_tables()
    dec_tables = _dec_tables()
    ext_tables = _ext_tables()

    c16 = jnp.zeros((SROWS, d), F32).at[:bd].set(c_sample).at[bd:bd + bp].set(c_prompt)
    mods = _ada(c16, w_ada, b_ada)

    def mod_p(l, which):
        return mods[l, bd:bd + bp, which * d:(which + 1) * d][:, None, :]

    def mod_s(l, *which):
        return jnp.concatenate([mods[l, :, w * d:(w + 1) * d] for w in which], axis=-1)

    w_in_b = _pack_in(w_in)
    w_o_b = _cast(w_o)
    w_down_b = _cast(w_down)

    xp, hp = _ln_mod(x_prompt, ln_in_g, ln_in_b, mod=(mod_p(0, 1), mod_p(0, 0)), tr=256)
    xs, hs = _ln_mod(x_sample, ln_in_g, ln_in_b,
                     mod=(mods[0, :bd, d:2 * d][:, None, :], mods[0, :bd, 0:d][:, None, :]), tr=1)
    xp, hp = xp.reshape(m_p, d), hp.reshape(m_p, d)
    xs, hs = _rows16(xs[:, 0, :]), _rows16(hs[:, 0, :])

    zero_state = jnp.zeros((bp, GLA_HEADS, GLA_DK, LANES), F32)
    k_all = v_all = None
    fp, sp = [], []
    kd, vd, fd, sd, gd = [], [], [], [], []
    for l in range(depth):
        last = l == depth - 1
        q16, k_all, k16, v_all, vt16, ug, tail, samp = _proj(hp, hs, w_in_b, l, k_all, v_all, depth, tm=256)

        samp = samp[:bd]
        heads3 = lambda a: a.reshape(bd, N_FOX_HEADS, HEAD_DIM)
        q_s = heads3(samp[:, 0:FOX_WIDTH])
        k_s = heads3(samp[:, FOX_WIDTH:2 * FOX_WIDTH])
        v_s = heads3(samp[:, 2 * FOX_WIDTH:3 * FOX_WIDTH])
        ug_s = _pad_rows(samp[:, 6 * TN:11 * TN], CHUNK)
        tail_s = _pad_rows(samp[:, 11 * TN:11 * TN + LANES], CHUNK)
        logf_s, _, _ = _fox_cum(tail_s, b_f[l], ext_tables)
        logf_s = logf_s[:, 0, :]
        fox_s = _fox_decode(page_table, cache_k, cache_v, cache_logf, l, q_s, k_s, v_s,
                            jnp.broadcast_to(logf_s[:, :, None], (bd, N_FOX_HEADS, LANES)), dec_tables)
        gm_s, vn_s = _gmlp(ug_s, gm_ln_g[l], gm_ln_b[l], gm_ws[l], gm_bs[l], emit_vn=True)
        gla_s, st_s = _gla(ug_s, tail_s, gla_w2[l], gla_b2[l], gla_norm_g[l], state_gla[l], gla_tables, valid_len=1)
        mix_s = jnp.concatenate([fox_s.reshape(bd, -1).astype(BF16), gm_s[:, 0, :], gla_s[:, 0, :]], axis=-1)

        tail3 = tail.reshape(bp, seq, LANES)
        ug3 = ug.reshape(bp, seq, UG_WIDTH)
        logf, ext_q, ext_k = _fox_cum(tail3, b_f[l], ext_tables)
        fox_o = _fox_prompt(q16.reshape(bp, seq, FOX_WIDTH), k16.reshape(bp, seq, FOX_WIDTH), vt16, ext_q, ext_k)
        gm_o, _ = _gmlp(ug3, gm_ln_g[l], gm_ln_b[l], gm_ws[l], gm_bs[l], emit_vn=False)
        gla_o, st = _gla(ug3, tail3, gla_w2[l], gla_b2[l], gla_norm_g[l], zero_state, gla_tables)

        x1, h1, x1_s, h1_s = _dense_ln(
            [fox_o.reshape(m_p, -1), gm_o.reshape(m_p, -1), gla_o.reshape(m_p, -1)], _rows16(mix_s), w_o_b, l,
            xp, xs, mod_p(l, 2), ln1_g[l], ln1_b[l], (mod_p(l, 4), mod_p(l, 3)), mod_s(l, 2, 4, 3),
            alpha=alpha, rows_per_batch=seq, tm=256)
        act, act_s = _swiglu(h1, h1_s, w_gu, l, tm=1024, tn=512)
        nxt = None if last else (mod_p(l + 1, 1), mod_p(l + 1, 0))
        nxt_s = mod_s(l, 5, 5, 5) if last else jnp.concatenate([mod_s(l, 5), mod_s(l + 1, 1, 0)], axis=-1)
        xp, hp, xs, hs = _dense_ln([act], act_s, w_down_b, l, x1, x1_s, mod_p(l, 5), ln2_g[l], ln2_b[l], nxt, nxt_s,
                                   alpha=alpha, rows_per_batch=seq, tm=256)

        fp.append(logf)
        sp.append(st)
        kd.append(k_s[:, None])
        vd.append(v_s[:, None])
        fd.append(logf_s[:, None, :])
        sd.append(st_s)
        gd.append(vn_s[:, 0:1, :])

    kv_shape = (depth, bp, seq, N_FOX_HEADS, HEAD_DIM)
    return (xp.reshape(bp, seq, d), xs[:bd, None, :], k_all.reshape(kv_shape), v_all.reshape(kv_shape),
            jnp.stack(fp), jnp.stack(sp), jnp.stack(kd), jnp.stack(vd), jnp.stack(fd), jnp.stack(sd), jnp.stack(gd))
```

```python
import functools
import math

import numpy as np
import jax
import jax.numpy as jnp
from jax import lax
from jax.experimental import pallas as pl
from jax.experimental.pallas import tpu as pltpu

F32 = jnp.float32
BF16 = jnp.bfloat16

LANES = 128
HEAD_DIM = 128
N_FOX_HEADS = 8
FOX_WIDTH = N_FOX_HEADS * HEAD_DIM
GMLP_GROUPS = 4
GMLP_WIDTH = GMLP_GROUPS * LANES
GLA_HEADS = 4
GLA_DK = 64
GLA_KW = GLA_HEADS * GLA_DK
GLA_WIDTH = GLA_HEADS * LANES
GLA_RANK = 16
GLA_TAU = 16.0
CHUNK = 128
PAGE = 128
LN_EPS = 1e-5
LOG2E = math.log2(math.e)
VMEM_LIMIT = 56 * 1024 * 1024
SROWS = 16

TN = 512
OFF_FF = 3 * FOX_WIDTH
OFF_MU = OFF_FF + N_FOX_HEADS
OFF_GR = OFF_MU + 2 * GMLP_WIDTH + 2 * GLA_KW + GLA_WIDTH
OFF_GG = OFF_GR + GLA_RANK
IN_WIDTH = OFF_GG + GLA_WIDTH
N_PROJ_TILES = 12
UG_WIDTH = 5 * TN


def _cparams(*sem):
    return pltpu.CompilerParams(dimension_semantics=sem, vmem_limit_bytes=VMEM_LIMIT)


def _log_sigmoid(x):
    return jnp.minimum(x, 0.0) - jnp.log1p(jnp.exp(-jnp.abs(x)))


def _silu(x):
    return x * (1.0 / (1.0 + jnp.exp(-x)))


def _split3(x):
    hi = x.astype(BF16)
    r = x - hi.astype(F32)
    mid = r.astype(BF16)
    lo = (r - mid.astype(F32)).astype(BF16)
    return hi, mid, lo


def _dot(a, b):
    return jnp.dot(a, b, preferred_element_type=F32)


def _dot_nt(a, b):
    return lax.dot_general(a, b, (((1,), (1,)), ((), ())), preferred_element_type=F32)


def _dot_exact(x, w):
    hi, mid, lo = _split3(x)
    return _dot(hi, w) + _dot(mid, w) + _dot(lo, w)


def _dot_exact_l(w, x):
    hi, mid, lo = _split3(x)
    return _dot(w, hi) + _dot(w, mid) + _dot(w, lo)


def _ada_kernel(c_ref, w_ref, b_ref, o_ref):
    a = _silu(c_ref[...]).astype(BF16)
    o_ref[...] = _dot(a, w_ref[...].astype(BF16)) + b_ref[...]


def _ada(c16, w_ada, b_ada):
    depth, d, n = w_ada.shape
    tn = 1024
    return pl.pallas_call(
        _ada_kernel,
        grid=(depth, n // tn),
        in_specs=[pl.BlockSpec((16, d), lambda l, j: (0, 0)),
                  pl.BlockSpec((None, d, tn), lambda l, j: (l, 0, j)),
                  pl.BlockSpec((None, 1, tn), lambda l, j: (l, 0, j))],
        out_specs=pl.BlockSpec((None, 16, tn), lambda l, j: (l, 0, j)),
        out_shape=jax.ShapeDtypeStruct((depth, 16, n), F32),
        compiler_params=_cparams("parallel", "parallel"),
        name="ada",
    )(c16, w_ada, b_ada.reshape(depth, 1, n))


def _ln_kernel(*refs, has_res, has_mod, alpha):
    it = iter(refs)
    x = next(it)[...]
    if has_res:
        y = next(it)[...]
        gate = next(it)[...]
        x = alpha * x + gate * y
    g = next(it)[...]
    b = next(it)[...]
    if has_mod:
        sc = next(it)[...]
        sh = next(it)[...]
    xn_ref = next(it)
    mu = jnp.mean(x, axis=-1, keepdims=True)
    xc = x - mu
    var = jnp.mean(xc * xc, axis=-1, keepdims=True)
    xn = xc * lax.rsqrt(var + LN_EPS) * g + b
    xn_ref[...] = xn
    if has_mod:
        h_ref = next(it)
        h_ref[...] = (xn * (1.0 + sc) + sh).astype(BF16)


def _ln_mod(x, g, b, *, res=None, mod=None, alpha=1.0, tr):
    bsz, s, d = x.shape
    tr = min(tr, s)
    row = pl.BlockSpec((None, tr, d), lambda i, j: (i, j, 0))
    per_b = pl.BlockSpec((None, 1, d), lambda i, j: (i, 0, 0))
    vec = pl.BlockSpec((1, d), lambda i, j: (0, 0))
    args, specs = [x], [row]
    if res is not None:
        y, gate = res
        args += [y, gate]
        specs += [row, per_b]
    args += [g.reshape(1, d), b.reshape(1, d)]
    specs += [vec, vec]
    out_shape = [jax.ShapeDtypeStruct((bsz, s, d), F32)]
    out_specs = [row]
    if mod is not None:
        sc, sh = mod
        args += [sc, sh]
        specs += [per_b, per_b]
        out_shape.append(jax.ShapeDtypeStruct((bsz, s, d), BF16))
        out_specs.append(row)
    out = pl.pallas_call(
        functools.partial(_ln_kernel, has_res=res is not None, has_mod=mod is not None, alpha=alpha),
        grid=(bsz, s // tr),
        in_specs=specs, out_specs=out_specs, out_shape=out_shape,
        compiler_params=_cparams("parallel", "parallel"),
        name="ln_mod",
    )(*args)
    return out if mod is not None else (out[0], None)


PACK_EXTRA = 32


def _pack_in_kernel(wm_ref, wn_ref, wg_ref, o_ref):
    j = pl.program_id(1)
    k_dim = wm_ref.shape[1]

    @pl.when(j < 6)
    def _():
        o_ref[...] = wm_ref[...].T.astype(BF16)

    def shifted(rows):
        return jnp.concatenate([wm_ref[rows:TN, :], wn_ref[0:rows, :]], axis=0).T.astype(BF16)

    @pl.when((j >= 6) & (j < 10))
    def _():
        o_ref[...] = shifted(OFF_MU - OFF_FF)

    @pl.when(j == 10)
    def _():
        o_ref[...] = shifted(OFF_GG - 10 * TN)

    @pl.when(j == N_PROJ_TILES - 1)
    def _():
        lo = OFF_GR - (OFF_GR // PACK_EXTRA) * PACK_EXTRA
        t = jnp.concatenate([wn_ref[0:N_FOX_HEADS, :], wg_ref[lo:lo + GLA_RANK, :],
                             jnp.zeros((TN - N_FOX_HEADS - GLA_RANK, k_dim), F32)], axis=0)
        o_ref[...] = t.T.astype(BF16)


def _pack_in(w_in):
    depth, k, _ = w_in.shape
    w_t = jnp.transpose(w_in, (0, 2, 1))
    per = TN // PACK_EXTRA
    nb = lambda j: jnp.where((j >= 6) & (j < N_PROJ_TILES - 1), per * (j + 1), OFF_FF // PACK_EXTRA)
    return pl.pallas_call(
        _pack_in_kernel,
        grid=(depth, N_PROJ_TILES),
        in_specs=[pl.BlockSpec((None, TN, k), lambda l, j: (l, jnp.minimum(j, 10), 0)),
                  pl.BlockSpec((None, PACK_EXTRA, k), lambda l, j: (l, nb(j), 0)),
                  pl.BlockSpec((None, PACK_EXTRA, k), lambda l, j: (l, OFF_GR // PACK_EXTRA, 0))],
        out_specs=pl.BlockSpec((None, k, TN), lambda l, j: (l, 0, j)),
        out_shape=jax.ShapeDtypeStruct((depth, k, N_PROJ_TILES * TN), BF16),
        compiler_params=_cparams("parallel", "parallel"),
        name="pack_in",
    )(w_t, w_t, w_t)


def _cast_kernel(w_ref, o_ref):
    o_ref[...] = w_ref[...].astype(BF16)


def _cast(w):
    depth, k, n = w.shape
    blk = pl.BlockSpec((None, k, TN), lambda l, j: (l, 0, j))
    return pl.pallas_call(
        _cast_kernel, grid=(depth, n // TN), in_specs=[blk], out_specs=blk,
        out_shape=jax.ShapeDtypeStruct(w.shape, BF16),
        compiler_params=_cparams("parallel", "parallel"), name="cast",
    )(w)


def _proj_kernel(*refs, aliased, q_scale, tm):
    x_ref, xs_ref, w_ref = refs[:3]
    q_ref, k32_ref, k16_ref, v32_ref, vt_ref, ug_ref, t_ref, s_ref = refs[3 + (2 if aliased else 0):]
    x = x_ref[...]
    q_ref[...] = (_dot(x, w_ref[:, 0:FOX_WIDTH]) * q_scale).astype(BF16)
    k = _dot(x, w_ref[:, FOX_WIDTH:2 * FOX_WIDTH])
    k16_ref[...] = k.astype(BF16)
    v = _dot(x, w_ref[:, 2 * FOX_WIDTH:3 * FOX_WIDTH])
    vt_ref[...] = v.T.astype(BF16)
    for h in range(N_FOX_HEADS):
        cols = slice(h * HEAD_DIM, (h + 1) * HEAD_DIM)
        k32_ref[pl.ds(h, tm, stride=N_FOX_HEADS), :] = k[:, cols]
        v32_ref[pl.ds(h, tm, stride=N_FOX_HEADS), :] = v[:, cols]
    ug_ref[...] = _dot(x, w_ref[:, 6 * TN:11 * TN])
    t_ref[...] = _dot(x, w_ref[:, 11 * TN:11 * TN + LANES])

    @pl.when(pl.program_id(0) == 0)
    def _():
        s_ref[...] = _dot(xs_ref[...], w_ref[...])


def _proj(x, xs, w_in, layer, k_prev, v_prev, depth, *, tm):
    m, k = x.shape
    tm = min(tm, m)
    n_all = w_in.shape[2]
    in_specs = [pl.BlockSpec((tm, k), lambda i: (i, 0)),
                pl.BlockSpec((SROWS, k), lambda i: (0, 0)),
                pl.BlockSpec((None, k, n_all), lambda i: (layer, 0, 0), pipeline_mode=pl.Buffered(1))]
    args = [x, xs, w_in]
    aliases = {}
    if k_prev is not None:
        in_specs += [pl.BlockSpec(memory_space=pl.ANY)] * 2
        args += [k_prev, v_prev]
        aliases = {3: 1, 4: 3}
    rows = pl.BlockSpec((tm, FOX_WIDTH), lambda i: (i, 0))
    pairs = pl.BlockSpec((None, tm * N_FOX_HEADS, HEAD_DIM), lambda i: (layer, i, 0))
    out_shape = [jax.ShapeDtypeStruct((m, FOX_WIDTH), BF16),
                 jax.ShapeDtypeStruct((depth, m * N_FOX_HEADS, HEAD_DIM), F32),
                 jax.ShapeDtypeStruct((m, FOX_WIDTH), BF16),
                 jax.ShapeDtypeStruct((depth, m * N_FOX_HEADS, HEAD_DIM), F32),
                 jax.ShapeDtypeStruct((FOX_WIDTH, m), BF16),
                 jax.ShapeDtypeStruct((m, UG_WIDTH), F32),
                 jax.ShapeDtypeStruct((m, LANES), F32),
                 jax.ShapeDtypeStruct((SROWS, n_all), F32)]
    out_specs = [rows, pairs, rows, pairs,
                 pl.BlockSpec((FOX_WIDTH, tm), lambda i: (0, i)),
                 pl.BlockSpec((tm, UG_WIDTH), lambda i: (i, 0)),
                 pl.BlockSpec((tm, LANES), lambda i: (i, 0)),
                 pl.BlockSpec((SROWS, n_all), lambda i: (0, 0))]
    return pl.pallas_call(
        functools.partial(_proj_kernel, aliased=k_prev is not None, q_scale=HEAD_DIM ** -0.5 * LOG2E, tm=tm),
        grid=(m // tm,),
        in_specs=in_specs, out_specs=out_specs, out_shape=out_shape,
        input_output_aliases=aliases,
        compiler_params=_cparams("arbitrary"),
        name="proj",
    )(*args)


def _swiglu_kernel(x_ref, xs_ref, wg_ref, wu_ref, o_ref, os_ref, wbg, wbu):
    def result(x):
        return (_silu(_dot(x, wbg[...])) * _dot(x, wbu[...])).astype(BF16)

    @pl.when(pl.program_id(1) == 0)
    def _():
        wbg[...] = wg_ref[...].astype(BF16)
        wbu[...] = wu_ref[...].astype(BF16)
        os_ref[...] = result(xs_ref[...])

    o_ref[...] = result(x_ref[...])


def _swiglu(x, x_sample, w_gu, layer, *, tm, tn):
    m, k = x.shape
    tm = min(tm, m)
    n = w_gu.shape[2] // 2
    up0 = n // tn
    return pl.pallas_call(
        _swiglu_kernel,
        grid=(n // tn, m // tm),
        in_specs=[pl.BlockSpec((tm, k), lambda j, i: (i, 0)),
                  pl.BlockSpec((SROWS, k), lambda j, i: (0, 0)),
                  pl.BlockSpec((None, k, tn), lambda j, i: (layer, 0, j)),
                  pl.BlockSpec((None, k, tn), lambda j, i: (layer, 0, up0 + j))],
        out_specs=[pl.BlockSpec((tm, tn), lambda j, i: (i, j)), pl.BlockSpec((SROWS, tn), lambda j, i: (0, j))],
        out_shape=[jax.ShapeDtypeStruct((m, n), BF16), jax.ShapeDtypeStruct((SROWS, n), BF16)],
        scratch_shapes=[pltpu.VMEM((k, tn), BF16)] * 2,
        compiler_params=_cparams("arbitrary", "arbitrary"),
        name="swiglu",
    )(x, x_sample, w_gu, w_gu)


DENSE_LN_SUB = 128


def _layer_norm(z, g, b):
    mu = jnp.mean(z, axis=-1, keepdims=True)
    zc = z - mu
    var = jnp.mean(zc * zc, axis=-1, keepdims=True)
    return zc * lax.rsqrt(var + LN_EPS) * g + b


def _dense_ln_kernel(*refs, n_lhs, has_mod, alpha):
    it = iter(refs)
    x_refs = [next(it) for _ in range(n_lhs)]
    xs_ref, w_ref, res_ref, gate_ref, g_ref, b_ref = [next(it) for _ in range(6)]
    sc_ref, sh_ref = (next(it), next(it)) if has_mod else (None, None)
    res_s_ref, mod_s_ref = next(it), next(it)
    xn_ref = next(it)
    h_ref = next(it) if has_mod else None
    xn_s_ref = next(it)
    h_s_ref = next(it) if has_mod else None
    d = xn_ref.shape[-1]

    def finish(lhs, res, gate, sc, sh):
        xn = _layer_norm(alpha * res + gate * _dot(lhs, w_ref[...]), g_ref[...], b_ref[...])
        return xn, ((xn * (1.0 + sc) + sh).astype(BF16) if has_mod else None)

    @pl.when(pl.program_id(0) == 0)
    def _():
        mod_s = mod_s_ref[...]
        xn, h = finish(xs_ref[...], res_s_ref[...], mod_s[:, 0:d], mod_s[:, d:2 * d], mod_s[:, 2 * d:3 * d])
        xn_s_ref[...] = xn
        if has_mod:
            h_s_ref[...] = h

    tm = xn_ref.shape[0]
    sub = min(tm, DENSE_LN_SUB)
    for r0 in range(0, tm, sub):
        rows = slice(r0, r0 + sub)
        parts = [x[rows, :] for x in x_refs]
        lhs = parts[0] if n_lhs == 1 else jnp.concatenate(parts, axis=1)
        xn, h = finish(lhs, res_ref[rows, :], gate_ref[...], sc_ref[...] if has_mod else None,
                       sh_ref[...] if has_mod else None)
        xn_ref[rows, :] = xn
        if has_mod:
            h_ref[rows, :] = h


def _dense_ln(xs_list, x_sample, w, layer, res, res_s, gate, ln_g, ln_b, mod, mod_s, *, alpha, rows_per_batch, tm):
    m = xs_list[0].shape[0]
    tm = min(tm, rows_per_batch)
    k, d = w.shape[1], w.shape[2]
    assert sum(x.shape[1] for x in xs_list) == k and x_sample.shape == (SROWS, k)
    steps_per_batch = rows_per_batch // tm
    has_mod = mod is not None
    row = pl.BlockSpec((tm, d), lambda i: (i, 0))
    per_b = pl.BlockSpec((None, 1, d), lambda i: (i // steps_per_batch, 0, 0))
    vec = pl.BlockSpec((1, d), lambda i: (0, 0))
    srow = pl.BlockSpec((SROWS, d), lambda i: (0, 0))
    in_specs = [pl.BlockSpec((tm, x.shape[1]), lambda i: (i, 0)) for x in xs_list]
    in_specs += [pl.BlockSpec((SROWS, k), lambda i: (0, 0)),
                 pl.BlockSpec((None, k, d), lambda i: (layer, 0, 0), pipeline_mode=pl.Buffered(1)),
                 row, per_b, vec, vec]
    args = list(xs_list) + [x_sample, w, res, gate, ln_g.reshape(1, d), ln_b.reshape(1, d)]
    if has_mod:
        in_specs += [per_b, per_b]
        args += list(mod)
    in_specs += [srow, pl.BlockSpec((SROWS, 3 * d), lambda i: (0, 0))]
    args += [res_s, mod_s]
    out_specs, out_shape = [row], [jax.ShapeDtypeStruct((m, d), F32)]
    if has_mod:
        out_specs.append(row)
        out_shape.append(jax.ShapeDtypeStruct((m, d), BF16))
    out_specs.append(srow)
    out_shape.append(jax.ShapeDtypeStruct((SROWS, d), F32))
    if has_mod:
        out_specs.append(srow)
        out_shape.append(jax.ShapeDtypeStruct((SROWS, d), BF16))
    out = pl.pallas_call(
        functools.partial(_dense_ln_kernel, n_lhs=len(xs_list), has_mod=has_mod, alpha=alpha),
        grid=(m // tm,),
        in_specs=in_specs, out_specs=out_specs, out_shape=out_shape,
        compiler_params=_cparams("arbitrary"),
        name="dense_ln",
    )(*args)
    return out if has_mod else (out[0], None, out[1], None)


EXT_PER_HEAD = 6


def _ext_tables():
    pq = np.zeros((3 * LANES, LANES), np.float32)
    pk = np.zeros((3 * LANES, LANES), np.float32)
    oq = np.zeros((1, LANES), np.float32)
    ok = np.zeros((1, LANES), np.float32)
    for h in range(N_FOX_HEADS):
        for part in range(3):
            pq[part * LANES + h, EXT_PER_HEAD * h + part] = 1.0
            pk[part * LANES + h, EXT_PER_HEAD * h + 3 + part] = -1.0
            oq[0, EXT_PER_HEAD * h + 3 + part] = 1.0
            ok[0, EXT_PER_HEAD * h + part] = 1.0
    return jnp.asarray(pq, BF16), jnp.asarray(pk, BF16), jnp.asarray(oq), jnp.asarray(ok)


def _fox_cum_kernel(t_ref, bf_ref, pq_ref, pk_ref, oq_ref, ok_ref, lf_ref, eq_ref, ek_ref, carry, *, tb):
    @pl.when(pl.program_id(1) == 0)
    def _():
        carry[...] = jnp.zeros(carry.shape, F32)

    lf = _log_sigmoid(t_ref[...] + bf_ref[...])
    lf_ref[...] = lf[:, 0:N_FOX_HEADS]
    r = lax.broadcasted_iota(jnp.int32, (tb, tb), 0)
    c = lax.broadcasted_iota(jnp.int32, (tb, tb), 1)
    lower = (c <= r).astype(BF16)
    cum = _dot_exact_l(lower, lf) + carry[0:1, :]
    carry[...] = jnp.broadcast_to(cum[tb - 1:tb, :], carry.shape)
    hi, mid, lo = _split3(cum * LOG2E)
    cat = jnp.concatenate([hi, mid, lo], axis=1)
    eq_ref[...] = (_dot(cat, pq_ref[...]) + oq_ref[...]).astype(BF16)
    ek_ref[...] = (_dot(cat, pk_ref[...]) + ok_ref[...]).astype(BF16)


def _fox_cum(tail, b_f, tables):
    bsz, s, _ = tail.shape
    tb = min(512, s)
    pq, pk, oq, ok = tables
    bf = jnp.zeros((1, LANES), F32).at[0, :N_FOX_HEADS].set(b_f)
    blk = pl.BlockSpec((None, tb, LANES), lambda b, i: (b, i, 0))
    const = lambda shape: pl.BlockSpec(shape, lambda b, i: (0, 0))
    return pl.pallas_call(
        functools.partial(_fox_cum_kernel, tb=tb),
        grid=(bsz, s // tb),
        in_specs=[blk, const((1, LANES)), const(pq.shape), const(pk.shape), const((1, LANES)), const((1, LANES))],
        out_specs=[pl.BlockSpec((None, tb, N_FOX_HEADS), lambda b, i: (b, i, 0)), blk, blk],
        out_shape=[jax.ShapeDtypeStruct((bsz, s, N_FOX_HEADS), F32),
                   jax.ShapeDtypeStruct((bsz, s, LANES), BF16),
                   jax.ShapeDtypeStruct((bsz, s, LANES), BF16)],
        scratch_shapes=[pltpu.VMEM((8, LANES), F32)],
        compiler_params=_cparams("parallel", "arbitrary"),
        name="fox_cum",
    )(tail, bf, pq, pk, oq, ok)


HEADS_PER_STEP = 4
ATTN_TILE = 1024


def _fox_kernel(qi_ref, kj_ref, q_ref, k_ref, vt_ref, eq_ref, ek_ref, o_ref, m_sc, l_sc, acc_sc, qx_sc, *, tq, tk):
    grp = pl.program_id(1)
    p = pl.program_id(2)
    qi = qi_ref[p]
    kj = kj_ref[p]

    @pl.when(kj == 0)
    def _():
        m_sc[...] = jnp.full(m_sc.shape, -jnp.inf, F32)
        l_sc[...] = jnp.zeros(l_sc.shape, F32)
        acc_sc[...] = jnp.zeros(acc_sc.shape, F32)
        lane = lax.broadcasted_iota(jnp.int32, (tq, LANES), 1)
        eq = eq_ref[...]
        for hh in range(HEADS_PER_STEP):
            lo = (grp * HEADS_PER_STEP + hh) * EXT_PER_HEAD
            own = (lane >= lo) & (lane < lo + EXT_PER_HEAD)
            qx_sc[hh] = jnp.concatenate([q_ref[:, hh * LANES:(hh + 1) * LANES],
                                         jnp.where(own, eq, jnp.zeros_like(eq))], axis=1)

    def step(diagonal):
        ek = ek_ref[...]
        for hh in range(HEADS_PER_STEP):
            cols = slice(hh * LANES, (hh + 1) * LANES)
            kx = jnp.concatenate([k_ref[:, cols], ek], axis=1)
            s = _dot_nt(kx, qx_sc[hh])
            if diagonal:
                key = lax.broadcasted_iota(jnp.int32, (tk, tq), 0)
                qry = lax.broadcasted_iota(jnp.int32, (tk, tq), 1)
                s = jnp.where(key <= qry, s, -jnp.inf)
            m_prev = m_sc[hh][0:1, :]
            m_new = jnp.maximum(m_prev, jnp.max(s, axis=0, keepdims=True))
            alpha = jnp.exp2(m_prev - m_new)
            pr = jnp.exp2(s - m_new)
            l_new = alpha * l_sc[hh][0:1, :] + jnp.sum(pr, axis=0, keepdims=True)
            acc = alpha * acc_sc[hh] + _dot(vt_ref[cols, :], pr.astype(BF16))
            m_sc[hh] = jnp.broadcast_to(m_new, (8, tq))
            l_sc[hh] = jnp.broadcast_to(l_new, (8, tq))
            acc_sc[hh] = acc
            if diagonal:
                o_ref[:, cols] = (acc / l_new).T.astype(o_ref.dtype)

    @pl.when(kj == qi)
    def _():
        step(True)

    @pl.when(kj != qi)
    def _():
        step(False)


def _fox_prompt(q, k, vt, ext_q, ext_k):
    bsz, s, _ = q.shape
    tq = tk = min(ATTN_TILE, s)
    nq = s // tq
    width = HEADS_PER_STEP * LANES
    pairs = [(i, j) for i in range(nq) for j in range(i + 1)]
    qi_tbl = jnp.asarray([pr[0] for pr in pairs], jnp.int32)
    kj_tbl = jnp.asarray([pr[1] for pr in pairs], jnp.int32)
    grid_spec = pltpu.PrefetchScalarGridSpec(
        num_scalar_prefetch=2,
        grid=(bsz, N_FOX_HEADS // HEADS_PER_STEP, len(pairs)),
        in_specs=[pl.BlockSpec((None, tq, width), lambda b, g, p, qi, kj: (b, qi[p], g)),
                  pl.BlockSpec((None, tk, width), lambda b, g, p, qi, kj: (b, kj[p], g)),
                  pl.BlockSpec((width, tk), lambda b, g, p, qi, kj: (g, b * nq + kj[p])),
                  pl.BlockSpec((None, tq, LANES), lambda b, g, p, qi, kj: (b, qi[p], 0)),
                  pl.BlockSpec((None, tk, LANES), lambda b, g, p, qi, kj: (b, kj[p], 0))],
        out_specs=pl.BlockSpec((None, tq, width), lambda b, g, p, qi, kj: (b, qi[p], g)),
        scratch_shapes=[pltpu.VMEM((HEADS_PER_STEP, 8, tq), F32),
                        pltpu.VMEM((HEADS_PER_STEP, 8, tq), F32),
                        pltpu.VMEM((HEADS_PER_STEP, LANES, tq), F32),
                        pltpu.VMEM((HEADS_PER_STEP, tq, 2 * LANES), BF16)],
    )
    return pl.pallas_call(
        functools.partial(_fox_kernel, tq=tq, tk=tk),
        grid_spec=grid_spec,
        out_shape=jax.ShapeDtypeStruct((bsz, s, FOX_WIDTH), BF16),
        compiler_params=_cparams("parallel", "parallel", "arbitrary"),
        name="fox_prompt",
    )(qi_tbl, kj_tbl, q, k, vt, ext_q, ext_k)


GMLP_CHUNKS_PER_STEP = 4


def _gmlp_kernel(uv_ref, g_ref, b_ref, ws_ref, bs_ref, o_ref, *vn_ref):
    r = lax.broadcasted_iota(jnp.int32, (CHUNK, CHUNK), 0)
    c = lax.broadcasted_iota(jnp.int32, (CHUNK, CHUNK), 1)
    for grp in range(GMLP_GROUPS):
        lo = grp * LANES
        w = jnp.where(c <= r, ws_ref[grp], 0.0).astype(BF16)
        for r0 in range(0, uv_ref.shape[0], CHUNK):
            rows = slice(r0, r0 + CHUNK)
            u = jax.nn.gelu(uv_ref[rows, lo:lo + LANES])
            v = jax.nn.gelu(uv_ref[rows, GMLP_WIDTH + lo:GMLP_WIDTH + lo + LANES])
            mu = jnp.mean(v, axis=-1, keepdims=True)
            vc = v - mu
            var = jnp.mean(vc * vc, axis=-1, keepdims=True)
            vn = vc * lax.rsqrt(var + LN_EPS) * g_ref[:, lo:lo + LANES] + b_ref[:, lo:lo + LANES]
            mixed = _dot(w, vn.astype(BF16)) + bs_ref[grp]
            o_ref[rows, lo:lo + LANES] = (u * mixed).astype(o_ref.dtype)
            if vn_ref:
                vn_ref[0][rows, lo:lo + LANES] = vn


def _gmlp(ug, ln_g, ln_b, ws, bs, *, emit_vn):
    bsz, s, _ = ug.shape
    rows = CHUNK * min(GMLP_CHUNKS_PER_STEP, s // CHUNK)
    blk = pl.BlockSpec((None, rows, GMLP_WIDTH), lambda b, i: (b, i, 0))
    vec = pl.BlockSpec((1, GMLP_WIDTH), lambda b, i: (0, 0))
    cube = pl.BlockSpec((GMLP_GROUPS, CHUNK, CHUNK), lambda b, i: (0, 0, 0))
    out_shape = [jax.ShapeDtypeStruct((bsz, s, GMLP_WIDTH), BF16)]
    if emit_vn:
        out_shape.append(jax.ShapeDtypeStruct((bsz, s, GMLP_WIDTH), F32))
    bs_b = jnp.broadcast_to(bs[:, :, None], (GMLP_GROUPS, CHUNK, LANES))
    out = pl.pallas_call(
        _gmlp_kernel,
        grid=(bsz, s // rows),
        in_specs=[pl.BlockSpec((None, rows, 2 * GMLP_WIDTH), lambda b, i: (b, i, 0)), vec, vec, cube, cube],
        out_specs=[blk] * len(out_shape),
        out_shape=out_shape,
        compiler_params=_cparams("parallel", "parallel"),
        name="gmlp",
    )(ug, ln_g.reshape(1, GMLP_WIDTH), ln_b.reshape(1, GMLP_WIDTH), ws, bs_b)
    return out if emit_vn else (out[0], None)


_GLA_LEVELS = (64, 32, 16, 8, 4, 2, 1)
_N_LEVEL = len(_GLA_LEVELS)


def _gla_tables():
    x = np.arange(CHUNK)[:, None]
    j = np.arange(CHUNK)[None, :]
    groups = []
    lv = np.full((CHUNK, CHUNK), -1, np.int32)
    for i, half in enumerate(_GLA_LEVELS):
        anchor = (x // (2 * half)) * (2 * half) + half - 1
        groups.append((j > np.minimum(x, anchor)) & (j <= np.maximum(x, anchor)))
        same = (x // (2 * half)) == (j // (2 * half))
        upper_t = (x % (2 * half)) >= half
        lower_s = (j % (2 * half)) < half
        lv[same & upper_t & lower_s] = i
    lv[np.arange(CHUNK), np.arange(CHUNK)] = _N_LEVEL
    groups.append(j <= x)
    groups.append(j > x)
    mall = np.concatenate(groups, axis=0).astype(np.float32)
    return jnp.asarray(mall, BF16), jnp.asarray(np.tile(lv, (GLA_HEADS, 1)))


def _gla_kernel(qk_ref, v_ref, g_ref, t_ref, w2_ref, b2_ref, ng_ref, s0_ref, mall_ref, lv_ref, o_ref, st_ref, state,
                *, valid_len, n_chunks):
    ci = pl.program_id(1)

    @pl.when(ci == 0)
    def _():
        state[...] = s0_ref[...].T

    qk = qk_ref[...]
    q = qk[:, 0:GLA_KW] * (GLA_DK ** -0.5)
    k = qk[:, GLA_KW:2 * GLA_KW]
    v = v_ref[...]
    gate = g_ref[...]
    la = _log_sigmoid(_dot(t_ref[...].astype(BF16), w2_ref[...].astype(BF16)) + b2_ref[...]) * (1.0 / GLA_TAU)
    if valid_len < CHUNK:
        live = lax.broadcasted_iota(jnp.int32, (CHUNK, GLA_KW), 0) < valid_len
        la = jnp.where(live, la, 0.0)
        k = jnp.where(live, k, 0.0)

    la_hi = la.astype(BF16)
    la_lo = (la - la_hi.astype(F32)).astype(BF16)
    mall = mall_ref[...]
    w = jnp.exp(_dot(mall, la_hi) + _dot(mall, la_lo))

    lane_head = lax.broadcasted_iota(jnp.int32, (CHUNK, GLA_KW), 1) // GLA_DK
    heads = [lane_head == h for h in range(GLA_HEADS)]
    q_stack = jnp.concatenate([jnp.where(hm, q, 0.0) for hm in heads], axis=0)
    lv = lv_ref[...]

    def level_w(i):
        return w[i * CHUNK:(i + 1) * CHUNK]

    a = jnp.zeros((GLA_HEADS * CHUNK, CHUNK), F32)
    for i in range(_N_LEVEL):
        wi = level_w(i)
        qi = q_stack * jnp.concatenate([wi] * GLA_HEADS, axis=0)
        a = jnp.where(lv == i, _dot_nt(qi.astype(BF16), (k * wi).astype(BF16)), a)
    a = jnp.where(lv == _N_LEVEL, _dot_nt(q_stack.astype(BF16), k.astype(BF16)), a)

    w_pre = level_w(_N_LEVEL)
    w_suf = level_w(_N_LEVEL + 1)
    st = state[...]
    q_pre = q_stack * jnp.concatenate([w_pre] * GLA_HEADS, axis=0)
    o_inter = _dot_nt(q_pre.astype(BF16), st.astype(BF16))
    v_b = v.astype(BF16)
    for h in range(GLA_HEADS):
        rows = slice(h * CHUNK, (h + 1) * CHUNK)
        cols = slice(h * LANES, (h + 1) * LANES)
        o = _dot(a[rows].astype(BF16), v_b[:, cols]) + o_inter[rows]
        o = o * lax.rsqrt(jnp.mean(o * o, axis=-1, keepdims=True) + LN_EPS)
        o_ref[:, cols] = (o * ng_ref[:, cols] * _silu(gate[:, cols])).astype(o_ref.dtype)

    kvt = _dot(v.T.astype(BF16), (k * w_suf).astype(BF16))
    upd = jnp.zeros((LANES, GLA_KW), F32)
    for h in range(GLA_HEADS):
        upd = upd + jnp.where(heads[h], kvt[h * LANES:(h + 1) * LANES], 0.0)
    st_new = st * w_pre[CHUNK - 1:CHUNK, :] + upd
    state[...] = st_new

    @pl.when(ci == n_chunks - 1)
    def _():
        st_ref[...] = st_new.T


def _gla(ug, tail, w2, b2, norm_g, s0, tables, *, valid_len=CHUNK):
    bsz, s, _ = ug.shape
    n_chunks = s // CHUNK
    mall, lv = tables
    w2pad = jnp.zeros((LANES, GLA_KW), F32).at[N_FOX_HEADS:N_FOX_HEADS + GLA_RANK].set(w2)
    const2 = lambda b, i: (0, 0)
    col = lambda c: pl.BlockSpec((None, CHUNK, TN), lambda b, i: (b, i, c))
    o, st = pl.pallas_call(
        functools.partial(_gla_kernel, valid_len=valid_len, n_chunks=n_chunks),
        grid=(bsz, n_chunks),
        in_specs=[col(2), col(3), col(4),
                  pl.BlockSpec((None, CHUNK, LANES), lambda b, i: (b, i, 0)),
                  pl.BlockSpec((LANES, GLA_KW), const2),
                  pl.BlockSpec((1, GLA_KW), const2),
                  pl.BlockSpec((1, GLA_WIDTH), const2),
                  pl.BlockSpec((None, GLA_KW, LANES), lambda b, i: (b, 0, 0)),
                  pl.BlockSpec(mall.shape, const2),
                  pl.BlockSpec(lv.shape, const2)],
        out_specs=[pl.BlockSpec((None, CHUNK, GLA_WIDTH), lambda b, i: (b, i, 0)),
                   pl.BlockSpec((None, GLA_KW, LANES), lambda b, i: (b, 0, 0))],
        out_shape=[jax.ShapeDtypeStruct((bsz, s, GLA_WIDTH), BF16),
                   jax.ShapeDtypeStruct((bsz, GLA_KW, LANES), F32)],
        scratch_shapes=[pltpu.VMEM((LANES, GLA_KW), F32)],
        compiler_params=_cparams("parallel", "arbitrary"),
        name="gla",
    )(ug, ug, ug, tail, w2pad, b2.reshape(1, GLA_KW), norm_g.reshape(1, GLA_WIDTH),
      s0.reshape(bsz, GLA_KW, LANES), mall, lv)
    return o, st.reshape(bsz, GLA_HEADS, GLA_DK, LANES)


PAGES_PER_STEP = 16
PAGE_ROWS = PAGE * N_FOX_HEADS


def _dec_tables():
    t = np.arange(PAGE)
    after = (t[:, None] > t[None, :]).astype(np.float32)
    spread = (t[:, None] == (np.arange(PAGE_ROWS) // N_FOX_HEADS)[None, :]).astype(np.float32)
    return jnp.asarray(after, BF16), jnp.asarray(spread, BF16)


def _dec_kernel(pt_ref, *refs, n_steps):
    n = PAGES_PER_STEP
    f_refs, k_refs, v_refs = refs[:n], refs[n:2 * n], refs[2 * n:3 * n]
    q_ref, kn_ref, vn_ref, fn_ref, after_ref, spread_ref, o_ref, m_sc, l_sc, acc_sc, carry = refs[3 * n:]
    i = pl.program_id(1)
    scale = HEAD_DIM ** -0.5
    q = q_ref[...]

    @pl.when(i == 0)
    def _():
        s_self = jnp.sum(q.astype(BF16).astype(F32) * kn_ref[...].astype(BF16).astype(F32), axis=-1, keepdims=True) * scale
        m_sc[...] = jnp.broadcast_to(s_self, m_sc.shape)
        l_sc[...] = jnp.ones(l_sc.shape, F32)
        acc_sc[...] = vn_ref[...].astype(BF16).astype(F32)
        carry[...] = fn_ref[...]

    lhs = jnp.concatenate([f_ref[...] for f_ref in f_refs], axis=0)
    suffix = _dot_exact(lhs, after_ref[...])
    total = _dot_exact(lhs, jnp.ones((PAGE, LANES), BF16))
    run = carry[...]
    bias = [None] * n
    for j in reversed(range(n)):
        sl = slice(j * N_FOX_HEADS, (j + 1) * N_FOX_HEADS)
        bias[j] = suffix[sl] + run
        run = run + total[sl]
    carry[...] = run
    bias_x = _dot_exact(jnp.concatenate(bias, axis=0), spread_ref[...])

    q_b = q.astype(BF16)
    valid = lax.broadcasted_iota(jnp.int32, (N_FOX_HEADS, PAGE_ROWS), 1) % N_FOX_HEADS == \
        lax.broadcasted_iota(jnp.int32, (N_FOX_HEADS, PAGE_ROWS), 0)
    scores = []
    for j, k_ref in enumerate(k_refs):
        s = _dot_nt(q_b, k_ref[...].astype(BF16)) * scale + bias_x[j * N_FOX_HEADS:(j + 1) * N_FOX_HEADS]
        scores.append(jnp.where(valid, s, -jnp.inf))
    m_prev = m_sc[...]
    m_cur = scores[0]
    for s in scores[1:]:
        m_cur = jnp.maximum(m_cur, s)
    m_new = jnp.maximum(m_prev, jnp.max(m_cur, axis=1, keepdims=True))
    alpha = jnp.exp(m_prev - m_new)
    m_wide = jnp.concatenate([m_new] * (PAGE_ROWS // LANES), axis=1)
    acc = alpha * acc_sc[...]
    l_part = jnp.zeros((N_FOX_HEADS, PAGE_ROWS), F32)
    for s, v_ref in zip(scores, v_refs):
        p = jnp.exp(s - m_wide)
        l_part = l_part + p
        acc = acc + _dot(p.astype(BF16), v_ref[...].astype(BF16))
    l_new = alpha * l_sc[...] + jnp.sum(l_part, axis=1, keepdims=True)
    m_sc[...] = m_new
    l_sc[...] = l_new
    acc_sc[...] = acc

    @pl.when(i == n_steps - 1)
    def _():
        o_ref[...] = acc / l_new


def _fox_decode(page_table, k_pool, v_pool, f_pool, layer, q, k_new, v_new, logf_new_b, tables):
    bd, n_pages = page_table.shape
    n_steps = n_pages // PAGES_PER_STEP
    depth, n_pool = k_pool.shape[:2]
    k_view = k_pool.reshape(depth, n_pool, PAGE_ROWS, HEAD_DIM)
    v_view = v_pool.reshape(depth, n_pool, PAGE_ROWS, HEAD_DIM)
    f_view = jnp.transpose(f_pool, (0, 1, 3, 2))
    after, spread = tables

    def page_of(b, i, pt, j):
        return pt[b, (n_steps - 1 - i) * PAGES_PER_STEP + j]

    def page_spec(j, rows):
        return pl.BlockSpec((None, None, rows, LANES), lambda b, i, pt: (layer, page_of(b, i, pt, j), 0, 0))

    vec = pl.BlockSpec((None, N_FOX_HEADS, HEAD_DIM), lambda b, i, pt: (b, 0, 0))
    const = lambda a: pl.BlockSpec(a.shape, lambda b, i, pt: (0, 0))
    pages = range(PAGES_PER_STEP)
    grid_spec = pltpu.PrefetchScalarGridSpec(
        num_scalar_prefetch=1,
        grid=(bd, n_steps),
        in_specs=[page_spec(j, N_FOX_HEADS) for j in pages] + [page_spec(j, PAGE_ROWS) for j in pages] * 2
        + [vec, vec, vec, vec, const(after), const(spread)],
        out_specs=vec,
        scratch_shapes=[pltpu.VMEM((N_FOX_HEADS, LANES), F32)] * 4,
    )
    return pl.pallas_call(
        functools.partial(_dec_kernel, n_steps=n_steps),
        grid_spec=grid_spec,
        out_shape=jax.ShapeDtypeStruct((bd, N_FOX_HEADS, HEAD_DIM), F32),
        compiler_params=_cparams("parallel", "arbitrary"),
        name="fox_decode",
    )(page_table, *([f_view] * PAGES_PER_STEP), *([k_view] * PAGES_PER_STEP), *([v_view] * PAGES_PER_STEP),
      q, k_new, v_new, logf_new_b, after, spread)


def _pad_rows(x, rows):
    return jnp.pad(x[:, None, :], ((0, 0), (0, rows - 1), (0, 0)))


def _rows16(x):
    return jnp.pad(x, ((0, SROWS - x.shape[0]), (0, 0)))


def kernel(x_prompt, x_sample, cache_k, cache_v, cache_logf, state_gla, page_table, c_prompt, c_sample,
           ln_in_g, ln_in_b, w_ada, b_ada, w_in, b_f, gm_ln_g, gm_ln_b, gm_ws, gm_bs, gla_w2, gla_b2,
           gla_norm_g, w_o, ln1_g, ln1_b, w_gu, w_down, ln2_g, ln2_b):
    bp, seq, d = x_prompt.shape
    bd = x_sample.shape[0]
    depth = w_ada.shape[0]
    assert d == 4 * GLA_WIDTH and x_sample.shape[1] == 1 and bp + bd <= 16 and bd <= SROWS
    assert w_in.shape[2] == IN_WIDTH
    alpha = (2 * depth) ** 0.25
    m_p = bp * seq

    gla_tables = _gla_tables()
    dec_tables = _dec_tables()
    ext_tables = _ext_tables()

    c16 = jnp.zeros((SROWS, d), F32).at[:bd].set(c_sample).at[bd:bd + bp].set(c_prompt)
    mods = _ada(c16, w_ada, b_ada)

    def mod_p(l, which):
        return mods[l, bd:bd + bp, which * d:(which + 1) * d][:, None, :]

    def mod_s(l, *which):
        return jnp.concatenate([mods[l, :, w * d:(w + 1) * d] for w in which], axis=-1)

    w_in_b = _pack_in(w_in)
    w_o_b = _cast(w_o)
    w_down_b = _cast(w_down)

    xp, hp = _ln_mod(x_prompt, ln_in_g, ln_in_b, mod=(mod_p(0, 1), mod_p(0, 0)), tr=256)
    xs, hs = _ln_mod(x_sample, ln_in_g, ln_in_b,
                     mod=(mods[0, :bd, d:2 * d][:, None, :], mods[0, :bd, 0:d][:, None, :]), tr=1)
    xp, hp = xp.reshape(m_p, d), hp.reshape(m_p, d)
    xs, hs = _rows16(xs[:, 0, :]), _rows16(hs[:, 0, :])

    zero_state = jnp.zeros((bp, GLA_HEADS, GLA_DK, LANES), F32)
    k_all = v_all = None
    fp, sp = [], []
    kd, vd, fd, sd, gd = [], [], [], [], []
    for l in range(depth):
        last = l == depth - 1
        q16, k_all, k16, v_all, vt16, ug, tail, samp = _proj(hp, hs, w_in_b, l, k_all, v_all, depth, tm=256)

        samp = samp[:bd]
        heads3 = lambda a: a.reshape(bd, N_FOX_HEADS, HEAD_DIM)
        q_s = heads3(samp[:, 0:FOX_WIDTH])
        k_s = heads3(samp[:, FOX_WIDTH:2 * FOX_WIDTH])
        v_s = heads3(samp[:, 2 * FOX_WIDTH:3 * FOX_WIDTH])
        ug_s = _pad_rows(samp[:, 6 * TN:11 * TN], CHUNK)
        tail_s = _pad_rows(samp[:, 11 * TN:11 * TN + LANES], CHUNK)
        logf_s, _, _ = _fox_cum(tail_s, b_f[l], ext_tables)
        logf_s = logf_s[:, 0, :]
        fox_s = _fox_decode(page_table, cache_k, cache_v, cache_logf, l, q_s, k_s, v_s,
                            jnp.broadcast_to(logf_s[:, :, None], (bd, N_FOX_HEADS, LANES)), dec_tables)
        gm_s, vn_s = _gmlp(ug_s, gm_ln_g[l], gm_ln_b[l], gm_ws[l], gm_bs[l], emit_vn=True)
        gla_s, st_s = _gla(ug_s, tail_s, gla_w2[l], gla_b2[l], gla_norm_g[l], state_gla[l], gla_tables, valid_len=1)
        mix_s = jnp.concatenate([fox_s.reshape(bd, -1).astype(BF16), gm_s[:, 0, :], gla_s[:, 0, :]], axis=-1)

        tail3 = tail.reshape(bp, seq, LANES)
        ug3 = ug.reshape(bp, seq, UG_WIDTH)
        logf, ext_q, ext_k = _fox_cum(tail3, b_f[l], ext_tables)
        fox_o = _fox_prompt(q16.reshape(bp, seq, FOX_WIDTH), k16.reshape(bp, seq, FOX_WIDTH), vt16, ext_q, ext_k)
        gm_o, _ = _gmlp(ug3, gm_ln_g[l], gm_ln_b[l], gm_ws[l], gm_bs[l], emit_vn=False)
        gla_o, st = _gla(ug3, tail3, gla_w2[l], gla_b2[l], gla_norm_g[l], zero_state, gla_tables)

        x1, h1, x1_s, h1_s = _dense_ln(
            [fox_o.reshape(m_p, -1), gm_o.reshape(m_p, -1), gla_o.reshape(m_p, -1)], _rows16(mix_s), w_o_b, l,
            xp, xs, mod_p(l, 2), ln1_g[l], ln1_b[l], (mod_p(l, 4), mod_p(l, 3)), mod_s(l, 2, 4, 3),
            alpha=alpha, rows_per_batch=seq, tm=256)
        act, act_s = _swiglu(h1, h1_s, w_gu, l, tm=1024, tn=512)
        nxt = None if last else (mod_p(l + 1, 1), mod_p(l + 1, 0))
        nxt_s = mod_s(l, 5, 5, 5) if last else jnp.concatenate([mod_s(l, 5), mod_s(l + 1, 1, 0)], axis=-1)
        xp, hp, xs, hs = _dense_ln([act], act_s, w_down_b, l, x1, x1_s, mod_p(l, 5), ln2_g[l], ln2_b[l], nxt, nxt_s,
                                   alpha=alpha, rows_per_batch=seq, tm=256)

        fp.append(logf)
        sp.append(st)
        kd.append(k_s[:, None])
        vd.append(v_s[:, None])
        fd.append(logf_s[:, None, :])
        sd.append(st_s)
        gd.append(vn_s[:, 0:1, :])

    kv_shape = (depth, bp, seq, N_FOX_HEADS, HEAD_DIM)
    return (xp.reshape(bp, seq, d), xs[:bd, None, :], k_all.reshape(kv_shape), v_all.reshape(kv_shape),
            jnp.stack(fp), jnp.stack(sp), jnp.stack(kd), jnp.stack(vd), jnp.stack(fd), jnp.stack(sd), jnp.stack(gd))
```

```python
import functools
import math

import numpy as np
import jax
import jax.numpy as jnp
from jax import lax
from jax.experimental import pallas as pl
from jax.experimental.pallas import tpu as pltpu

F32 = jnp.float32
BF16 = jnp.bfloat16

LANES = 128
HEAD_DIM = 128
N_FOX_HEADS = 8
FOX_WIDTH = N_FOX_HEADS * HEAD_DIM
GMLP_GROUPS = 4
GMLP_WIDTH = GMLP_GROUPS * LANES
GLA_HEADS = 4
GLA_DK = 64
GLA_KW = GLA_HEADS * GLA_DK
GLA_WIDTH = GLA_HEADS * LANES
GLA_RANK = 16
GLA_TAU = 16.0
CHUNK = 128
PAGE = 128
LN_EPS = 1e-5
LOG2E = math.log2(math.e)
VMEM_LIMIT = 56 * 1024 * 1024
SROWS = 16

TN = 512
OFF_FF = 3 * FOX_WIDTH
OFF_MU = OFF_FF + N_FOX_HEADS
OFF_GR = OFF_MU + 2 * GMLP_WIDTH + 2 * GLA_KW + GLA_WIDTH
OFF_GG = OFF_GR + GLA_RANK
IN_WIDTH = OFF_GG + GLA_WIDTH
N_PROJ_TILES = 12
UG_WIDTH = 5 * TN


def _cparams(*sem):
    return pltpu.CompilerParams(dimension_semantics=sem, vmem_limit_bytes=VMEM_LIMIT)


def _log_sigmoid(x):
    return jnp.minimum(x, 0.0) - jnp.log1p(jnp.exp(-jnp.abs(x)))


def _silu(x):
    return x * (1.0 / (1.0 + jnp.exp(-x)))


def _split3(x):
    hi = x.astype(BF16)
    r = x - hi.astype(F32)
    mid = r.astype(BF16)
    lo = (r - mid.astype(F32)).astype(BF16)
    return hi, mid, lo


def _dot(a, b):
    return jnp.dot(a, b, preferred_element_type=F32)


def _dot_nt(a, b):
    return lax.dot_general(a, b, (((1,), (1,)), ((), ())), preferred_element_type=F32)


def _dot_exact(x, w):
    hi, mid, lo = _split3(x)
    return _dot(hi, w) + _dot(mid, w) + _dot(lo, w)


def _dot_exact_l(w, x):
    hi, mid, lo = _split3(x)
    return _dot(w, hi) + _dot(w, mid) + _dot(w, lo)


def _ada_kernel(c_ref, w_ref, b_ref, o_ref):
    a = _silu(c_ref[...]).astype(BF16)
    o_ref[...] = _dot(a, w_ref[...].astype(BF16)) + b_ref[...]


def _ada(c16, w_ada, b_ada):
    depth, d, n = w_ada.shape
    tn = 1024
    return pl.pallas_call(
        _ada_kernel,
        grid=(depth, n // tn),
        in_specs=[pl.BlockSpec((16, d), lambda l, j: (0, 0)),
                  pl.BlockSpec((None, d, tn), lambda l, j: (l, 0, j)),
                  pl.BlockSpec((None, 1, tn), lambda l, j: (l, 0, j))],
        out_specs=pl.BlockSpec((None, 16, tn), lambda l, j: (l, 0, j)),
        out_shape=jax.ShapeDtypeStruct((depth, 16, n), F32),
        compiler_params=_cparams("parallel", "parallel"),
        name="ada",
    )(c16, w_ada, b_ada.reshape(depth, 1, n))


def _ln_kernel(*refs, has_res, has_mod, alpha):
    it = iter(refs)
    x = next(it)[...]
    if has_res:
        y = next(it)[...]
        gate = next(it)[...]
        x = alpha * x + gate * y
    g = next(it)[...]
    b = next(it)[...]
    if has_mod:
        sc = next(it)[...]
        sh = next(it)[...]
    xn_ref = next(it)
    mu = jnp.mean(x, axis=-1, keepdims=True)
    xc = x - mu
    var = jnp.mean(xc * xc, axis=-1, keepdims=True)
    xn = xc * lax.rsqrt(var + LN_EPS) * g + b
    xn_ref[...] = xn
    if has_mod:
        h_ref = next(it)
        h_ref[...] = (xn * (1.0 + sc) + sh).astype(BF16)


def _ln_mod(x, g, b, *, res=None, mod=None, alpha=1.0, tr):
    bsz, s, d = x.shape
    tr = min(tr, s)
    row = pl.BlockSpec((None, tr, d), lambda i, j: (i, j, 0))
    per_b = pl.BlockSpec((None, 1, d), lambda i, j: (i, 0, 0))
    vec = pl.BlockSpec((1, d), lambda i, j: (0, 0))
    args, specs = [x], [row]
    if res is not None:
        y, gate = res
        args += [y, gate]
        specs += [row, per_b]
    args += [g.reshape(1, d), b.reshape(1, d)]
    specs += [vec, vec]
    out_shape = [jax.ShapeDtypeStruct((bsz, s, d), F32)]
    out_specs = [row]
    if mod is not None:
        sc, sh = mod
        args += [sc, sh]
        specs += [per_b, per_b]
        out_shape.append(jax.ShapeDtypeStruct((bsz, s, d), BF16))
        out_specs.append(row)
    out = pl.pallas_call(
        functools.partial(_ln_kernel, has_res=res is not None, has_mod=mod is not None, alpha=alpha),
        grid=(bsz, s // tr),
        in_specs=specs, out_specs=out_specs, out_shape=out_shape,
        compiler_params=_cparams("parallel", "parallel"),
        name="ln_mod",
    )(*args)
    return out if mod is not None else (out[0], None)


PACK_EXTRA = 32


def _pack_in_kernel(wm_ref, wn_ref, wg_ref, o_ref):
    j = pl.program_id(1)
    k_dim = wm_ref.shape[1]

    @pl.when(j < 6)
    def _():
        o_ref[...] = wm_ref[...].T.astype(BF16)

    def shifted(rows):
        return jnp.concatenate([wm_ref[rows:TN, :], wn_ref[0:rows, :]], axis=0).T.astype(BF16)

    @pl.when((j >= 6) & (j < 10))
    def _():
        o_ref[...] = shifted(OFF_MU - OFF_FF)

    @pl.when(j == 10)
    def _():
        o_ref[...] = shifted(OFF_GG - 10 * TN)

    @pl.when(j == N_PROJ_TILES - 1)
    def _():
        lo = OFF_GR - (OFF_GR // PACK_EXTRA) * PACK_EXTRA
        t = jnp.concatenate([wn_ref[0:N_FOX_HEADS, :], wg_ref[lo:lo + GLA_RANK, :],
                             jnp.zeros((TN - N_FOX_HEADS - GLA_RANK, k_dim), F32)], axis=0)
        o_ref[...] = t.T.astype(BF16)


def _pack_in(w_in):
    depth, k, _ = w_in.shape
    w_t = jnp.transpose(w_in, (0, 2, 1))
    per = TN // PACK_EXTRA
    nb = lambda j: jnp.where((j >= 6) & (j < N_PROJ_TILES - 1), per * (j + 1), OFF_FF // PACK_EXTRA)
    return pl.pallas_call(
        _pack_in_kernel,
        grid=(depth, N_PROJ_TILES),
        in_specs=[pl.BlockSpec((None, TN, k), lambda l, j: (l, jnp.minimum(j, 10), 0)),
                  pl.BlockSpec((None, PACK_EXTRA, k), lambda l, j: (l, nb(j), 0)),
                  pl.BlockSpec((None, PACK_EXTRA, k), lambda l, j: (l, OFF_GR // PACK_EXTRA, 0))],
        out_specs=pl.BlockSpec((None, k, TN), lambda l, j: (l, 0, j)),
        out_shape=jax.ShapeDtypeStruct((depth, k, N_PROJ_TILES * TN), BF16),
        compiler_params=_cparams("parallel", "parallel"),
        name="pack_in",
    )(w_t, w_t, w_t)


def _cast_kernel(w_ref, o_ref):
    o_ref[...] = w_ref[...].astype(BF16)


def _cast(w):
    depth, k, n = w.shape
    blk = pl.BlockSpec((None, k, TN), lambda l, j: (l, 0, j))
    return pl.pallas_call(
        _cast_kernel, grid=(depth, n // TN), in_specs=[blk], out_specs=blk,
        out_shape=jax.ShapeDtypeStruct(w.shape, BF16),
        compiler_params=_cparams("parallel", "parallel"), name="cast",
    )(w)


def _proj_kernel(*refs, aliased, q_scale, tm):
    x_ref, xs_ref, w_ref = refs[:3]
    q_ref, k32_ref, k16_ref, v32_ref, vt_ref, ug_ref, t_ref, s_ref = refs[3 + (2 if aliased else 0):]
    x = x_ref[...]
    q_ref[...] = (_dot(x, w_ref[:, 0:FOX_WIDTH]) * q_scale).astype(BF16)
    k = _dot(x, w_ref[:, FOX_WIDTH:2 * FOX_WIDTH])
    k16_ref[...] = k.astype(BF16)
    v = _dot(x, w_ref[:, 2 * FOX_WIDTH:3 * FOX_WIDTH])
    vt_ref[...] = v.T.astype(BF16)
    for h in range(N_FOX_HEADS):
        cols = slice(h * HEAD_DIM, (h + 1) * HEAD_DIM)
        k32_ref[pl.ds(h, tm, stride=N_FOX_HEADS), :] = k[:, cols]
        v32_ref[pl.ds(h, tm, stride=N_FOX_HEADS), :] = v[:, cols]
    ug_ref[...] = _dot(x, w_ref[:, 6 * TN:11 * TN])
    t_ref[...] = _dot(x, w_ref[:, 11 * TN:11 * TN + LANES])

    @pl.when(pl.program_id(0) == 0)
    def _():
        s_ref[...] = _dot(xs_ref[...], w_ref[...])


def _proj(x, xs, w_in, layer, k_prev, v_prev, depth, *, tm):
    m, k = x.shape
    tm = min(tm, m)
    n_all = w_in.shape[2]
    in_specs = [pl.BlockSpec((tm, k), lambda i: (i, 0)),
                pl.BlockSpec((SROWS, k), lambda i: (0, 0)),
                pl.BlockSpec((None, k, n_all), lambda i: (layer, 0, 0), pipeline_mode=pl.Buffered(1))]
    args = [x, xs, w_in]
    aliases = {}
    if k_prev is not None:
        in_specs += [pl.BlockSpec(memory_space=pl.ANY)] * 2
        args += [k_prev, v_prev]
        aliases = {3: 1, 4: 3}
    rows = pl.BlockSpec((tm, FOX_WIDTH), lambda i: (i, 0))
    pairs = pl.BlockSpec((None, tm * N_FOX_HEADS, HEAD_DIM), lambda i: (layer, i, 0))
    out_shape = [jax.ShapeDtypeStruct((m, FOX_WIDTH), BF16),
                 jax.ShapeDtypeStruct((depth, m * N_FOX_HEADS, HEAD_DIM), F32),
                 jax.ShapeDtypeStruct((m, FOX_WIDTH), BF16),
                 jax.ShapeDtypeStruct((depth, m * N_FOX_HEADS, HEAD_DIM), F32),
                 jax.ShapeDtypeStruct((FOX_WIDTH, m), BF16),
                 jax.ShapeDtypeStruct((m, UG_WIDTH), F32),
                 jax.ShapeDtypeStruct((m, LANES), F32),
                 jax.ShapeDtypeStruct((SROWS, n_all), F32)]
    out_specs = [rows, pairs, rows, pairs,
                 pl.BlockSpec((FOX_WIDTH, tm), lambda i: (0, i)),
                 pl.BlockSpec((tm, UG_WIDTH), lambda i: (i, 0)),
                 pl.BlockSpec((tm, LANES), lambda i: (i, 0)),
                 pl.BlockSpec((SROWS, n_all), lambda i: (0, 0))]
    return pl.pallas_call(
        functools.partial(_proj_kernel, aliased=k_prev is not None, q_scale=HEAD_DIM ** -0.5 * LOG2E, tm=tm),
        grid=(m // tm,),
        in_specs=in_specs, out_specs=out_specs, out_shape=out_shape,
        input_output_aliases=aliases,
        compiler_params=_cparams("arbitrary"),
        name="proj",
    )(*args)


SWIGLU_SUB = 1024


def _swiglu_kernel(x_ref, xs_ref, wg_ref, wu_ref, o_ref, os_ref, wbg, wbu):
    def result(x):
        return (_silu(_dot(x, wbg[...])) * _dot(x, wbu[...])).astype(BF16)

    @pl.when(pl.program_id(1) == 0)
    def _():
        wbg[...] = wg_ref[...].astype(BF16)
        wbu[...] = wu_ref[...].astype(BF16)
        os_ref[...] = result(xs_ref[...])

    sub = min(SWIGLU_SUB, x_ref.shape[0])
    for r0 in range(0, x_ref.shape[0], sub):
        o_ref[r0:r0 + sub, :] = result(x_ref[r0:r0 + sub, :])


def _swiglu(x, x_sample, w_gu, layer, *, tm, tn):
    m, k = x.shape
    tm = min(tm, m)
    n = w_gu.shape[2] // 2
    up0 = n // tn
    return pl.pallas_call(
        _swiglu_kernel,
        grid=(n // tn, m // tm),
        in_specs=[pl.BlockSpec((tm, k), lambda j, i: (i, 0)),
                  pl.BlockSpec((SROWS, k), lambda j, i: (0, 0)),
                  pl.BlockSpec((None, k, tn), lambda j, i: (layer, 0, j)),
                  pl.BlockSpec((None, k, tn), lambda j, i: (layer, 0, up0 + j))],
        out_specs=[pl.BlockSpec((tm, tn), lambda j, i: (i, j)), pl.BlockSpec((SROWS, tn), lambda j, i: (0, j))],
        out_shape=[jax.ShapeDtypeStruct((m, n), BF16), jax.ShapeDtypeStruct((SROWS, n), BF16)],
        scratch_shapes=[pltpu.VMEM((k, tn), BF16)] * 2,
        compiler_params=_cparams("arbitrary", "arbitrary"),
        name="swiglu",
    )(x, x_sample, w_gu, w_gu)


DENSE_LN_SUB = 128


def _layer_norm(z, g, b):
    mu = jnp.mean(z, axis=-1, keepdims=True)
    zc = z - mu
    var = jnp.mean(zc * zc, axis=-1, keepdims=True)
    return zc * lax.rsqrt(var + LN_EPS) * g + b


def _dense_ln_kernel(*refs, n_lhs, has_mod, alpha):
    it = iter(refs)
    x_refs = [next(it) for _ in range(n_lhs)]
    xs_ref, w_ref, res_ref, gate_ref, g_ref, b_ref = [next(it) for _ in range(6)]
    sc_ref, sh_ref = (next(it), next(it)) if has_mod else (None, None)
    res_s_ref, mod_s_ref = next(it), next(it)
    xn_ref = next(it)
    h_ref = next(it) if has_mod else None
    xn_s_ref = next(it)
    h_s_ref = next(it) if has_mod else None
    d = xn_ref.shape[-1]

    def finish(lhs, res, gate, sc, sh):
        xn = _layer_norm(alpha * res + gate * _dot(lhs, w_ref[...]), g_ref[...], b_ref[...])
        return xn, ((xn * (1.0 + sc) + sh).astype(BF16) if has_mod else None)

    @pl.when(pl.program_id(0) == 0)
    def _():
        mod_s = mod_s_ref[...]
        xn, h = finish(xs_ref[...], res_s_ref[...], mod_s[:, 0:d], mod_s[:, d:2 * d], mod_s[:, 2 * d:3 * d])
        xn_s_ref[...] = xn
        if has_mod:
            h_s_ref[...] = h

    tm = xn_ref.shape[0]
    sub = min(tm, DENSE_LN_SUB)
    for r0 in range(0, tm, sub):
        rows = slice(r0, r0 + sub)
        parts = [x[rows, :] for x in x_refs]
        lhs = parts[0] if n_lhs == 1 else jnp.concatenate(parts, axis=1)
        xn, h = finish(lhs, res_ref[rows, :], gate_ref[...], sc_ref[...] if has_mod else None,
                       sh_ref[...] if has_mod else None)
        xn_ref[rows, :] = xn
        if has_mod:
            h_ref[rows, :] = h


def _dense_ln(xs_list, x_sample, w, layer, res, res_s, gate, ln_g, ln_b, mod, mod_s, *, alpha, rows_per_batch, tm):
    m = xs_list[0].shape[0]
    tm = min(tm, rows_per_batch)
    k, d = w.shape[1], w.shape[2]
    assert sum(x.shape[1] for x in xs_list) == k and x_sample.shape == (SROWS, k)
    steps_per_batch = rows_per_batch // tm
    has_mod = mod is not None
    row = pl.BlockSpec((tm, d), lambda i: (i, 0))
    per_b = pl.BlockSpec((None, 1, d), lambda i: (i // steps_per_batch, 0, 0))
    vec = pl.BlockSpec((1, d), lambda i: (0, 0))
    srow = pl.BlockSpec((SROWS, d), lambda i: (0, 0))
    in_specs = [pl.BlockSpec((tm, x.shape[1]), lambda i: (i, 0)) for x in xs_list]
    in_specs += [pl.BlockSpec((SROWS, k), lambda i: (0, 0)),
                 pl.BlockSpec((None, k, d), lambda i: (layer, 0, 0), pipeline_mode=pl.Buffered(1)),
                 row, per_b, vec, vec]
    args = list(xs_list) + [x_sample, w, res, gate, ln_g.reshape(1, d), ln_b.reshape(1, d)]
    if has_mod:
        in_specs += [per_b, per_b]
        args += list(mod)
    in_specs += [srow, pl.BlockSpec((SROWS, 3 * d), lambda i: (0, 0))]
    args += [res_s, mod_s]
    out_specs, out_shape = [row], [jax.ShapeDtypeStruct((m, d), F32)]
    if has_mod:
        out_specs.append(row)
        out_shape.append(jax.ShapeDtypeStruct((m, d), BF16))
    out_specs.append(srow)
    out_shape.append(jax.ShapeDtypeStruct((SROWS, d), F32))
    if has_mod:
        out_specs.append(srow)
        out_shape.append(jax.ShapeDtypeStruct((SROWS, d), BF16))
    out = pl.pallas_call(
        functools.partial(_dense_ln_kernel, n_lhs=len(xs_list), has_mod=has_mod, alpha=alpha),
        grid=(m // tm,),
        in_specs=in_specs, out_specs=out_specs, out_shape=out_shape,
        compiler_params=_cparams("arbitrary"),
        name="dense_ln",
    )(*args)
    return out if has_mod else (out[0], None, out[1], None)


EXT_PER_HEAD = 6


def _ext_tables():
    pq = np.zeros((3 * LANES, LANES), np.float32)
    pk = np.zeros((3 * LANES, LANES), np.float32)
    oq = np.zeros((1, LANES), np.float32)
    ok = np.zeros((1, LANES), np.float32)
    for h in range(N_FOX_HEADS):
        for part in range(3):
            pq[part * LANES + h, EXT_PER_HEAD * h + part] = 1.0
            pk[part * LANES + h, EXT_PER_HEAD * h + 3 + part] = -1.0
            oq[0, EXT_PER_HEAD * h + 3 + part] = 1.0
            ok[0, EXT_PER_HEAD * h + part] = 1.0
    return jnp.asarray(pq, BF16), jnp.asarray(pk, BF16), jnp.asarray(oq), jnp.asarray(ok)


def _fox_cum_kernel(t_ref, bf_ref, pq_ref, pk_ref, oq_ref, ok_ref, lf_ref, eq_ref, ek_ref, carry, *, tb):
    @pl.when(pl.program_id(1) == 0)
    def _():
        carry[...] = jnp.zeros(carry.shape, F32)

    lf = _log_sigmoid(t_ref[...] + bf_ref[...])
    lf_ref[...] = lf[:, 0:N_FOX_HEADS]
    r = lax.broadcasted_iota(jnp.int32, (tb, tb), 0)
    c = lax.broadcasted_iota(jnp.int32, (tb, tb), 1)
    lower = (c <= r).astype(BF16)
    cum = _dot_exact_l(lower, lf) + carry[0:1, :]
    carry[...] = jnp.broadcast_to(cum[tb - 1:tb, :], carry.shape)
    hi, mid, lo = _split3(cum * LOG2E)
    cat = jnp.concatenate([hi, mid, lo], axis=1)
    eq_ref[...] = (_dot(cat, pq_ref[...]) + oq_ref[...]).astype(BF16)
    ek_ref[...] = (_dot(cat, pk_ref[...]) + ok_ref[...]).astype(BF16)


def _fox_cum(tail, b_f, tables):
    bsz, s, _ = tail.shape
    tb = min(512, s)
    pq, pk, oq, ok = tables
    bf = jnp.zeros((1, LANES), F32).at[0, :N_FOX_HEADS].set(b_f)
    blk = pl.BlockSpec((None, tb, LANES), lambda b, i: (b, i, 0))
    const = lambda shape: pl.BlockSpec(shape, lambda b, i: (0, 0))
    return pl.pallas_call(
        functools.partial(_fox_cum_kernel, tb=tb),
        grid=(bsz, s // tb),
        in_specs=[blk, const((1, LANES)), const(pq.shape), const(pk.shape), const((1, LANES)), const((1, LANES))],
        out_specs=[pl.BlockSpec((None, tb, N_FOX_HEADS), lambda b, i: (b, i, 0)), blk, blk],
        out_shape=[jax.ShapeDtypeStruct((bsz, s, N_FOX_HEADS), F32),
                   jax.ShapeDtypeStruct((bsz, s, LANES), BF16),
                   jax.ShapeDtypeStruct((bsz, s, LANES), BF16)],
        scratch_shapes=[pltpu.VMEM((8, LANES), F32)],
        compiler_params=_cparams("parallel", "arbitrary"),
        name="fox_cum",
    )(tail, bf, pq, pk, oq, ok)


HEADS_PER_STEP = 4
ATTN_TILE = 1024


def _fox_kernel(qi_ref, kj_ref, q_ref, k_ref, vt_ref, eq_ref, ek_ref, o_ref, m_sc, l_sc, acc_sc, qx_sc, *, tq, tk):
    grp = pl.program_id(1)
    p = pl.program_id(2)
    qi = qi_ref[p]
    kj = kj_ref[p]

    @pl.when(kj == 0)
    def _():
        m_sc[...] = jnp.full(m_sc.shape, -jnp.inf, F32)
        l_sc[...] = jnp.zeros(l_sc.shape, F32)
        acc_sc[...] = jnp.zeros(acc_sc.shape, F32)
        lane = lax.broadcasted_iota(jnp.int32, (tq, LANES), 1)
        eq = eq_ref[...]
        for hh in range(HEADS_PER_STEP):
            lo = (grp * HEADS_PER_STEP + hh) * EXT_PER_HEAD
            own = (lane >= lo) & (lane < lo + EXT_PER_HEAD)
            qx_sc[hh] = jnp.concatenate([q_ref[:, hh * LANES:(hh + 1) * LANES],
                                         jnp.where(own, eq, jnp.zeros_like(eq))], axis=1)

    def step(diagonal):
        ek = ek_ref[...]
        for hh in range(HEADS_PER_STEP):
            cols = slice(hh * LANES, (hh + 1) * LANES)
            kx = jnp.concatenate([k_ref[:, cols], ek], axis=1)
            s = _dot_nt(kx, qx_sc[hh])
            if diagonal:
                key = lax.broadcasted_iota(jnp.int32, (tk, tq), 0)
                qry = lax.broadcasted_iota(jnp.int32, (tk, tq), 1)
                s = jnp.where(key <= qry, s, -jnp.inf)
            m_prev = m_sc[hh][0:1, :]
            m_new = jnp.maximum(m_prev, jnp.max(s, axis=0, keepdims=True))
            alpha = jnp.exp2(m_prev - m_new)
            pr = jnp.exp2(s - m_new)
            l_new = alpha * l_sc[hh][0:1, :] + jnp.sum(pr, axis=0, keepdims=True)
            acc = alpha * acc_sc[hh] + _dot(vt_ref[cols, :], pr.astype(BF16))
            m_sc[hh] = jnp.broadcast_to(m_new, (8, tq))
            l_sc[hh] = jnp.broadcast_to(l_new, (8, tq))
            acc_sc[hh] = acc
            if diagonal:
                o_ref[:, cols] = (acc / l_new).T.astype(o_ref.dtype)

    @pl.when(kj == qi)
    def _():
        step(True)

    @pl.when(kj != qi)
    def _():
        step(False)


def _fox_prompt(q, k, vt, ext_q, ext_k):
    bsz, s, _ = q.shape
    tq = tk = min(ATTN_TILE, s)
    nq = s // tq
    width = HEADS_PER_STEP * LANES
    pairs = [(i, j) for i in range(nq) for j in range(i + 1)]
    qi_tbl = jnp.asarray([pr[0] for pr in pairs], jnp.int32)
    kj_tbl = jnp.asarray([pr[1] for pr in pairs], jnp.int32)
    grid_spec = pltpu.PrefetchScalarGridSpec(
        num_scalar_prefetch=2,
        grid=(bsz, N_FOX_HEADS // HEADS_PER_STEP, len(pairs)),
        in_specs=[pl.BlockSpec((None, tq, width), lambda b, g, p, qi, kj: (b, qi[p], g)),
                  pl.BlockSpec((None, tk, width), lambda b, g, p, qi, kj: (b, kj[p], g)),
                  pl.BlockSpec((width, tk), lambda b, g, p, qi, kj: (g, b * nq + kj[p])),
                  pl.BlockSpec((None, tq, LANES), lambda b, g, p, qi, kj: (b, qi[p], 0)),
                  pl.BlockSpec((None, tk, LANES), lambda b, g, p, qi, kj: (b, kj[p], 0))],
        out_specs=pl.BlockSpec((None, tq, width), lambda b, g, p, qi, kj: (b, qi[p], g)),
        scratch_shapes=[pltpu.VMEM((HEADS_PER_STEP, 8, tq), F32),
                        pltpu.VMEM((HEADS_PER_STEP, 8, tq), F32),
                        pltpu.VMEM((HEADS_PER_STEP, LANES, tq), F32),
                        pltpu.VMEM((HEADS_PER_STEP, tq, 2 * LANES), BF16)],
    )
    return pl.pallas_call(
        functools.partial(_fox_kernel, tq=tq, tk=tk),
        grid_spec=grid_spec,
        out_shape=jax.ShapeDtypeStruct((bsz, s, FOX_WIDTH), BF16),
        compiler_params=_cparams("parallel", "parallel", "arbitrary"),
        name="fox_prompt",
    )(qi_tbl, kj_tbl, q, k, vt, ext_q, ext_k)


GMLP_CHUNKS_PER_STEP = 4


def _gmlp_kernel(uv_ref, g_ref, b_ref, ws_ref, bs_ref, o_ref, *vn_ref):
    r = lax.broadcasted_iota(jnp.int32, (CHUNK, CHUNK), 0)
    c = lax.broadcasted_iota(jnp.int32, (CHUNK, CHUNK), 1)
    for grp in range(GMLP_GROUPS):
        lo = grp * LANES
        w = jnp.where(c <= r, ws_ref[grp], 0.0).astype(BF16)
        for r0 in range(0, uv_ref.shape[0], CHUNK):
            rows = slice(r0, r0 + CHUNK)
            u = jax.nn.gelu(uv_ref[rows, lo:lo + LANES])
            v = jax.nn.gelu(uv_ref[rows, GMLP_WIDTH + lo:GMLP_WIDTH + lo + LANES])
            mu = jnp.mean(v, axis=-1, keepdims=True)
            vc = v - mu
            var = jnp.mean(vc * vc, axis=-1, keepdims=True)
            vn = vc * lax.rsqrt(var + LN_EPS) * g_ref[:, lo:lo + LANES] + b_ref[:, lo:lo + LANES]
            mixed = _dot(w, vn.astype(BF16)) + bs_ref[grp]
            o_ref[rows, lo:lo + LANES] = (u * mixed).astype(o_ref.dtype)
            if vn_ref:
                vn_ref[0][rows, lo:lo + LANES] = vn


def _gmlp(ug, ln_g, ln_b, ws, bs, *, emit_vn):
    bsz, s, _ = ug.shape
    rows = CHUNK * min(GMLP_CHUNKS_PER_STEP, s // CHUNK)
    blk = pl.BlockSpec((None, rows, GMLP_WIDTH), lambda b, i: (b, i, 0))
    vec = pl.BlockSpec((1, GMLP_WIDTH), lambda b, i: (0, 0))
    cube = pl.BlockSpec((GMLP_GROUPS, CHUNK, CHUNK), lambda b, i: (0, 0, 0))
    out_shape = [jax.ShapeDtypeStruct((bsz, s, GMLP_WIDTH), BF16)]
    if emit_vn:
        out_shape.append(jax.ShapeDtypeStruct((bsz, s, GMLP_WIDTH), F32))
    bs_b = jnp.broadcast_to(bs[:, :, None], (GMLP_GROUPS, CHUNK, LANES))
    out = pl.pallas_call(
        _gmlp_kernel,
        grid=(bsz, s // rows),
        in_specs=[pl.BlockSpec((None, rows, 2 * GMLP_WIDTH), lambda b, i: (b, i, 0)), vec, vec, cube, cube],
        out_specs=[blk] * len(out_shape),
        out_shape=out_shape,
        compiler_params=_cparams("parallel", "parallel"),
        name="gmlp",
    )(ug, ln_g.reshape(1, GMLP_WIDTH), ln_b.reshape(1, GMLP_WIDTH), ws, bs_b)
    return out if emit_vn else (out[0], None)


_GLA_LEVELS = (64, 32, 16, 8, 4, 2, 1)
_N_LEVEL = len(_GLA_LEVELS)


def _gla_tables():
    x = np.arange(CHUNK)[:, None]
    j = np.arange(CHUNK)[None, :]
    groups = []
    lv = np.full((CHUNK, CHUNK), -1, np.int32)
    for i, half in enumerate(_GLA_LEVELS):
        anchor = (x // (2 * half)) * (2 * half) + half - 1
        groups.append((j > np.minimum(x, anchor)) & (j <= np.maximum(x, anchor)))
        same = (x // (2 * half)) == (j // (2 * half))
        upper_t = (x % (2 * half)) >= half
        lower_s = (j % (2 * half)) < half
        lv[same & upper_t & lower_s] = i
    lv[np.arange(CHUNK), np.arange(CHUNK)] = _N_LEVEL
    groups.append(j <= x)
    groups.append(j > x)
    mall = np.concatenate(groups, axis=0).astype(np.float32)
    return jnp.asarray(mall, BF16), jnp.asarray(np.tile(lv, (GLA_HEADS, 1)))


def _gla_kernel(qk_ref, v_ref, g_ref, t_ref, w2_ref, b2_ref, ng_ref, s0_ref, mall_ref, lv_ref, o_ref, st_ref, state,
                *, valid_len, n_chunks):
    ci = pl.program_id(1)

    @pl.when(ci == 0)
    def _():
        state[...] = s0_ref[...].T

    qk = qk_ref[...]
    q = qk[:, 0:GLA_KW] * (GLA_DK ** -0.5)
    k = qk[:, GLA_KW:2 * GLA_KW]
    v = v_ref[...]
    gate = g_ref[...]
    la = _log_sigmoid(_dot(t_ref[...].astype(BF16), w2_ref[...].astype(BF16)) + b2_ref[...]) * (1.0 / GLA_TAU)
    if valid_len < CHUNK:
        live = lax.broadcasted_iota(jnp.int32, (CHUNK, GLA_KW), 0) < valid_len
        la = jnp.where(live, la, 0.0)
        k = jnp.where(live, k, 0.0)

    la_hi = la.astype(BF16)
    la_lo = (la - la_hi.astype(F32)).astype(BF16)
    mall = mall_ref[...]
    w = jnp.exp(_dot(mall, la_hi) + _dot(mall, la_lo))

    lane_head = lax.broadcasted_iota(jnp.int32, (CHUNK, GLA_KW), 1) // GLA_DK
    heads = [lane_head == h for h in range(GLA_HEADS)]

    q_stack = jnp.concatenate([jnp.where(hm, q, 0.0) for hm in heads], axis=0)
    lv = lv_ref[...]

    def level_w(i):
        return w[i * CHUNK:(i + 1) * CHUNK]

    a = jnp.zeros((GLA_HEADS * CHUNK, CHUNK), F32)
    for i in range(_N_LEVEL):
        wi = level_w(i)
        qi = q_stack * jnp.concatenate([wi] * GLA_HEADS, axis=0)
        a = jnp.where(lv == i, _dot_nt(qi.astype(BF16), (k * wi).astype(BF16)), a)
    a = jnp.where(lv == _N_LEVEL, _dot_nt(q_stack.astype(BF16), k.astype(BF16)), a)

    w_pre = level_w(_N_LEVEL)
    w_suf = level_w(_N_LEVEL + 1)
    st = state[...]
    q_pre = q_stack * jnp.concatenate([w_pre] * GLA_HEADS, axis=0)
    o_inter = _dot_nt(q_pre.astype(BF16), st.astype(BF16))
    v_b = v.astype(BF16)
    for h in range(GLA_HEADS):
        rows = slice(h * CHUNK, (h + 1) * CHUNK)
        cols = slice(h * LANES, (h + 1) * LANES)
        o = _dot(a[rows].astype(BF16), v_b[:, cols]) + o_inter[rows]
        o = o * lax.rsqrt(jnp.mean(o * o, axis=-1, keepdims=True) + LN_EPS)
        o_ref[:, cols] = (o * ng_ref[:, cols] * _silu(gate[:, cols])).astype(o_ref.dtype)

    kvt = _dot(v.T.astype(BF16), (k * w_suf).astype(BF16))
    upd = jnp.zeros((LANES, GLA_KW), F32)
    for h in range(GLA_HEADS):
        upd = upd + jnp.where(heads[h], kvt[h * LANES:(h + 1) * LANES], 0.0)
    st_new = st * w_pre[CHUNK - 1:CHUNK, :] + upd
    state[...] = st_new

    @pl.when(ci == n_chunks - 1)
    def _():
        st_ref[...] = st_new.T


def _gla(ug, tail, w2, b2, norm_g, s0, tables, *, valid_len=CHUNK):
    bsz, s, _ = ug.shape
    n_chunks = s // CHUNK
    mall, lv = tables
    w2pad = jnp.zeros((LANES, GLA_KW), F32).at[N_FOX_HEADS:N_FOX_HEADS + GLA_RANK].set(w2)
    const2 = lambda b, i: (0, 0)
    col = lambda c: pl.BlockSpec((None, CHUNK, TN), lambda b, i: (b, i, c))
    o, st = pl.pallas_call(
        functools.partial(_gla_kernel, valid_len=valid_len, n_chunks=n_chunks),
        grid=(bsz, n_chunks),
        in_specs=[col(2), col(3), col(4),
                  pl.BlockSpec((None, CHUNK, LANES), lambda b, i: (b, i, 0)),
                  pl.BlockSpec((LANES, GLA_KW), const2),
                  pl.BlockSpec((1, GLA_KW), const2),
                  pl.BlockSpec((1, GLA_WIDTH), const2),
                  pl.BlockSpec((None, GLA_KW, LANES), lambda b, i: (b, 0, 0)),
                  pl.BlockSpec(mall.shape, const2),
                  pl.BlockSpec(lv.shape, const2)],
        out_specs=[pl.BlockSpec((None, CHUNK, GLA_WIDTH), lambda b, i: (b, i, 0)),
                   pl.BlockSpec((None, GLA_KW, LANES), lambda b, i: (b, 0, 0))],
        out_shape=[jax.ShapeDtypeStruct((bsz, s, GLA_WIDTH), BF16),
                   jax.ShapeDtypeStruct((bsz, GLA_KW, LANES), F32)],
        scratch_shapes=[pltpu.VMEM((LANES, GLA_KW), F32)],
        compiler_params=_cparams("parallel", "arbitrary"),
        name="gla",
    )(ug, ug, ug, tail, w2pad, b2.reshape(1, GLA_KW), norm_g.reshape(1, GLA_WIDTH),
      s0.reshape(bsz, GLA_KW, LANES), mall, lv)
    return o, st.reshape(bsz, GLA_HEADS, GLA_DK, LANES)


PAGES_PER_STEP = 16
PAGE_ROWS = PAGE * N_FOX_HEADS


def _dec_tables():
    t = np.arange(PAGE)
    after = (t[:, None] > t[None, :]).astype(np.float32)
    spread = (t[:, None] == (np.arange(PAGE_ROWS) // N_FOX_HEADS)[None, :]).astype(np.float32)
    return jnp.asarray(after, BF16), jnp.asarray(spread, BF16)


def _dec_kernel(pt_ref, *refs, n_steps):
    n = PAGES_PER_STEP
    f_refs, k_refs, v_refs = refs[:n], refs[n:2 * n], refs[2 * n:3 * n]
    q_ref, kn_ref, vn_ref, fn_ref, after_ref, spread_ref, o_ref, m_sc, l_sc, acc_sc, carry = refs[3 * n:]
    i = pl.program_id(1)
    scale = HEAD_DIM ** -0.5
    q = q_ref[...]

    @pl.when(i == 0)
    def _():
        s_self = jnp.sum(q.astype(BF16).astype(F32) * kn_ref[...].astype(BF16).astype(F32), axis=-1, keepdims=True) * scale
        m_sc[...] = jnp.broadcast_to(s_self, m_sc.shape)
        l_sc[...] = jnp.ones(l_sc.shape, F32)
        acc_sc[...] = vn_ref[...].astype(BF16).astype(F32)
        carry[...] = fn_ref[...]

    lhs = jnp.concatenate([f_ref[...] for f_ref in f_refs], axis=0)
    suffix = _dot_exact(lhs, after_ref[...])
    total = _dot_exact(lhs, jnp.ones((PAGE, LANES), BF16))
    run = carry[...]
    bias = [None] * n
    for j in reversed(range(n)):
        sl = slice(j * N_FOX_HEADS, (j + 1) * N_FOX_HEADS)
        bias[j] = suffix[sl] + run
        run = run + total[sl]
    carry[...] = run
    bias_x = _dot_exact(jnp.concatenate(bias, axis=0), spread_ref[...])

    q_b = q.astype(BF16)
    valid = lax.broadcasted_iota(jnp.int32, (N_FOX_HEADS, PAGE_ROWS), 1) % N_FOX_HEADS == \
        lax.broadcasted_iota(jnp.int32, (N_FOX_HEADS, PAGE_ROWS), 0)
    scores = []
    for j, k_ref in enumerate(k_refs):
        s = _dot_nt(q_b, k_ref[...].astype(BF16)) * scale + bias_x[j * N_FOX_HEADS:(j + 1) * N_FOX_HEADS]
        scores.append(jnp.where(valid, s, -jnp.inf))
    m_prev = m_sc[...]
    m_cur = scores[0]
    for s in scores[1:]:
        m_cur = jnp.maximum(m_cur, s)
    m_new = jnp.maximum(m_prev, jnp.max(m_cur, axis=1, keepdims=True))
    alpha = jnp.exp(m_prev - m_new)
    m_wide = jnp.concatenate([m_new] * (PAGE_ROWS // LANES), axis=1)
    acc = alpha * acc_sc[...]
    l_part = jnp.zeros((N_FOX_HEADS, PAGE_ROWS), F32)
    for s, v_ref in zip(scores, v_refs):
        p = jnp.exp(s - m_wide)
        l_part = l_part + p
        acc = acc + _dot(p.astype(BF16), v_ref[...].astype(BF16))
    l_new = alpha * l_sc[...] + jnp.sum(l_part, axis=1, keepdims=True)
    m_sc[...] = m_new
    l_sc[...] = l_new
    acc_sc[...] = acc

    @pl.when(i == n_steps - 1)
    def _():
        o_ref[...] = acc / l_new


def _fox_decode(page_table, k_pool, v_pool, f_pool, layer, q, k_new, v_new, logf_new_b, tables):
    bd, n_pages = page_table.shape
    n_steps = n_pages // PAGES_PER_STEP
    depth, n_pool = k_pool.shape[:2]
    k_view = k_pool.reshape(depth, n_pool, PAGE_ROWS, HEAD_DIM)
    v_view = v_pool.reshape(depth, n_pool, PAGE_ROWS, HEAD_DIM)
    f_view = jnp.transpose(f_pool, (0, 1, 3, 2))
    after, spread = tables

    def page_of(b, i, pt, j):
        return pt[b, (n_steps - 1 - i) * PAGES_PER_STEP + j]

    def page_spec(j, rows):
        return pl.BlockSpec((None, None, rows, LANES), lambda b, i, pt: (layer, page_of(b, i, pt, j), 0, 0))

    vec = pl.BlockSpec((None, N_FOX_HEADS, HEAD_DIM), lambda b, i, pt: (b, 0, 0))
    const = lambda a: pl.BlockSpec(a.shape, lambda b, i, pt: (0, 0))
    pages = range(PAGES_PER_STEP)
    grid_spec = pltpu.PrefetchScalarGridSpec(
        num_scalar_prefetch=1,
        grid=(bd, n_steps),
        in_specs=[page_spec(j, N_FOX_HEADS) for j in pages] + [page_spec(j, PAGE_ROWS) for j in pages] * 2
        + [vec, vec, vec, vec, const(after), const(spread)],
        out_specs=vec,
        scratch_shapes=[pltpu.VMEM((N_FOX_HEADS, LANES), F32)] * 4,
    )
    return pl.pallas_call(
        functools.partial(_dec_kernel, n_steps=n_steps),
        grid_spec=grid_spec,
        out_shape=jax.ShapeDtypeStruct((bd, N_FOX_HEADS, HEAD_DIM), F32),
        compiler_params=_cparams("parallel", "arbitrary"),
        name="fox_decode",
    )(page_table, *([f_view] * PAGES_PER_STEP), *([k_view] * PAGES_PER_STEP), *([v_view] * PAGES_PER_STEP),
      q, k_new, v_new, logf_new_b, after, spread)


def _pad_rows(x, rows):
    return jnp.pad(x[:, None, :], ((0, 0), (0, rows - 1), (0, 0)))


def _rows16(x):
    return jnp.pad(x, ((0, SROWS - x.shape[0]), (0, 0)))


def kernel(x_prompt, x_sample, cache_k, cache_v, cache_logf, state_gla, page_table, c_prompt, c_sample,
           ln_in_g, ln_in_b, w_ada, b_ada, w_in, b_f, gm_ln_g, gm_ln_b, gm_ws, gm_bs, gla_w2, gla_b2,
           gla_norm_g, w_o, ln1_g, ln1_b, w_gu, w_down, ln2_g, ln2_b):
    bp, seq, d = x_prompt.shape
    bd = x_sample.shape[0]
    depth = w_ada.shape[0]
    assert d == 4 * GLA_WIDTH and x_sample.shape[1] == 1 and bp + bd <= 16 and bd <= SROWS
    assert w_in.shape[2] == IN_WIDTH
    alpha = (2 * depth) ** 0.25
    m_p = bp * seq

    gla_tables = _gla_tables()
    dec_tables = _dec_tables()
    ext_tables = _ext_tables()

    c16 = jnp.zeros((SROWS, d), F32).at[:bd].set(c_sample).at[bd:bd + bp].set(c_prompt)
    mods = _ada(c16, w_ada, b_ada)

    def mod_p(l, which):
        return mods[l, bd:bd + bp, which * d:(which + 1) * d][:, None, :]

    def mod_s(l, *which):
        return jnp.concatenate([mods[l, :, w * d:(w + 1) * d] for w in which], axis=-1)

    w_in_b = _pack_in(w_in)
    w_o_b = _cast(w_o)
    w_down_b = _cast(w_down)

    xp, hp = _ln_mod(x_prompt, ln_in_g, ln_in_b, mod=(mod_p(0, 1), mod_p(0, 0)), tr=256)
    xs, hs = _ln_mod(x_sample, ln_in_g, ln_in_b,
                     mod=(mods[0, :bd, d:2 * d][:, None, :], mods[0, :bd, 0:d][:, None, :]), tr=1)
    xp, hp = xp.reshape(m_p, d), hp.reshape(m_p, d)
    xs, hs = _rows16(xs[:, 0, :]), _rows16(hs[:, 0, :])

    zero_state = jnp.zeros((bp, GLA_HEADS, GLA_DK, LANES), F32)
    k_all = v_all = None
    fp, sp = [], []
    kd, vd, fd, sd, gd = [], [], [], [], []
    for l in range(depth):
        last = l == depth - 1
        q16, k_all, k16, v_all, vt16, ug, tail, samp = _proj(hp, hs, w_in_b, l, k_all, v_all, depth, tm=256)

        samp = samp[:bd]
        heads3 = lambda a: a.reshape(bd, N_FOX_HEADS, HEAD_DIM)
        q_s = heads3(samp[:, 0:FOX_WIDTH])
        k_s = heads3(samp[:, FOX_WIDTH:2 * FOX_WIDTH])
        v_s = heads3(samp[:, 2 * FOX_WIDTH:3 * FOX_WIDTH])
        ug_s = _pad_rows(samp[:, 6 * TN:11 * TN], CHUNK)
        tail_s = _pad_rows(samp[:, 11 * TN:11 * TN + LANES], CHUNK)
        logf_s, _, _ = _fox_cum(tail_s, b_f[l], ext_tables)
        logf_s = logf_s[:, 0, :]
        fox_s = _fox_decode(page_table, cache_k, cache_v, cache_logf, l, q_s, k_s, v_s,
                            jnp.broadcast_to(logf_s[:, :, None], (bd, N_FOX_HEADS, LANES)), dec_tables)
        gm_s, vn_s = _gmlp(ug_s, gm_ln_g[l], gm_ln_b[l], gm_ws[l], gm_bs[l], emit_vn=True)
        gla_s, st_s = _gla(ug_s, tail_s, gla_w2[l], gla_b2[l], gla_norm_g[l], state_gla[l], gla_tables, valid_len=1)
        mix_s = jnp.concatenate([fox_s.reshape(bd, -1).astype(BF16), gm_s[:, 0, :], gla_s[:, 0, :]], axis=-1)

        tail3 = tail.reshape(bp, seq, LANES)
        ug3 = ug.reshape(bp, seq, UG_WIDTH)
        logf, ext_q, ext_k = _fox_cum(tail3, b_f[l], ext_tables)
        fox_o = _fox_prompt(q16.reshape(bp, seq, FOX_WIDTH), k16.reshape(bp, seq, FOX_WIDTH), vt16, ext_q, ext_k)
        gm_o, _ = _gmlp(ug3, gm_ln_g[l], gm_ln_b[l], gm_ws[l], gm_bs[l], emit_vn=False)
        gla_o, st = _gla(ug3, tail3, gla_w2[l], gla_b2[l], gla_norm_g[l], zero_state, gla_tables)

        x1, h1, x1_s, h1_s = _dense_ln(
            [fox_o.reshape(m_p, -1), gm_o.reshape(m_p, -1), gla_o.reshape(m_p, -1)], _rows16(mix_s), w_o_b, l,
            xp, xs, mod_p(l, 2), ln1_g[l], ln1_b[l], (mod_p(l, 4), mod_p(l, 3)), mod_s(l, 2, 4, 3),
            alpha=alpha, rows_per_batch=seq, tm=512)
        act, act_s = _swiglu(h1, h1_s, w_gu, l, tm=2048, tn=512)
        nxt = None if last else (mod_p(l + 1, 1), mod_p(l + 1, 0))
        nxt_s = mod_s(l, 5, 5, 5) if last else jnp.concatenate([mod_s(l, 5), mod_s(l + 1, 1, 0)], axis=-1)
        xp, hp, xs, hs = _dense_ln([act], act_s, w_down_b, l, x1, x1_s, mod_p(l, 5), ln2_g[l], ln2_b[l], nxt, nxt_s,
                                   alpha=alpha, rows_per_batch=seq, tm=256)

        fp.append(logf)
        sp.append(st)
        kd.append(k_s[:, None])
        vd.append(v_s[:, None])
        fd.append(logf_s[:, None, :])
        sd.append(st_s)
        gd.append(vn_s[:, 0:1, :])

    kv_shape = (depth, bp, seq, N_FOX_HEADS, HEAD_DIM)
    return (xp.reshape(bp, seq, d), xs[:bd, None, :], k_all.reshape(kv_shape), v_all.reshape(kv_shape),
            jnp.stack(fp), jnp.stack(sp), jnp.stack(kd), jnp.stack(vd), jnp.stack(fd), jnp.stack(sd), jnp.stack(gd))
```

```python
import functools
import math

import numpy as np
import jax
import jax.numpy as jnp
from jax import lax
from jax.experimental import pallas as pl
from jax.experimental.pallas import tpu as pltpu

F32 = jnp.float32
BF16 = jnp.bfloat16

LANES = 128
HEAD_DIM = 128
N_FOX_HEADS = 8
FOX_WIDTH = N_FOX_HEADS * HEAD_DIM
GMLP_GROUPS = 4
GMLP_WIDTH = GMLP_GROUPS * LANES
GLA_HEADS = 4
GLA_DK = 64
GLA_KW = GLA_HEADS * GLA_DK
GLA_WIDTH = GLA_HEADS * LANES
GLA_RANK = 16
GLA_TAU = 16.0
CHUNK = 128
PAGE = 128
LN_EPS = 1e-5
LOG2E = math.log2(math.e)
VMEM_LIMIT = 56 * 1024 * 1024
SROWS = 16

TN = 512
OFF_FF = 3 * FOX_WIDTH
OFF_MU = OFF_FF + N_FOX_HEADS
OFF_GR = OFF_MU + 2 * GMLP_WIDTH + 2 * GLA_KW + GLA_WIDTH
OFF_GG = OFF_GR + GLA_RANK
IN_WIDTH = OFF_GG + GLA_WIDTH
N_PROJ_TILES = 12
UG_WIDTH = 5 * TN


def _cparams(*sem):
    return pltpu.CompilerParams(dimension_semantics=sem, vmem_limit_bytes=VMEM_LIMIT)


def _log_sigmoid(x):
    return jnp.minimum(x, 0.0) - jnp.log1p(jnp.exp(-jnp.abs(x)))


def _silu(x):
    return x * (1.0 / (1.0 + jnp.exp(-x)))


def _split3(x):
    hi = x.astype(BF16)
    r = x - hi.astype(F32)
    mid = r.astype(BF16)
    lo = (r - mid.astype(F32)).astype(BF16)
    return hi, mid, lo


def _dot(a, b):
    return jnp.dot(a, b, preferred_element_type=F32)


def _dot_nt(a, b):
    return lax.dot_general(a, b, (((1,), (1,)), ((), ())), preferred_element_type=F32)


def _dot_exact(x, w):
    hi, mid, lo = _split3(x)
    return _dot(hi, w) + _dot(mid, w) + _dot(lo, w)


def _dot_exact_l(w, x):
    hi, mid, lo = _split3(x)
    return _dot(w, hi) + _dot(w, mid) + _dot(w, lo)


def _ada_kernel(c_ref, w_ref, b_ref, o_ref):
    a = _silu(c_ref[...]).astype(BF16)
    o_ref[...] = _dot(a, w_ref[...].astype(BF16)) + b_ref[...]


def _ada(c16, w_ada, b_ada):
    depth, d, n = w_ada.shape
    tn = 1024
    return pl.pallas_call(
        _ada_kernel,
        grid=(depth, n // tn),
        in_specs=[pl.BlockSpec((16, d), lambda l, j: (0, 0)),
                  pl.BlockSpec((None, d, tn), lambda l, j: (l, 0, j)),
                  pl.BlockSpec((None, 1, tn), lambda l, j: (l, 0, j))],
        out_specs=pl.BlockSpec((None, 16, tn), lambda l, j: (l, 0, j)),
        out_shape=jax.ShapeDtypeStruct((depth, 16, n), F32),
        compiler_params=_cparams("parallel", "parallel"),
        name="ada",
    )(c16, w_ada, b_ada.reshape(depth, 1, n))


def _ln_kernel(*refs, has_res, has_mod, alpha):
    it = iter(refs)
    x = next(it)[...]
    if has_res:
        y = next(it)[...]
        gate = next(it)[...]
        x = alpha * x + gate * y
    g = next(it)[...]
    b = next(it)[...]
    if has_mod:
        sc = next(it)[...]
        sh = next(it)[...]
    xn_ref = next(it)
    mu = jnp.mean(x, axis=-1, keepdims=True)
    xc = x - mu
    var = jnp.mean(xc * xc, axis=-1, keepdims=True)
    xn = xc * lax.rsqrt(var + LN_EPS) * g + b
    xn_ref[...] = xn
    if has_mod:
        h_ref = next(it)
        h_ref[...] = (xn * (1.0 + sc) + sh).astype(BF16)


def _ln_mod(x, g, b, *, res=None, mod=None, alpha=1.0, tr):
    bsz, s, d = x.shape
    tr = min(tr, s)
    row = pl.BlockSpec((None, tr, d), lambda i, j: (i, j, 0))
    per_b = pl.BlockSpec((None, 1, d), lambda i, j: (i, 0, 0))
    vec = pl.BlockSpec((1, d), lambda i, j: (0, 0))
    args, specs = [x], [row]
    if res is not None:
        y, gate = res
        args += [y, gate]
        specs += [row, per_b]
    args += [g.reshape(1, d), b.reshape(1, d)]
    specs += [vec, vec]
    out_shape = [jax.ShapeDtypeStruct((bsz, s, d), F32)]
    out_specs = [row]
    if mod is not None:
        sc, sh = mod
        args += [sc, sh]
        specs += [per_b, per_b]
        out_shape.append(jax.ShapeDtypeStruct((bsz, s, d), BF16))
        out_specs.append(row)
    out = pl.pallas_call(
        functools.partial(_ln_kernel, has_res=res is not None, has_mod=mod is not None, alpha=alpha),
        grid=(bsz, s // tr),
        in_specs=specs, out_specs=out_specs, out_shape=out_shape,
        compiler_params=_cparams("parallel", "parallel"),
        name="ln_mod",
    )(*args)
    return out if mod is not None else (out[0], None)


PACK_EXTRA = 32


def _pack_in_kernel(wm_ref, wn_ref, wg_ref, o_ref):
    j = pl.program_id(1)
    k_dim = wm_ref.shape[1]

    @pl.when(j < 6)
    def _():
        o_ref[...] = wm_ref[...].T.astype(BF16)

    def shifted(rows):
        return jnp.concatenate([wm_ref[rows:TN, :], wn_ref[0:rows, :]], axis=0).T.astype(BF16)

    @pl.when((j >= 6) & (j < 10))
    def _():
        o_ref[...] = shifted(OFF_MU - OFF_FF)

    @pl.when(j == 10)
    def _():
        o_ref[...] = shifted(OFF_GG - 10 * TN)

    @pl.when(j == N_PROJ_TILES - 1)
    def _():
        lo = OFF_GR - (OFF_GR // PACK_EXTRA) * PACK_EXTRA
        t = jnp.concatenate([wn_ref[0:N_FOX_HEADS, :], wg_ref[lo:lo + GLA_RANK, :],
                             jnp.zeros((TN - N_FOX_HEADS - GLA_RANK, k_dim), F32)], axis=0)
        o_ref[...] = t.T.astype(BF16)


def _pack_in(w_in):
    depth, k, _ = w_in.shape
    w_t = jnp.transpose(w_in, (0, 2, 1))
    per = TN // PACK_EXTRA
    nb = lambda j: jnp.where((j >= 6) & (j < N_PROJ_TILES - 1), per * (j + 1), OFF_FF // PACK_EXTRA)
    return pl.pallas_call(
        _pack_in_kernel,
        grid=(depth, N_PROJ_TILES),
        in_specs=[pl.BlockSpec((None, TN, k), lambda l, j: (l, jnp.minimum(j, 10), 0)),
                  pl.BlockSpec((None, PACK_EXTRA, k), lambda l, j: (l, nb(j), 0)),
                  pl.BlockSpec((None, PACK_EXTRA, k), lambda l, j: (l, OFF_GR // PACK_EXTRA, 0))],
        out_specs=pl.BlockSpec((None, k, TN), lambda l, j: (l, 0, j)),
        out_shape=jax.ShapeDtypeStruct((depth, k, N_PROJ_TILES * TN), BF16),
        compiler_params=_cparams("parallel", "parallel"),
        name="pack_in",
    )(w_t, w_t, w_t)


def _cast_kernel(w_ref, o_ref):
    o_ref[...] = w_ref[...].astype(BF16)


def _cast(w):
    depth, k, n = w.shape
    blk = pl.BlockSpec((None, k, TN), lambda l, j: (l, 0, j))
    return pl.pallas_call(
        _cast_kernel, grid=(depth, n // TN), in_specs=[blk], out_specs=blk,
        out_shape=jax.ShapeDtypeStruct(w.shape, BF16),
        compiler_params=_cparams("parallel", "parallel"), name="cast",
    )(w)


def _proj_kernel(*refs, aliased, q_scale, tm):
    x_ref, xs_ref, w_ref = refs[:3]
    q_ref, k32_ref, k16_ref, v32_ref, vt_ref, ug_ref, t_ref, s_ref = refs[3 + (2 if aliased else 0):]
    x = x_ref[...]
    q_ref[...] = (_dot(x, w_ref[:, 0:FOX_WIDTH]) * q_scale).astype(BF16)
    k = _dot(x, w_ref[:, FOX_WIDTH:2 * FOX_WIDTH])
    k16_ref[...] = k.astype(BF16)
    v = _dot(x, w_ref[:, 2 * FOX_WIDTH:3 * FOX_WIDTH])
    vt_ref[...] = v.T.astype(BF16)
    for h in range(N_FOX_HEADS):
        cols = slice(h * HEAD_DIM, (h + 1) * HEAD_DIM)
        k32_ref[pl.ds(h, tm, stride=N_FOX_HEADS), :] = k[:, cols]
        v32_ref[pl.ds(h, tm, stride=N_FOX_HEADS), :] = v[:, cols]
    ug_ref[...] = _dot(x, w_ref[:, 6 * TN:11 * TN])
    t_ref[...] = _dot(x, w_ref[:, 11 * TN:11 * TN + LANES])

    @pl.when(pl.program_id(0) == 0)
    def _():
        s_ref[...] = _dot(xs_ref[...], w_ref[...])


def _proj(x, xs, w_in, layer, k_prev, v_prev, depth, *, tm):
    m, k = x.shape
    tm = min(tm, m)
    n_all = w_in.shape[2]
    in_specs = [pl.BlockSpec((tm, k), lambda i: (i, 0)),
                pl.BlockSpec((SROWS, k), lambda i: (0, 0)),
                pl.BlockSpec((None, k, n_all), lambda i: (layer, 0, 0), pipeline_mode=pl.Buffered(1))]
    args = [x, xs, w_in]
    aliases = {}
    if k_prev is not None:
        in_specs += [pl.BlockSpec(memory_space=pl.ANY)] * 2
        args += [k_prev, v_prev]
        aliases = {3: 1, 4: 3}
    rows = pl.BlockSpec((tm, FOX_WIDTH), lambda i: (i, 0))
    pairs = pl.BlockSpec((None, tm * N_FOX_HEADS, HEAD_DIM), lambda i: (layer, i, 0))
    out_shape = [jax.ShapeDtypeStruct((m, FOX_WIDTH), BF16),
                 jax.ShapeDtypeStruct((depth, m * N_FOX_HEADS, HEAD_DIM), F32),
                 jax.ShapeDtypeStruct((m, FOX_WIDTH), BF16),
                 jax.ShapeDtypeStruct((depth, m * N_FOX_HEADS, HEAD_DIM), F32),
                 jax.ShapeDtypeStruct((FOX_WIDTH, m), BF16),
                 jax.ShapeDtypeStruct((m, UG_WIDTH), F32),
                 jax.ShapeDtypeStruct((m, LANES), F32),
                 jax.ShapeDtypeStruct((SROWS, n_all), F32)]
    out_specs = [rows, pairs, rows, pairs,
                 pl.BlockSpec((FOX_WIDTH, tm), lambda i: (0, i)),
                 pl.BlockSpec((tm, UG_WIDTH), lambda i: (i, 0)),
                 pl.BlockSpec((tm, LANES), lambda i: (i, 0)),
                 pl.BlockSpec((SROWS, n_all), lambda i: (0, 0))]
    return pl.pallas_call(
        functools.partial(_proj_kernel, aliased=k_prev is not None, q_scale=HEAD_DIM ** -0.5 * LOG2E, tm=tm),
        grid=(m // tm,),
        in_specs=in_specs, out_specs=out_specs, out_shape=out_shape,
        input_output_aliases=aliases,
        compiler_params=_cparams("arbitrary"),
        name="proj",
    )(*args)


SWIGLU_SUB = 1024


def _swiglu_kernel(x_ref, xs_ref, wg_ref, wu_ref, o_ref, os_ref, wbg, wbu):
    def result(x):
        return (_silu(_dot(x, wbg[...])) * _dot(x, wbu[...])).astype(BF16)

    @pl.when(pl.program_id(1) == 0)
    def _():
        wbg[...] = wg_ref[...].astype(BF16)
        wbu[...] = wu_ref[...].astype(BF16)
        os_ref[...] = result(xs_ref[...])

    sub = min(SWIGLU_SUB, x_ref.shape[0])
    for r0 in range(0, x_ref.shape[0], sub):
        o_ref[r0:r0 + sub, :] = result(x_ref[r0:r0 + sub, :])


def _swiglu(x, x_sample, w_gu, layer, *, tm, tn):
    m, k = x.shape
    tm = min(tm, m)
    n = w_gu.shape[2] // 2
    up0 = n // tn
    return pl.pallas_call(
        _swiglu_kernel,
        grid=(n // tn, m // tm),
        in_specs=[pl.BlockSpec((tm, k), lambda j, i: (i, 0)),
                  pl.BlockSpec((SROWS, k), lambda j, i: (0, 0)),
                  pl.BlockSpec((None, k, tn), lambda j, i: (layer, 0, j)),
                  pl.BlockSpec((None, k, tn), lambda j, i: (layer, 0, up0 + j))],
        out_specs=[pl.BlockSpec((tm, tn), lambda j, i: (i, j)), pl.BlockSpec((SROWS, tn), lambda j, i: (0, j))],
        out_shape=[jax.ShapeDtypeStruct((m, n), BF16), jax.ShapeDtypeStruct((SROWS, n), BF16)],
        scratch_shapes=[pltpu.VMEM((k, tn), BF16)] * 2,
        compiler_params=_cparams("arbitrary", "arbitrary"),
        name="swiglu",
    )(x, x_sample, w_gu, w_gu)


DENSE_LN_SUB = 128


def _layer_norm(z, g, b):
    mu = jnp.mean(z, axis=-1, keepdims=True)
    zc = z - mu
    var = jnp.mean(zc * zc, axis=-1, keepdims=True)
    return zc * lax.rsqrt(var + LN_EPS) * g + b


def _dense_ln_kernel(*refs, n_lhs, has_mod, alpha):
    it = iter(refs)
    x_refs = [next(it) for _ in range(n_lhs)]
    xs_ref, w_ref, res_ref, gate_ref, g_ref, b_ref = [next(it) for _ in range(6)]
    sc_ref, sh_ref = (next(it), next(it)) if has_mod else (None, None)
    res_s_ref, mod_s_ref = next(it), next(it)
    xn_ref = next(it)
    h_ref = next(it) if has_mod else None
    xn_s_ref = next(it)
    h_s_ref = next(it) if has_mod else None
    d = xn_ref.shape[-1]

    def finish(lhs, res, gate, sc, sh):
        xn = _layer_norm(alpha * res + gate * _dot(lhs, w_ref[...]), g_ref[...], b_ref[...])
        return xn, ((xn * (1.0 + sc) + sh).astype(BF16) if has_mod else None)

    @pl.when(pl.program_id(0) == 0)
    def _():
        mod_s = mod_s_ref[...]
        xn, h = finish(xs_ref[...], res_s_ref[...], mod_s[:, 0:d], mod_s[:, d:2 * d], mod_s[:, 2 * d:3 * d])
        xn_s_ref[...] = xn
        if has_mod:
            h_s_ref[...] = h

    tm = xn_ref.shape[0]
    sub = min(tm, DENSE_LN_SUB)
    for r0 in range(0, tm, sub):
        rows = slice(r0, r0 + sub)
        parts = [x[rows, :] for x in x_refs]
        lhs = parts[0] if n_lhs == 1 else jnp.concatenate(parts, axis=1)
        xn, h = finish(lhs, res_ref[rows, :], gate_ref[...], sc_ref[...] if has_mod else None,
                       sh_ref[...] if has_mod else None)
        xn_ref[rows, :] = xn
        if has_mod:
            h_ref[rows, :] = h


def _dense_ln(xs_list, x_sample, w, layer, res, res_s, gate, ln_g, ln_b, mod, mod_s, *, alpha, rows_per_batch, tm):
    m = xs_list[0].shape[0]
    tm = min(tm, rows_per_batch)
    k, d = w.shape[1], w.shape[2]
    assert sum(x.shape[1] for x in xs_list) == k and x_sample.shape == (SROWS, k)
    steps_per_batch = rows_per_batch // tm
    has_mod = mod is not None
    row = pl.BlockSpec((tm, d), lambda i: (i, 0))
    per_b = pl.BlockSpec((None, 1, d), lambda i: (i // steps_per_batch, 0, 0))
    vec = pl.BlockSpec((1, d), lambda i: (0, 0))
    srow = pl.BlockSpec((SROWS, d), lambda i: (0, 0))
    in_specs = [pl.BlockSpec((tm, x.shape[1]), lambda i: (i, 0)) for x in xs_list]
    in_specs += [pl.BlockSpec((SROWS, k), lambda i: (0, 0)),
                 pl.BlockSpec((None, k, d), lambda i: (layer, 0, 0), pipeline_mode=pl.Buffered(1)),
                 row, per_b, vec, vec]
    args = list(xs_list) + [x_sample, w, res, gate, ln_g.reshape(1, d), ln_b.reshape(1, d)]
    if has_mod:
        in_specs += [per_b, per_b]
        args += list(mod)
    in_specs += [srow, pl.BlockSpec((SROWS, 3 * d), lambda i: (0, 0))]
    args += [res_s, mod_s]
    out_specs, out_shape = [row], [jax.ShapeDtypeStruct((m, d), F32)]
    if has_mod:
        out_specs.append(row)
        out_shape.append(jax.ShapeDtypeStruct((m, d), BF16))
    out_specs.append(srow)
    out_shape.append(jax.ShapeDtypeStruct((SROWS, d), F32))
    if has_mod:
        out_specs.append(srow)
        out_shape.append(jax.ShapeDtypeStruct((SROWS, d), BF16))
    out = pl.pallas_call(
        functools.partial(_dense_ln_kernel, n_lhs=len(xs_list), has_mod=has_mod, alpha=alpha),
        grid=(m // tm,),
        in_specs=in_specs, out_specs=out_specs, out_shape=out_shape,
        compiler_params=_cparams("arbitrary"),
        name="dense_ln",
    )(*args)
    return out if has_mod else (out[0], None, out[1], None)


EXT_PER_HEAD = 6


def _ext_tables():
    pq = np.zeros((3 * LANES, LANES), np.float32)
    pk = np.zeros((3 * LANES, LANES), np.float32)
    oq = np.zeros((1, LANES), np.float32)
    ok = np.zeros((1, LANES), np.float32)
    for h in range(N_FOX_HEADS):
        for part in range(3):
            pq[part * LANES + h, EXT_PER_HEAD * h + part] = 1.0
            pk[part * LANES + h, EXT_PER_HEAD * h + 3 + part] = -1.0
            oq[0, EXT_PER_HEAD * h + 3 + part] = 1.0
            ok[0, EXT_PER_HEAD * h + part] = 1.0
    return jnp.asarray(pq, BF16), jnp.asarray(pk, BF16), jnp.asarray(oq), jnp.asarray(ok)


def _fox_cum_kernel(t_ref, bf_ref, pq_ref, pk_ref, oq_ref, ok_ref, lf_ref, eq_ref, ek_ref, carry, *, tb):
    @pl.when(pl.program_id(1) == 0)
    def _():
        carry[...] = jnp.zeros(carry.shape, F32)

    lf = _log_sigmoid(t_ref[...] + bf_ref[...])
    lf_ref[...] = lf[:, 0:N_FOX_HEADS]
    r = lax.broadcasted_iota(jnp.int32, (tb, tb), 0)
    c = lax.broadcasted_iota(jnp.int32, (tb, tb), 1)
    lower = (c <= r).astype(BF16)
    cum = _dot_exact_l(lower, lf) + carry[0:1, :]
    carry[...] = jnp.broadcast_to(cum[tb - 1:tb, :], carry.shape)
    hi, mid, lo = _split3(cum * LOG2E)
    cat = jnp.concatenate([hi, mid, lo], axis=1)
    eq_ref[...] = (_dot(cat, pq_ref[...]) + oq_ref[...]).astype(BF16)
    ek_ref[...] = (_dot(cat, pk_ref[...]) + ok_ref[...]).astype(BF16)


def _fox_cum(tail, b_f, tables):
    bsz, s, _ = tail.shape
    tb = min(512, s)
    pq, pk, oq, ok = tables
    bf = jnp.zeros((1, LANES), F32).at[0, :N_FOX_HEADS].set(b_f)
    blk = pl.BlockSpec((None, tb, LANES), lambda b, i: (b, i, 0))
    const = lambda shape: pl.BlockSpec(shape, lambda b, i: (0, 0))
    return pl.pallas_call(
        functools.partial(_fox_cum_kernel, tb=tb),
        grid=(bsz, s // tb),
        in_specs=[blk, const((1, LANES)), const(pq.shape), const(pk.shape), const((1, LANES)), const((1, LANES))],
        out_specs=[pl.BlockSpec((None, tb, N_FOX_HEADS), lambda b, i: (b, i, 0)), blk, blk],
        out_shape=[jax.ShapeDtypeStruct((bsz, s, N_FOX_HEADS), F32),
                   jax.ShapeDtypeStruct((bsz, s, LANES), BF16),
                   jax.ShapeDtypeStruct((bsz, s, LANES), BF16)],
        scratch_shapes=[pltpu.VMEM((8, LANES), F32)],
        compiler_params=_cparams("parallel", "arbitrary"),
        name="fox_cum",
    )(tail, bf, pq, pk, oq, ok)


HEADS_PER_STEP = 8
ATTN_TILE = 1024


def _fox_kernel(qi_ref, kj_ref, q_ref, k_ref, vt_ref, eq_ref, ek_ref, o_ref, m_sc, l_sc, acc_sc, qx_sc, *, tq, tk):
    grp = pl.program_id(1)
    p = pl.program_id(2)
    qi = qi_ref[p]
    kj = kj_ref[p]

    @pl.when(kj == 0)
    def _():
        m_sc[...] = jnp.full(m_sc.shape, -jnp.inf, F32)
        l_sc[...] = jnp.zeros(l_sc.shape, F32)
        acc_sc[...] = jnp.zeros(acc_sc.shape, F32)
        lane = lax.broadcasted_iota(jnp.int32, (tq, LANES), 1)
        eq = eq_ref[...]
        for hh in range(HEADS_PER_STEP):
            lo = (grp * HEADS_PER_STEP + hh) * EXT_PER_HEAD
            own = (lane >= lo) & (lane < lo + EXT_PER_HEAD)
            qx_sc[hh] = jnp.concatenate([q_ref[:, hh * LANES:(hh + 1) * LANES],
                                         jnp.where(own, eq, jnp.zeros_like(eq))], axis=1)

    def step(diagonal):
        ek = ek_ref[...]
        for hh in range(HEADS_PER_STEP):
            cols = slice(hh * LANES, (hh + 1) * LANES)
            kx = jnp.concatenate([k_ref[:, cols], ek], axis=1)
            s = _dot_nt(kx, qx_sc[hh])
            if diagonal:
                key = lax.broadcasted_iota(jnp.int32, (tk, tq), 0)
                qry = lax.broadcasted_iota(jnp.int32, (tk, tq), 1)
                s = jnp.where(key <= qry, s, -jnp.inf)
            m_prev = m_sc[hh][0:1, :]
            m_new = jnp.maximum(m_prev, jnp.max(s, axis=0, keepdims=True))
            alpha = jnp.exp2(m_prev - m_new)
            pr = jnp.exp2(s - m_new)
            l_new = alpha * l_sc[hh][0:1, :] + jnp.sum(pr, axis=0, keepdims=True)
            acc = alpha * acc_sc[hh] + _dot(vt_ref[cols, :], pr.astype(BF16))
            m_sc[hh] = jnp.broadcast_to(m_new, (8, tq))
            l_sc[hh] = jnp.broadcast_to(l_new, (8, tq))
            acc_sc[hh] = acc
            if diagonal:
                o_ref[:, cols] = (acc / l_new).T.astype(o_ref.dtype)

    @pl.when(kj == qi)
    def _():
        step(True)

    @pl.when(kj != qi)
    def _():
        step(False)


def _fox_prompt(q, k, vt, ext_q, ext_k):
    bsz, s, _ = q.shape
    tq = tk = min(ATTN_TILE, s)
    nq = s // tq
    width = HEADS_PER_STEP * LANES
    pairs = [(i, j) for i in range(nq) for j in range(i + 1)]
    qi_tbl = jnp.asarray([pr[0] for pr in pairs], jnp.int32)
    kj_tbl = jnp.asarray([pr[1] for pr in pairs], jnp.int32)
    grid_spec = pltpu.PrefetchScalarGridSpec(
        num_scalar_prefetch=2,
        grid=(bsz, N_FOX_HEADS // HEADS_PER_STEP, len(pairs)),
        in_specs=[pl.BlockSpec((None, tq, width), lambda b, g, p, qi, kj: (b, qi[p], g)),
                  pl.BlockSpec((None, tk, width), lambda b, g, p, qi, kj: (b, kj[p], g)),
                  pl.BlockSpec((width, tk), lambda b, g, p, qi, kj: (g, b * nq + kj[p])),
                  pl.BlockSpec((None, tq, LANES), lambda b, g, p, qi, kj: (b, qi[p], 0)),
                  pl.BlockSpec((None, tk, LANES), lambda b, g, p, qi, kj: (b, kj[p], 0))],
        out_specs=pl.BlockSpec((None, tq, width), lambda b, g, p, qi, kj: (b, qi[p], g)),
        scratch_shapes=[pltpu.VMEM((HEADS_PER_STEP, 8, tq), F32),
                        pltpu.VMEM((HEADS_PER_STEP, 8, tq), F32),
                        pltpu.VMEM((HEADS_PER_STEP, LANES, tq), F32),
                        pltpu.VMEM((HEADS_PER_STEP, tq, 2 * LANES), BF16)],
    )
    return pl.pallas_call(
        functools.partial(_fox_kernel, tq=tq, tk=tk),
        grid_spec=grid_spec,
        out_shape=jax.ShapeDtypeStruct((bsz, s, FOX_WIDTH), BF16),
        compiler_params=_cparams("parallel", "parallel", "arbitrary"),
        name="fox_prompt",
    )(qi_tbl, kj_tbl, q, k, vt, ext_q, ext_k)


GMLP_CHUNKS_PER_STEP = 4


def _gmlp_kernel(uv_ref, g_ref, b_ref, ws_ref, bs_ref, o_ref, *vn_ref):
    r = lax.broadcasted_iota(jnp.int32, (CHUNK, CHUNK), 0)
    c = lax.broadcasted_iota(jnp.int32, (CHUNK, CHUNK), 1)
    for grp in range(GMLP_GROUPS):
        lo = grp * LANES
        w = jnp.where(c <= r, ws_ref[grp], 0.0).astype(BF16)
        for r0 in range(0, uv_ref.shape[0], CHUNK):
            rows = slice(r0, r0 + CHUNK)
            u = jax.nn.gelu(uv_ref[rows, lo:lo + LANES])
            v = jax.nn.gelu(uv_ref[rows, GMLP_WIDTH + lo:GMLP_WIDTH + lo + LANES])
            mu = jnp.mean(v, axis=-1, keepdims=True)
            vc = v - mu
            var = jnp.mean(vc * vc, axis=-1, keepdims=True)
            vn = vc * lax.rsqrt(var + LN_EPS) * g_ref[:, lo:lo + LANES] + b_ref[:, lo:lo + LANES]
            mixed = _dot(w, vn.astype(BF16)) + bs_ref[grp]
            o_ref[rows, lo:lo + LANES] = (u * mixed).astype(o_ref.dtype)
            if vn_ref:
                vn_ref[0][rows, lo:lo + LANES] = vn


def _gmlp(ug, ln_g, ln_b, ws, bs, *, emit_vn):
    bsz, s, _ = ug.shape
    rows = CHUNK * min(GMLP_CHUNKS_PER_STEP, s // CHUNK)
    blk = pl.BlockSpec((None, rows, GMLP_WIDTH), lambda b, i: (b, i, 0))
    vec = pl.BlockSpec((1, GMLP_WIDTH), lambda b, i: (0, 0))
    cube = pl.BlockSpec((GMLP_GROUPS, CHUNK, CHUNK), lambda b, i: (0, 0, 0))
    out_shape = [jax.ShapeDtypeStruct((bsz, s, GMLP_WIDTH), BF16)]
    if emit_vn:
        out_shape.append(jax.ShapeDtypeStruct((bsz, s, GMLP_WIDTH), F32))
    bs_b = jnp.broadcast_to(bs[:, :, None], (GMLP_GROUPS, CHUNK, LANES))
    out = pl.pallas_call(
        _gmlp_kernel,
        grid=(bsz, s // rows),
        in_specs=[pl.BlockSpec((None, rows, 2 * GMLP_WIDTH), lambda b, i: (b, i, 0)), vec, vec, cube, cube],
        out_specs=[blk] * len(out_shape),
        out_shape=out_shape,
        compiler_params=_cparams("parallel", "parallel"),
        name="gmlp",
    )(ug, ln_g.reshape(1, GMLP_WIDTH), ln_b.reshape(1, GMLP_WIDTH), ws, bs_b)
    return out if emit_vn else (out[0], None)


_GLA_LEVELS = (64, 32, 16, 8, 4, 2, 1)
_N_LEVEL = len(_GLA_LEVELS)


def _gla_tables():
    x = np.arange(CHUNK)[:, None]
    j = np.arange(CHUNK)[None, :]
    groups = []
    lv = np.full((CHUNK, CHUNK), -1, np.int32)
    for i, half in enumerate(_GLA_LEVELS):
        anchor = (x // (2 * half)) * (2 * half) + half - 1
        groups.append((j > np.minimum(x, anchor)) & (j <= np.maximum(x, anchor)))
        same = (x // (2 * half)) == (j // (2 * half))
        upper_t = (x % (2 * half)) >= half
        lower_s = (j % (2 * half)) < half
        lv[same & upper_t & lower_s] = i
    lv[np.arange(CHUNK), np.arange(CHUNK)] = _N_LEVEL
    groups.append(j <= x)
    groups.append(j > x)
    mall = np.concatenate(groups, axis=0).astype(np.float32)
    return jnp.asarray(mall, BF16), jnp.asarray(np.tile(lv, (GLA_HEADS, 1)))


def _gla_kernel(qk_ref, v_ref, g_ref, t_ref, w2_ref, b2_ref, ng_ref, s0_ref, mall_ref, lv_ref, o_ref, st_ref, state,
                *, valid_len, n_chunks):
    ci = pl.program_id(1)

    @pl.when(ci == 0)
    def _():
        state[...] = s0_ref[...].T

    qk = qk_ref[...]
    q = qk[:, 0:GLA_KW] * (GLA_DK ** -0.5)
    k = qk[:, GLA_KW:2 * GLA_KW]
    v = v_ref[...]
    gate = g_ref[...]
    la = _log_sigmoid(_dot(t_ref[...].astype(BF16), w2_ref[...].astype(BF16)) + b2_ref[...]) * (1.0 / GLA_TAU)
    if valid_len < CHUNK:
        live = lax.broadcasted_iota(jnp.int32, (CHUNK, GLA_KW), 0) < valid_len
        la = jnp.where(live, la, 0.0)
        k = jnp.where(live, k, 0.0)

    la_hi = la.astype(BF16)
    la_lo = (la - la_hi.astype(F32)).astype(BF16)
    single = valid_len == 1
    mall = mall_ref[_N_LEVEL * CHUNK:, :] if single else mall_ref[...]
    w = jnp.exp(_dot(mall, la_hi) + _dot(mall, la_lo))

    lane_head = lax.broadcasted_iota(jnp.int32, (CHUNK, GLA_KW), 1) // GLA_DK
    heads = [lane_head == h for h in range(GLA_HEADS)]

    q_stack = jnp.concatenate([jnp.where(hm, q, 0.0) for hm in heads], axis=0)
    n_lv = 0 if single else _N_LEVEL

    def level_w(i):
        return w[i * CHUNK:(i + 1) * CHUNK]

    if single:
        a = _dot_nt(q_stack.astype(BF16), k.astype(BF16))
    else:
        lv = lv_ref[...]
        a = jnp.zeros((GLA_HEADS * CHUNK, CHUNK), F32)
        for i in range(_N_LEVEL):
            wi = level_w(i)
            qi = q_stack * jnp.concatenate([wi] * GLA_HEADS, axis=0)
            a = jnp.where(lv == i, _dot_nt(qi.astype(BF16), (k * wi).astype(BF16)), a)
        a = jnp.where(lv == _N_LEVEL, _dot_nt(q_stack.astype(BF16), k.astype(BF16)), a)

    w_pre = level_w(n_lv)
    w_suf = level_w(n_lv + 1)
    st = state[...]
    q_pre = q_stack * jnp.concatenate([w_pre] * GLA_HEADS, axis=0)
    o_inter = _dot_nt(q_pre.astype(BF16), st.astype(BF16))
    v_b = v.astype(BF16)
    for h in range(GLA_HEADS):
        rows = slice(h * CHUNK, (h + 1) * CHUNK)
        cols = slice(h * LANES, (h + 1) * LANES)
        o = _dot(a[rows].astype(BF16), v_b[:, cols]) + o_inter[rows]
        o = o * lax.rsqrt(jnp.mean(o * o, axis=-1, keepdims=True) + LN_EPS)
        o_ref[:, cols] = (o * ng_ref[:, cols] * _silu(gate[:, cols])).astype(o_ref.dtype)

    kvt = _dot(v.T.astype(BF16), (k * w_suf).astype(BF16))
    upd = jnp.zeros((LANES, GLA_KW), F32)
    for h in range(GLA_HEADS):
        upd = upd + jnp.where(heads[h], kvt[h * LANES:(h + 1) * LANES], 0.0)
    st_new = st * w_pre[CHUNK - 1:CHUNK, :] + upd
    state[...] = st_new

    @pl.when(ci == n_chunks - 1)
    def _():
        st_ref[...] = st_new.T


def _gla(ug, tail, w2, b2, norm_g, s0, tables, *, valid_len=CHUNK):
    bsz, s, _ = ug.shape
    n_chunks = s // CHUNK
    mall, lv = tables
    w2pad = jnp.zeros((LANES, GLA_KW), F32).at[N_FOX_HEADS:N_FOX_HEADS + GLA_RANK].set(w2)
    const2 = lambda b, i: (0, 0)
    col = lambda c: pl.BlockSpec((None, CHUNK, TN), lambda b, i: (b, i, c))
    o, st = pl.pallas_call(
        functools.partial(_gla_kernel, valid_len=valid_len, n_chunks=n_chunks),
        grid=(bsz, n_chunks),
        in_specs=[col(2), col(3), col(4),
                  pl.BlockSpec((None, CHUNK, LANES), lambda b, i: (b, i, 0)),
                  pl.BlockSpec((LANES, GLA_KW), const2),
                  pl.BlockSpec((1, GLA_KW), const2),
                  pl.BlockSpec((1, GLA_WIDTH), const2),
                  pl.BlockSpec((None, GLA_KW, LANES), lambda b, i: (b, 0, 0)),
                  pl.BlockSpec(mall.shape, const2),
                  pl.BlockSpec(lv.shape, const2)],
        out_specs=[pl.BlockSpec((None, CHUNK, GLA_WIDTH), lambda b, i: (b, i, 0)),
                   pl.BlockSpec((None, GLA_KW, LANES), lambda b, i: (b, 0, 0))],
        out_shape=[jax.ShapeDtypeStruct((bsz, s, GLA_WIDTH), BF16),
                   jax.ShapeDtypeStruct((bsz, GLA_KW, LANES), F32)],
        scratch_shapes=[pltpu.VMEM((LANES, GLA_KW), F32)],
        compiler_params=_cparams("parallel", "arbitrary"),
        name="gla",
    )(ug, ug, ug, tail, w2pad, b2.reshape(1, GLA_KW), norm_g.reshape(1, GLA_WIDTH),
      s0.reshape(bsz, GLA_KW, LANES), mall, lv)
    return o, st.reshape(bsz, GLA_HEADS, GLA_DK, LANES)


PAGES_PER_STEP = 16
PAGE_ROWS = PAGE * N_FOX_HEADS


def _dec_tables():
    t = np.arange(PAGE)
    after = (t[:, None] > t[None, :]).astype(np.float32)
    spread = (t[:, None] == (np.arange(PAGE_ROWS) // N_FOX_HEADS)[None, :]).astype(np.float32)
    return jnp.asarray(after, BF16), jnp.asarray(spread, BF16)


def _dec_kernel(pt_ref, *refs, n_steps):
    n = PAGES_PER_STEP
    f_refs, k_refs, v_refs = refs[:n], refs[n:2 * n], refs[2 * n:3 * n]
    q_ref, kn_ref, vn_ref, fn_ref, after_ref, spread_ref, o_ref, m_sc, l_sc, acc_sc, carry = refs[3 * n:]
    i = pl.program_id(1)
    scale = HEAD_DIM ** -0.5
    q = q_ref[...]

    @pl.when(i == 0)
    def _():
        s_self = jnp.sum(q.astype(BF16).astype(F32) * kn_ref[...].astype(BF16).astype(F32), axis=-1, keepdims=True) * scale
        m_sc[...] = jnp.broadcast_to(s_self, m_sc.shape)
        l_sc[...] = jnp.ones(l_sc.shape, F32)
        acc_sc[...] = vn_ref[...].astype(BF16).astype(F32)
        carry[...] = fn_ref[...]

    lhs = jnp.concatenate([f_ref[...] for f_ref in f_refs], axis=0)
    suffix = _dot_exact(lhs, after_ref[...])
    total = _dot_exact(lhs, jnp.ones((PAGE, LANES), BF16))
    run = carry[...]
    bias = [None] * n
    for j in reversed(range(n)):
        sl = slice(j * N_FOX_HEADS, (j + 1) * N_FOX_HEADS)
        bias[j] = suffix[sl] + run
        run = run + total[sl]
    carry[...] = run
    bias_x = _dot_exact(jnp.concatenate(bias, axis=0), spread_ref[...])

    q_b = q.astype(BF16)
    valid = lax.broadcasted_iota(jnp.int32, (N_FOX_HEADS, PAGE_ROWS), 1) % N_FOX_HEADS == \
        lax.broadcasted_iota(jnp.int32, (N_FOX_HEADS, PAGE_ROWS), 0)
    scores = []
    for j, k_ref in enumerate(k_refs):
        s = _dot_nt(q_b, k_ref[...].astype(BF16)) * scale + bias_x[j * N_FOX_HEADS:(j + 1) * N_FOX_HEADS]
        scores.append(jnp.where(valid, s, -jnp.inf))
    m_prev = m_sc[...]
    m_cur = scores[0]
    for s in scores[1:]:
        m_cur = jnp.maximum(m_cur, s)
    m_new = jnp.maximum(m_prev, jnp.max(m_cur, axis=1, keepdims=True))
    alpha = jnp.exp(m_prev - m_new)
    m_wide = jnp.concatenate([m_new] * (PAGE_ROWS // LANES), axis=1)
    acc = alpha * acc_sc[...]
    l_part = jnp.zeros((N_FOX_HEADS, PAGE_ROWS), F32)
    for s, v_ref in zip(scores, v_refs):
        p = jnp.exp(s - m_wide)
        l_part = l_part + p
        acc = acc + _dot(p.astype(BF16), v_ref[...].astype(BF16))
    l_new = alpha * l_sc[...] + jnp.sum(l_part, axis=1, keepdims=True)
    m_sc[...] = m_new
    l_sc[...] = l_new
    acc_sc[...] = acc

    @pl.when(i == n_steps - 1)
    def _():
        o_ref[...] = acc / l_new


def _fox_decode(page_table, k_pool, v_pool, f_pool, layer, q, k_new, v_new, logf_new_b, tables):
    bd, n_pages = page_table.shape
    n_steps = n_pages // PAGES_PER_STEP
    depth, n_pool = k_pool.shape[:2]
    k_view = k_pool.reshape(depth, n_pool, PAGE_ROWS, HEAD_DIM)
    v_view = v_pool.reshape(depth, n_pool, PAGE_ROWS, HEAD_DIM)
    f_view = jnp.transpose(f_pool, (0, 1, 3, 2))
    after, spread = tables

    def page_of(b, i, pt, j):
        return pt[b, (n_steps - 1 - i) * PAGES_PER_STEP + j]

    def page_spec(j, rows):
        return pl.BlockSpec((None, None, rows, LANES), lambda b, i, pt: (layer, page_of(b, i, pt, j), 0, 0))

    vec = pl.BlockSpec((None, N_FOX_HEADS, HEAD_DIM), lambda b, i, pt: (b, 0, 0))
    const = lambda a: pl.BlockSpec(a.shape, lambda b, i, pt: (0, 0))
    pages = range(PAGES_PER_STEP)
    grid_spec = pltpu.PrefetchScalarGridSpec(
        num_scalar_prefetch=1,
        grid=(bd, n_steps),
        in_specs=[page_spec(j, N_FOX_HEADS) for j in pages] + [page_spec(j, PAGE_ROWS) for j in pages] * 2
        + [vec, vec, vec, vec, const(after), const(spread)],
        out_specs=vec,
        scratch_shapes=[pltpu.VMEM((N_FOX_HEADS, LANES), F32)] * 4,
    )
    return pl.pallas_call(
        functools.partial(_dec_kernel, n_steps=n_steps),
        grid_spec=grid_spec,
        out_shape=jax.ShapeDtypeStruct((bd, N_FOX_HEADS, HEAD_DIM), F32),
        compiler_params=_cparams("parallel", "arbitrary"),
        name="fox_decode",
    )(page_table, *([f_view] * PAGES_PER_STEP), *([k_view] * PAGES_PER_STEP), *([v_view] * PAGES_PER_STEP),
      q, k_new, v_new, logf_new_b, after, spread)


def _pad_rows(x, rows):
    return jnp.pad(x[:, None, :], ((0, 0), (0, rows - 1), (0, 0)))


def _rows16(x):
    return jnp.pad(x, ((0, SROWS - x.shape[0]), (0, 0)))


def kernel(x_prompt, x_sample, cache_k, cache_v, cache_logf, state_gla, page_table, c_prompt, c_sample,
           ln_in_g, ln_in_b, w_ada, b_ada, w_in, b_f, gm_ln_g, gm_ln_b, gm_ws, gm_bs, gla_w2, gla_b2,
           gla_norm_g, w_o, ln1_g, ln1_b, w_gu, w_down, ln2_g, ln2_b):
    bp, seq, d = x_prompt.shape
    bd = x_sample.shape[0]
    depth = w_ada.shape[0]
    assert d == 4 * GLA_WIDTH and x_sample.shape[1] == 1 and bp + bd <= 16 and bd <= SROWS
    assert w_in.shape[2] == IN_WIDTH
    alpha = (2 * depth) ** 0.25
    m_p = bp * seq

    gla_tables = _gla_tables()
    dec_tables = _dec_tables()
    ext_tables = _ext_tables()

    c16 = jnp.zeros((SROWS, d), F32).at[:bd].set(c_sample).at[bd:bd + bp].set(c_prompt)
    mods = _ada(c16, w_ada, b_ada)

    def mod_p(l, which):
        return mods[l, bd:bd + bp, which * d:(which + 1) * d][:, None, :]

    def mod_s(l, *which):
        return jnp.concatenate([mods[l, :, w * d:(w + 1) * d] for w in which], axis=-1)

    w_in_b = _pack_in(w_in)
    w_o_b = _cast(w_o)
    w_down_b = _cast(w_down)

    xp, hp = _ln_mod(x_prompt, ln_in_g, ln_in_b, mod=(mod_p(0, 1), mod_p(0, 0)), tr=256)
    xs, hs = _ln_mod(x_sample, ln_in_g, ln_in_b,
                     mod=(mods[0, :bd, d:2 * d][:, None, :], mods[0, :bd, 0:d][:, None, :]), tr=1)
    xp, hp = xp.reshape(m_p, d), hp.reshape(m_p, d)
    xs, hs = _rows16(xs[:, 0, :]), _rows16(hs[:, 0, :])

    zero_state = jnp.zeros((bp, GLA_HEADS, GLA_DK, LANES), F32)
    k_all = v_all = None
    fp, sp = [], []
    kd, vd, fd, sd, gd = [], [], [], [], []
    for l in range(depth):
        last = l == depth - 1
        q16, k_all, k16, v_all, vt16, ug, tail, samp = _proj(hp, hs, w_in_b, l, k_all, v_all, depth, tm=256)

        samp = samp[:bd]
        heads3 = lambda a: a.reshape(bd, N_FOX_HEADS, HEAD_DIM)
        q_s = heads3(samp[:, 0:FOX_WIDTH])
        k_s = heads3(samp[:, FOX_WIDTH:2 * FOX_WIDTH])
        v_s = heads3(samp[:, 2 * FOX_WIDTH:3 * FOX_WIDTH])
        ug_s = _pad_rows(samp[:, 6 * TN:11 * TN], CHUNK)
        tail_s = _pad_rows(samp[:, 11 * TN:11 * TN + LANES], CHUNK)
        logf_s, _, _ = _fox_cum(tail_s, b_f[l], ext_tables)
        logf_s = logf_s[:, 0, :]
        fox_s = _fox_decode(page_table, cache_k, cache_v, cache_logf, l, q_s, k_s, v_s,
                            jnp.broadcast_to(logf_s[:, :, None], (bd, N_FOX_HEADS, LANES)), dec_tables)
        gm_s, vn_s = _gmlp(ug_s, gm_ln_g[l], gm_ln_b[l], gm_ws[l], gm_bs[l], emit_vn=True)
        gla_s, st_s = _gla(ug_s, tail_s, gla_w2[l], gla_b2[l], gla_norm_g[l], state_gla[l], gla_tables, valid_len=1)
        mix_s = jnp.concatenate([fox_s.reshape(bd, -1).astype(BF16), gm_s[:, 0, :], gla_s[:, 0, :]], axis=-1)

        tail3 = tail.reshape(bp, seq, LANES)
        ug3 = ug.reshape(bp, seq, UG_WIDTH)
        logf, ext_q, ext_k = _fox_cum(tail3, b_f[l], ext_tables)
        fox_o = _fox_prompt(q16.reshape(bp, seq, FOX_WIDTH), k16.reshape(bp, seq, FOX_WIDTH), vt16, ext_q, ext_k)
        gm_o, _ = _gmlp(ug3, gm_ln_g[l], gm_ln_b[l], gm_ws[l], gm_bs[l], emit_vn=False)
        gla_o, st = _gla(ug3, tail3, gla_w2[l], gla_b2[l], gla_norm_g[l], zero_state, gla_tables)

        x1, h1, x1_s, h1_s = _dense_ln(
            [fox_o.reshape(m_p, -1), gm_o.reshape(m_p, -1), gla_o.reshape(m_p, -1)], _rows16(mix_s), w_o_b, l,
            xp, xs, mod_p(l, 2), ln1_g[l], ln1_b[l], (mod_p(l, 4), mod_p(l, 3)), mod_s(l, 2, 4, 3),
            alpha=alpha, rows_per_batch=seq, tm=512)
        act, act_s = _swiglu(h1, h1_s, w_gu, l, tm=2048, tn=512)
        nxt = None if last else (mod_p(l + 1, 1), mod_p(l + 1, 0))
        nxt_s = mod_s(l, 5, 5, 5) if last else jnp.concatenate([mod_s(l, 5), mod_s(l + 1, 1, 0)], axis=-1)
        xp, hp, xs, hs = _dense_ln([act], act_s, w_down_b, l, x1, x1_s, mod_p(l, 5), ln2_g[l], ln2_b[l], nxt, nxt_s,
                                   alpha=alpha, rows_per_batch=seq, tm=256)

        fp.append(logf)
        sp.append(st)
        kd.append(k_s[:, None])
        vd.append(v_s[:, None])
        fd.append(logf_s[:, None, :])
        sd.append(st_s)
        gd.append(vn_s[:, 0:1, :])

    kv_shape = (depth, bp, seq, N_FOX_HEADS, HEAD_DIM)
    return (xp.reshape(bp, seq, d), xs[:bd, None, :], k_all.reshape(kv_shape), v_all.reshape(kv_shape),
            jnp.stack(fp), jnp.stack(sp), jnp.stack(kd), jnp.stack(vd), jnp.stack(fd), jnp.stack(sd), jnp.stack(gd))
```

```python
import functools
import math

import numpy as np
import jax
import jax.numpy as jnp
from jax import lax
from jax.experimental import pallas as pl
from jax.experimental.pallas import tpu as pltpu

F32 = jnp.float32
BF16 = jnp.bfloat16

LANES = 128
HEAD_DIM = 128
N_FOX_HEADS = 8
FOX_WIDTH = N_FOX_HEADS * HEAD_DIM
GMLP_GROUPS = 4
GMLP_WIDTH = GMLP_GROUPS * LANES
GLA_HEADS = 4
GLA_DK = 64
GLA_KW = GLA_HEADS * GLA_DK
GLA_WIDTH = GLA_HEADS * LANES
GLA_RANK = 16
GLA_TAU = 16.0
CHUNK = 128
PAGE = 128
LN_EPS = 1e-5
LOG2E = math.log2(math.e)
VMEM_LIMIT = 56 * 1024 * 1024
PROJ_VMEM_LIMIT = 60 * 1024 * 1024
SROWS = 16

TN = 512
OFF_FF = 3 * FOX_WIDTH
OFF_MU = OFF_FF + N_FOX_HEADS
OFF_GR = OFF_MU + 2 * GMLP_WIDTH + 2 * GLA_KW + GLA_WIDTH
OFF_GG = OFF_GR + GLA_RANK
IN_WIDTH = OFF_GG + GLA_WIDTH
N_PROJ_TILES = 12
UG_WIDTH = 5 * TN


def _cparams(*sem):
    return pltpu.CompilerParams(dimension_semantics=sem, vmem_limit_bytes=VMEM_LIMIT)


def _log_sigmoid(x):
    return jnp.minimum(x, 0.0) - jnp.log1p(jnp.exp(-jnp.abs(x)))


def _silu(x):
    return x * (1.0 / (1.0 + jnp.exp(-x)))


def _split3(x):
    hi = x.astype(BF16)
    r = x - hi.astype(F32)
    mid = r.astype(BF16)
    lo = (r - mid.astype(F32)).astype(BF16)
    return hi, mid, lo


def _dot(a, b):
    return jnp.dot(a, b, preferred_element_type=F32)


def _dot_nt(a, b):
    return lax.dot_general(a, b, (((1,), (1,)), ((), ())), preferred_element_type=F32)


def _dot_exact(x, w):
    hi, mid, lo = _split3(x)
    return _dot(hi, w) + _dot(mid, w) + _dot(lo, w)


def _dot_exact_l(w, x):
    hi, mid, lo = _split3(x)
    return _dot(w, hi) + _dot(w, mid) + _dot(w, lo)


def _ada_kernel(c_ref, w_ref, b_ref, o_ref):
    a = _silu(c_ref[...]).astype(BF16)
    o_ref[...] = _dot(a, w_ref[...].astype(BF16)) + b_ref[...]


def _ada(c16, w_ada, b_ada):
    depth, d, n = w_ada.shape
    tn = 1024
    return pl.pallas_call(
        _ada_kernel,
        grid=(depth, n // tn),
        in_specs=[pl.BlockSpec((16, d), lambda l, j: (0, 0)),
                  pl.BlockSpec((None, d, tn), lambda l, j: (l, 0, j)),
                  pl.BlockSpec((None, 1, tn), lambda l, j: (l, 0, j))],
        out_specs=pl.BlockSpec((None, 16, tn), lambda l, j: (l, 0, j)),
        out_shape=jax.ShapeDtypeStruct((depth, 16, n), F32),
        compiler_params=_cparams("parallel", "parallel"),
        name="ada",
    )(c16, w_ada, b_ada.reshape(depth, 1, n))


def _ln_kernel(*refs, has_res, has_mod, alpha):
    it = iter(refs)
    x = next(it)[...]
    if has_res:
        y = next(it)[...]
        gate = next(it)[...]
        x = alpha * x + gate * y
    g = next(it)[...]
    b = next(it)[...]
    if has_mod:
        sc = next(it)[...]
        sh = next(it)[...]
    xn_ref = next(it)
    mu = jnp.mean(x, axis=-1, keepdims=True)
    xc = x - mu
    var = jnp.mean(xc * xc, axis=-1, keepdims=True)
    xn = xc * lax.rsqrt(var + LN_EPS) * g + b
    xn_ref[...] = xn
    if has_mod:
        h_ref = next(it)
        h_ref[...] = (xn * (1.0 + sc) + sh).astype(BF16)


def _ln_mod(x, g, b, *, res=None, mod=None, alpha=1.0, tr):
    bsz, s, d = x.shape
    tr = min(tr, s)
    row = pl.BlockSpec((None, tr, d), lambda i, j: (i, j, 0))
    per_b = pl.BlockSpec((None, 1, d), lambda i, j: (i, 0, 0))
    vec = pl.BlockSpec((1, d), lambda i, j: (0, 0))
    args, specs = [x], [row]
    if res is not None:
        y, gate = res
        args += [y, gate]
        specs += [row, per_b]
    args += [g.reshape(1, d), b.reshape(1, d)]
    specs += [vec, vec]
    out_shape = [jax.ShapeDtypeStruct((bsz, s, d), F32)]
    out_specs = [row]
    if mod is not None:
        sc, sh = mod
        args += [sc, sh]
        specs += [per_b, per_b]
        out_shape.append(jax.ShapeDtypeStruct((bsz, s, d), BF16))
        out_specs.append(row)
    out = pl.pallas_call(
        functools.partial(_ln_kernel, has_res=res is not None, has_mod=mod is not None, alpha=alpha),
        grid=(bsz, s // tr),
        in_specs=specs, out_specs=out_specs, out_shape=out_shape,
        compiler_params=_cparams("parallel", "parallel"),
        name="ln_mod",
    )(*args)
    return out if mod is not None else (out[0], None)


PACK_EXTRA = 32


def _pack_in_kernel(wm_ref, wn_ref, wg_ref, o_ref):
    j = pl.program_id(1)
    k_dim = wm_ref.shape[1]

    @pl.when(j < 6)
    def _():
        o_ref[...] = wm_ref[...].T.astype(BF16)

    def shifted(rows):
        return jnp.concatenate([wm_ref[rows:TN, :], wn_ref[0:rows, :]], axis=0).T.astype(BF16)

    @pl.when((j >= 6) & (j < 10))
    def _():
        o_ref[...] = shifted(OFF_MU - OFF_FF)

    @pl.when(j == 10)
    def _():
        o_ref[...] = shifted(OFF_GG - 10 * TN)

    @pl.when(j == N_PROJ_TILES - 1)
    def _():
        lo = OFF_GR - (OFF_GR // PACK_EXTRA) * PACK_EXTRA
        t = jnp.concatenate([wn_ref[0:N_FOX_HEADS, :], wg_ref[lo:lo + GLA_RANK, :],
                             jnp.zeros((TN - N_FOX_HEADS - GLA_RANK, k_dim), F32)], axis=0)
        o_ref[...] = t.T.astype(BF16)


def _pack_in(w_in):
    depth, k, _ = w_in.shape
    w_t = jnp.transpose(w_in, (0, 2, 1))
    per = TN // PACK_EXTRA
    nb = lambda j: jnp.where((j >= 6) & (j < N_PROJ_TILES - 1), per * (j + 1), OFF_FF // PACK_EXTRA)
    return pl.pallas_call(
        _pack_in_kernel,
        grid=(depth, N_PROJ_TILES),
        in_specs=[pl.BlockSpec((None, TN, k), lambda l, j: (l, jnp.minimum(j, 10), 0)),
                  pl.BlockSpec((None, PACK_EXTRA, k), lambda l, j: (l, nb(j), 0)),
                  pl.BlockSpec((None, PACK_EXTRA, k), lambda l, j: (l, OFF_GR // PACK_EXTRA, 0))],
        out_specs=pl.BlockSpec((None, k, TN), lambda l, j: (l, 0, j)),
        out_shape=jax.ShapeDtypeStruct((depth, k, N_PROJ_TILES * TN), BF16),
        compiler_params=_cparams("parallel", "parallel"),
        name="pack_in",
    )(w_t, w_t, w_t)


def _cast_kernel(w_ref, o_ref):
    o_ref[...] = w_ref[...].astype(BF16)


def _cast(w):
    depth, k, n = w.shape
    blk = pl.BlockSpec((None, k, TN), lambda l, j: (l, 0, j))
    return pl.pallas_call(
        _cast_kernel, grid=(depth, n // TN), in_specs=[blk], out_specs=blk,
        out_shape=jax.ShapeDtypeStruct(w.shape, BF16),
        compiler_params=_cparams("parallel", "parallel"), name="cast",
    )(w)


def _proj_kernel(*refs, aliased, q_scale, tm):
    x_ref, xs_ref, w_ref = refs[:3]
    q_ref, k32_ref, k16_ref, v32_ref, vt_ref, ug_ref, t_ref, s_ref = refs[3 + (2 if aliased else 0):]
    x = x_ref[...]
    q_ref[...] = (_dot(x, w_ref[:, 0:FOX_WIDTH]) * q_scale).astype(BF16)
    k = _dot(x, w_ref[:, FOX_WIDTH:2 * FOX_WIDTH])
    k16_ref[...] = k.astype(BF16)
    v = _dot(x, w_ref[:, 2 * FOX_WIDTH:3 * FOX_WIDTH])
    vt_ref[...] = v.T.astype(BF16)
    for h in range(N_FOX_HEADS):
        cols = slice(h * HEAD_DIM, (h + 1) * HEAD_DIM)
        k32_ref[pl.ds(h, tm, stride=N_FOX_HEADS), :] = k[:, cols]
        v32_ref[pl.ds(h, tm, stride=N_FOX_HEADS), :] = v[:, cols]
    ug_ref[...] = _dot(x, w_ref[:, 6 * TN:11 * TN])
    t_ref[...] = _dot(x, w_ref[:, 11 * TN:11 * TN + LANES])

    @pl.when(pl.program_id(0) == 0)
    def _():
        s_ref[...] = _dot(xs_ref[...], w_ref[...])


def _proj(x, xs, w_in, layer, k_prev, v_prev, depth, *, tm):
    m, k = x.shape
    tm = min(tm, m)
    n_all = w_in.shape[2]
    in_specs = [pl.BlockSpec((tm, k), lambda i: (i, 0)),
                pl.BlockSpec((SROWS, k), lambda i: (0, 0)),
                pl.BlockSpec((None, k, n_all), lambda i: (layer, 0, 0), pipeline_mode=pl.Buffered(1))]
    args = [x, xs, w_in]
    aliases = {}
    if k_prev is not None:
        in_specs += [pl.BlockSpec(memory_space=pl.ANY)] * 2
        args += [k_prev, v_prev]
        aliases = {3: 1, 4: 3}
    rows = pl.BlockSpec((tm, FOX_WIDTH), lambda i: (i, 0))
    pairs = pl.BlockSpec((None, tm * N_FOX_HEADS, HEAD_DIM), lambda i: (layer, i, 0))
    out_shape = [jax.ShapeDtypeStruct((m, FOX_WIDTH), BF16),
                 jax.ShapeDtypeStruct((depth, m * N_FOX_HEADS, HEAD_DIM), F32),
                 jax.ShapeDtypeStruct((m, FOX_WIDTH), BF16),
                 jax.ShapeDtypeStruct((depth, m * N_FOX_HEADS, HEAD_DIM), F32),
                 jax.ShapeDtypeStruct((FOX_WIDTH, m), BF16),
                 jax.ShapeDtypeStruct((m, UG_WIDTH), F32),
                 jax.ShapeDtypeStruct((m, LANES), F32),
                 jax.ShapeDtypeStruct((SROWS, n_all), F32)]
    out_specs = [rows, pairs, rows, pairs,
                 pl.BlockSpec((FOX_WIDTH, tm), lambda i: (0, i)),
                 pl.BlockSpec((tm, UG_WIDTH), lambda i: (i, 0)),
                 pl.BlockSpec((tm, LANES), lambda i: (i, 0)),
                 pl.BlockSpec((SROWS, n_all), lambda i: (0, 0))]
    return pl.pallas_call(
        functools.partial(_proj_kernel, aliased=k_prev is not None, q_scale=HEAD_DIM ** -0.5 * LOG2E, tm=tm),
        grid=(m // tm,),
        in_specs=in_specs, out_specs=out_specs, out_shape=out_shape,
        input_output_aliases=aliases,
        compiler_params=pltpu.CompilerParams(dimension_semantics=("arbitrary",), vmem_limit_bytes=PROJ_VMEM_LIMIT),
        name="proj",
    )(*args)


SWIGLU_SUB = 1024


def _swiglu_kernel(x_ref, xs_ref, wg_ref, wu_ref, o_ref, os_ref, wbg, wbu):
    def result(x):
        return (_silu(_dot(x, wbg[...])) * _dot(x, wbu[...])).astype(BF16)

    @pl.when(pl.program_id(1) == 0)
    def _():
        wbg[...] = wg_ref[...].astype(BF16)
        wbu[...] = wu_ref[...].astype(BF16)
        os_ref[...] = result(xs_ref[...])

    sub = min(SWIGLU_SUB, x_ref.shape[0])
    for r0 in range(0, x_ref.shape[0], sub):
        o_ref[r0:r0 + sub, :] = result(x_ref[r0:r0 + sub, :])


def _swiglu(x, x_sample, w_gu, layer, *, tm, tn):
    m, k = x.shape
    tm = min(tm, m)
    n = w_gu.shape[2] // 2
    up0 = n // tn
    return pl.pallas_call(
        _swiglu_kernel,
        grid=(n // tn, m // tm),
        in_specs=[pl.BlockSpec((tm, k), lambda j, i: (i, 0)),
                  pl.BlockSpec((SROWS, k), lambda j, i: (0, 0)),
                  pl.BlockSpec((None, k, tn), lambda j, i: (layer, 0, j)),
                  pl.BlockSpec((None, k, tn), lambda j, i: (layer, 0, up0 + j))],
        out_specs=[pl.BlockSpec((tm, tn), lambda j, i: (i, j)), pl.BlockSpec((SROWS, tn), lambda j, i: (0, j))],
        out_shape=[jax.ShapeDtypeStruct((m, n), BF16), jax.ShapeDtypeStruct((SROWS, n), BF16)],
        scratch_shapes=[pltpu.VMEM((k, tn), BF16)] * 2,
        compiler_params=_cparams("arbitrary", "arbitrary"),
        name="swiglu",
    )(x, x_sample, w_gu, w_gu)


DENSE_LN_SUB = 128


def _layer_norm(z, g, b):
    mu = jnp.mean(z, axis=-1, keepdims=True)
    zc = z - mu
    var = jnp.mean(zc * zc, axis=-1, keepdims=True)
    return zc * lax.rsqrt(var + LN_EPS) * g + b


def _dense_ln_kernel(*refs, n_lhs, has_mod, alpha):
    it = iter(refs)
    x_refs = [next(it) for _ in range(n_lhs)]
    xs_ref, w_ref, res_ref, gate_ref, g_ref, b_ref = [next(it) for _ in range(6)]
    sc_ref, sh_ref = (next(it), next(it)) if has_mod else (None, None)
    res_s_ref, mod_s_ref = next(it), next(it)
    xn_ref = next(it)
    h_ref = next(it) if has_mod else None
    xn_s_ref = next(it)
    h_s_ref = next(it) if has_mod else None
    d = xn_ref.shape[-1]

    def finish(lhs, res, gate, sc, sh):
        xn = _layer_norm(alpha * res + gate * _dot(lhs, w_ref[...]), g_ref[...], b_ref[...])
        return xn, ((xn * (1.0 + sc) + sh).astype(BF16) if has_mod else None)

    @pl.when(pl.program_id(0) == 0)
    def _():
        mod_s = mod_s_ref[...]
        xn, h = finish(xs_ref[...], res_s_ref[...], mod_s[:, 0:d], mod_s[:, d:2 * d], mod_s[:, 2 * d:3 * d])
        xn_s_ref[...] = xn
        if has_mod:
            h_s_ref[...] = h

    tm = xn_ref.shape[0]
    sub = min(tm, DENSE_LN_SUB)
    for r0 in range(0, tm, sub):
        rows = slice(r0, r0 + sub)
        parts = [x[rows, :] for x in x_refs]
        lhs = parts[0] if n_lhs == 1 else jnp.concatenate(parts, axis=1)
        xn, h = finish(lhs, res_ref[rows, :], gate_ref[...], sc_ref[...] if has_mod else None,
                       sh_ref[...] if has_mod else None)
        xn_ref[rows, :] = xn
        if has_mod:
            h_ref[rows, :] = h


def _dense_ln(xs_list, x_sample, w, layer, res, res_s, gate, ln_g, ln_b, mod, mod_s, *, alpha, rows_per_batch, tm):
    m = xs_list[0].shape[0]
    tm = min(tm, rows_per_batch)
    k, d = w.shape[1], w.shape[2]
    assert sum(x.shape[1] for x in xs_list) == k and x_sample.shape == (SROWS, k)
    steps_per_batch = rows_per_batch // tm
    has_mod = mod is not None
    row = pl.BlockSpec((tm, d), lambda i: (i, 0))
    per_b = pl.BlockSpec((None, 1, d), lambda i: (i // steps_per_batch, 0, 0))
    vec = pl.BlockSpec((1, d), lambda i: (0, 0))
    srow = pl.BlockSpec((SROWS, d), lambda i: (0, 0))
    in_specs = [pl.BlockSpec((tm, x.shape[1]), lambda i: (i, 0)) for x in xs_list]
    in_specs += [pl.BlockSpec((SROWS, k), lambda i: (0, 0)),
                 pl.BlockSpec((None, k, d), lambda i: (layer, 0, 0), pipeline_mode=pl.Buffered(1)),
                 row, per_b, vec, vec]
    args = list(xs_list) + [x_sample, w, res, gate, ln_g.reshape(1, d), ln_b.reshape(1, d)]
    if has_mod:
        in_specs += [per_b, per_b]
        args += list(mod)
    in_specs += [srow, pl.BlockSpec((SROWS, 3 * d), lambda i: (0, 0))]
    args += [res_s, mod_s]
    out_specs, out_shape = [row], [jax.ShapeDtypeStruct((m, d), F32)]
    if has_mod:
        out_specs.append(row)
        out_shape.append(jax.ShapeDtypeStruct((m, d), BF16))
    out_specs.append(srow)
    out_shape.append(jax.ShapeDtypeStruct((SROWS, d), F32))
    if has_mod:
        out_specs.append(srow)
        out_shape.append(jax.ShapeDtypeStruct((SROWS, d), BF16))
    out = pl.pallas_call(
        functools.partial(_dense_ln_kernel, n_lhs=len(xs_list), has_mod=has_mod, alpha=alpha),
        grid=(m // tm,),
        in_specs=in_specs, out_specs=out_specs, out_shape=out_shape,
        compiler_params=_cparams("arbitrary"),
        name="dense_ln",
    )(*args)
    return out if has_mod else (out[0], None, out[1], None)


EXT_PER_HEAD = 6


def _ext_tables():
    pq = np.zeros((3 * LANES, LANES), np.float32)
    pk = np.zeros((3 * LANES, LANES), np.float32)
    oq = np.zeros((1, LANES), np.float32)
    ok = np.zeros((1, LANES), np.float32)
    for h in range(N_FOX_HEADS):
        for part in range(3):
            pq[part * LANES + h, EXT_PER_HEAD * h + part] = 1.0
            pk[part * LANES + h, EXT_PER_HEAD * h + 3 + part] = -1.0
            oq[0, EXT_PER_HEAD * h + 3 + part] = 1.0
            ok[0, EXT_PER_HEAD * h + part] = 1.0
    return jnp.asarray(pq, BF16), jnp.asarray(pk, BF16), jnp.asarray(oq), jnp.asarray(ok)


def _fox_cum_kernel(t_ref, bf_ref, pq_ref, pk_ref, oq_ref, ok_ref, lf_ref, eq_ref, ek_ref, carry, *, tb):
    @pl.when(pl.program_id(1) == 0)
    def _():
        carry[...] = jnp.zeros(carry.shape, F32)

    lf = _log_sigmoid(t_ref[...] + bf_ref[...])
    lf_ref[...] = lf[:, 0:N_FOX_HEADS]
    r = lax.broadcasted_iota(jnp.int32, (tb, tb), 0)
    c = lax.broadcasted_iota(jnp.int32, (tb, tb), 1)
    lower = (c <= r).astype(BF16)
    cum = _dot_exact_l(lower, lf) + carry[0:1, :]
    carry[...] = jnp.broadcast_to(cum[tb - 1:tb, :], carry.shape)
    hi, mid, lo = _split3(cum * LOG2E)
    cat = jnp.concatenate([hi, mid, lo], axis=1)
    eq_ref[...] = (_dot(cat, pq_ref[...]) + oq_ref[...]).astype(BF16)
    ek_ref[...] = (_dot(cat, pk_ref[...]) + ok_ref[...]).astype(BF16)


def _fox_cum(tail, b_f, tables):
    bsz, s, _ = tail.shape
    tb = min(512, s)
    pq, pk, oq, ok = tables
    bf = jnp.zeros((1, LANES), F32).at[0, :N_FOX_HEADS].set(b_f)
    blk = pl.BlockSpec((None, tb, LANES), lambda b, i: (b, i, 0))
    const = lambda shape: pl.BlockSpec(shape, lambda b, i: (0, 0))
    return pl.pallas_call(
        functools.partial(_fox_cum_kernel, tb=tb),
        grid=(bsz, s // tb),
        in_specs=[blk, const((1, LANES)), const(pq.shape), const(pk.shape), const((1, LANES)), const((1, LANES))],
        out_specs=[pl.BlockSpec((None, tb, N_FOX_HEADS), lambda b, i: (b, i, 0)), blk, blk],
        out_shape=[jax.ShapeDtypeStruct((bsz, s, N_FOX_HEADS), F32),
                   jax.ShapeDtypeStruct((bsz, s, LANES), BF16),
                   jax.ShapeDtypeStruct((bsz, s, LANES), BF16)],
        scratch_shapes=[pltpu.VMEM((8, LANES), F32)],
        compiler_params=_cparams("parallel", "arbitrary"),
        name="fox_cum",
    )(tail, bf, pq, pk, oq, ok)


HEADS_PER_STEP = 8
ATTN_TILE = 1024


def _fox_kernel(qi_ref, kj_ref, q_ref, k_ref, vt_ref, eq_ref, ek_ref, o_ref, m_sc, l_sc, acc_sc, qx_sc, *, tq, tk):
    grp = pl.program_id(1)
    p = pl.program_id(2)
    qi = qi_ref[p]
    kj = kj_ref[p]

    @pl.when(kj == 0)
    def _():
        m_sc[...] = jnp.full(m_sc.shape, -jnp.inf, F32)
        l_sc[...] = jnp.zeros(l_sc.shape, F32)
        acc_sc[...] = jnp.zeros(acc_sc.shape, F32)
        lane = lax.broadcasted_iota(jnp.int32, (tq, LANES), 1)
        eq = eq_ref[...]
        for hh in range(HEADS_PER_STEP):
            lo = (grp * HEADS_PER_STEP + hh) * EXT_PER_HEAD
            own = (lane >= lo) & (lane < lo + EXT_PER_HEAD)
            qx_sc[hh] = jnp.concatenate([q_ref[:, hh * LANES:(hh + 1) * LANES],
                                         jnp.where(own, eq, jnp.zeros_like(eq))], axis=1)

    def step(diagonal):
        ek = ek_ref[...]
        for hh in range(HEADS_PER_STEP):
            cols = slice(hh * LANES, (hh + 1) * LANES)
            kx = jnp.concatenate([k_ref[:, cols], ek], axis=1)
            s = _dot_nt(kx, qx_sc[hh])
            if diagonal:
                key = lax.broadcasted_iota(jnp.int32, (tk, tq), 0)
                qry = lax.broadcasted_iota(jnp.int32, (tk, tq), 1)
                s = jnp.where(key <= qry, s, -jnp.inf)
            m_prev = m_sc[hh][0:1, :]
            m_new = jnp.maximum(m_prev, jnp.max(s, axis=0, keepdims=True))
            alpha = jnp.exp2(m_prev - m_new)
            pr = jnp.exp2(s - m_new)
            l_new = alpha * l_sc[hh][0:1, :] + jnp.sum(pr, axis=0, keepdims=True)
            acc = alpha * acc_sc[hh] + _dot(vt_ref[cols, :], pr.astype(BF16))
            m_sc[hh] = jnp.broadcast_to(m_new, (8, tq))
            l_sc[hh] = jnp.broadcast_to(l_new, (8, tq))
            acc_sc[hh] = acc
            if diagonal:
                o_ref[:, cols] = (acc / l_new).T.astype(o_ref.dtype)

    @pl.when(kj == qi)
    def _():
        step(True)

    @pl.when(kj != qi)
    def _():
        step(False)


def _fox_prompt(q, k, vt, ext_q, ext_k):
    bsz, s, _ = q.shape
    tq = tk = min(ATTN_TILE, s)
    nq = s // tq
    width = HEADS_PER_STEP * LANES
    pairs = [(i, j) for i in range(nq) for j in range(i + 1)]
    qi_tbl = jnp.asarray([pr[0] for pr in pairs], jnp.int32)
    kj_tbl = jnp.asarray([pr[1] for pr in pairs], jnp.int32)
    grid_spec = pltpu.PrefetchScalarGridSpec(
        num_scalar_prefetch=2,
        grid=(bsz, N_FOX_HEADS // HEADS_PER_STEP, len(pairs)),
        in_specs=[pl.BlockSpec((None, tq, width), lambda b, g, p, qi, kj: (b, qi[p], g)),
                  pl.BlockSpec((None, tk, width), lambda b, g, p, qi, kj: (b, kj[p], g)),
                  pl.BlockSpec((width, tk), lambda b, g, p, qi, kj: (g, b * nq + kj[p])),
                  pl.BlockSpec((None, tq, LANES), lambda b, g, p, qi, kj: (b, qi[p], 0)),
                  pl.BlockSpec((None, tk, LANES), lambda b, g, p, qi, kj: (b, kj[p], 0))],
        out_specs=pl.BlockSpec((None, tq, width), lambda b, g, p, qi, kj: (b, qi[p], g)),
        scratch_shapes=[pltpu.VMEM((HEADS_PER_STEP, 8, tq), F32),
                        pltpu.VMEM((HEADS_PER_STEP, 8, tq), F32),
                        pltpu.VMEM((HEADS_PER_STEP, LANES, tq), F32),
                        pltpu.VMEM((HEADS_PER_STEP, tq, 2 * LANES), BF16)],
    )
    return pl.pallas_call(
        functools.partial(_fox_kernel, tq=tq, tk=tk),
        grid_spec=grid_spec,
        out_shape=jax.ShapeDtypeStruct((bsz, s, FOX_WIDTH), BF16),
        compiler_params=_cparams("parallel", "parallel", "arbitrary"),
        name="fox_prompt",
    )(qi_tbl, kj_tbl, q, k, vt, ext_q, ext_k)


GMLP_CHUNKS_PER_STEP = 4


def _gmlp_kernel(uv_ref, g_ref, b_ref, ws_ref, bs_ref, o_ref, *vn_ref):
    r = lax.broadcasted_iota(jnp.int32, (CHUNK, CHUNK), 0)
    c = lax.broadcasted_iota(jnp.int32, (CHUNK, CHUNK), 1)
    for grp in range(GMLP_GROUPS):
        lo = grp * LANES
        w = jnp.where(c <= r, ws_ref[grp], 0.0).astype(BF16)
        for r0 in range(0, uv_ref.shape[0], CHUNK):
            rows = slice(r0, r0 + CHUNK)
            u = jax.nn.gelu(uv_ref[rows, lo:lo + LANES])
            v = jax.nn.gelu(uv_ref[rows, GMLP_WIDTH + lo:GMLP_WIDTH + lo + LANES])
            mu = jnp.mean(v, axis=-1, keepdims=True)
            vc = v - mu
            var = jnp.mean(vc * vc, axis=-1, keepdims=True)
            vn = vc * lax.rsqrt(var + LN_EPS) * g_ref[:, lo:lo + LANES] + b_ref[:, lo:lo + LANES]
            mixed = _dot(w, vn.astype(BF16)) + bs_ref[grp]
            o_ref[rows, lo:lo + LANES] = (u * mixed).astype(o_ref.dtype)
            if vn_ref:
                vn_ref[0][rows, lo:lo + LANES] = vn


def _gmlp(ug, ln_g, ln_b, ws, bs, *, emit_vn):
    bsz, s, _ = ug.shape
    rows = CHUNK * min(GMLP_CHUNKS_PER_STEP, s // CHUNK)
    blk = pl.BlockSpec((None, rows, GMLP_WIDTH), lambda b, i: (b, i, 0))
    vec = pl.BlockSpec((1, GMLP_WIDTH), lambda b, i: (0, 0))
    cube = pl.BlockSpec((GMLP_GROUPS, CHUNK, CHUNK), lambda b, i: (0, 0, 0))
    out_shape = [jax.ShapeDtypeStruct((bsz, s, GMLP_WIDTH), BF16)]
    if emit_vn:
        out_shape.append(jax.ShapeDtypeStruct((bsz, s, GMLP_WIDTH), F32))
    bs_b = jnp.broadcast_to(bs[:, :, None], (GMLP_GROUPS, CHUNK, LANES))
    out = pl.pallas_call(
        _gmlp_kernel,
        grid=(bsz, s // rows),
        in_specs=[pl.BlockSpec((None, rows, 2 * GMLP_WIDTH), lambda b, i: (b, i, 0)), vec, vec, cube, cube],
        out_specs=[blk] * len(out_shape),
        out_shape=out_shape,
        compiler_params=_cparams("parallel", "parallel"),
        name="gmlp",
    )(ug, ln_g.reshape(1, GMLP_WIDTH), ln_b.reshape(1, GMLP_WIDTH), ws, bs_b)
    return out if emit_vn else (out[0], None)


_GLA_LEVELS = (64, 32, 16, 8, 4, 2, 1)
_N_LEVEL = len(_GLA_LEVELS)


def _gla_tables():
    x = np.arange(CHUNK)[:, None]
    j = np.arange(CHUNK)[None, :]
    groups = []
    lv = np.full((CHUNK, CHUNK), -1, np.int32)
    for i, half in enumerate(_GLA_LEVELS):
        anchor = (x // (2 * half)) * (2 * half) + half - 1
        groups.append((j > np.minimum(x, anchor)) & (j <= np.maximum(x, anchor)))
        same = (x // (2 * half)) == (j // (2 * half))
        upper_t = (x % (2 * half)) >= half
        lower_s = (j % (2 * half)) < half
        lv[same & upper_t & lower_s] = i
    lv[np.arange(CHUNK), np.arange(CHUNK)] = _N_LEVEL
    groups.append(j <= x)
    groups.append(j > x)
    mall = np.concatenate(groups, axis=0).astype(np.float32)
    return jnp.asarray(mall, BF16), jnp.asarray(np.tile(lv, (GLA_HEADS, 1)))


def _gla_kernel(qk_ref, v_ref, g_ref, t_ref, w2_ref, b2_ref, ng_ref, s0_ref, mall_ref, lv_ref, o_ref, st_ref, state,
                *, valid_len, n_chunks):
    ci = pl.program_id(1)

    @pl.when(ci == 0)
    def _():
        state[...] = s0_ref[...].T

    qk = qk_ref[...]
    q = qk[:, 0:GLA_KW] * (GLA_DK ** -0.5)
    k = qk[:, GLA_KW:2 * GLA_KW]
    v = v_ref[...]
    gate = g_ref[...]
    la = _log_sigmoid(_dot(t_ref[...].astype(BF16), w2_ref[...].astype(BF16)) + b2_ref[...]) * (1.0 / GLA_TAU)
    if valid_len < CHUNK:
        live = lax.broadcasted_iota(jnp.int32, (CHUNK, GLA_KW), 0) < valid_len
        la = jnp.where(live, la, 0.0)
        k = jnp.where(live, k, 0.0)

    la_hi = la.astype(BF16)
    la_lo = (la - la_hi.astype(F32)).astype(BF16)
    single = valid_len == 1
    mall = mall_ref[_N_LEVEL * CHUNK:, :] if single else mall_ref[...]
    w = jnp.exp(_dot(mall, la_hi) + _dot(mall, la_lo))

    lane_head = lax.broadcasted_iota(jnp.int32, (CHUNK, GLA_KW), 1) // GLA_DK
    heads = [lane_head == h for h in range(GLA_HEADS)]

    q_stack = jnp.concatenate([jnp.where(hm, q, 0.0) for hm in heads], axis=0)
    n_lv = 0 if single else _N_LEVEL

    def level_w(i):
        return w[i * CHUNK:(i + 1) * CHUNK]

    if single:
        a = _dot_nt(q_stack.astype(BF16), k.astype(BF16))
    else:
        lv = lv_ref[...]
        a = jnp.zeros((GLA_HEADS * CHUNK, CHUNK), F32)
        for i in range(_N_LEVEL):
            wi = level_w(i)
            qi = q_stack * jnp.concatenate([wi] * GLA_HEADS, axis=0)
            a = jnp.where(lv == i, _dot_nt(qi.astype(BF16), (k * wi).astype(BF16)), a)
        a = jnp.where(lv == _N_LEVEL, _dot_nt(q_stack.astype(BF16), k.astype(BF16)), a)

    w_pre = level_w(n_lv)
    w_suf = level_w(n_lv + 1)
    st = state[...]
    q_pre = q_stack * jnp.concatenate([w_pre] * GLA_HEADS, axis=0)
    o_inter = _dot_nt(q_pre.astype(BF16), st.astype(BF16))
    v_b = v.astype(BF16)
    for h in range(GLA_HEADS):
        rows = slice(h * CHUNK, (h + 1) * CHUNK)
        cols = slice(h * LANES, (h + 1) * LANES)
        o = _dot(a[rows].astype(BF16), v_b[:, cols]) + o_inter[rows]
        o = o * lax.rsqrt(jnp.mean(o * o, axis=-1, keepdims=True) + LN_EPS)
        o_ref[:, cols] = (o * ng_ref[:, cols] * _silu(gate[:, cols])).astype(o_ref.dtype)

    kvt = _dot(v.T.astype(BF16), (k * w_suf).astype(BF16))
    upd = jnp.zeros((LANES, GLA_KW), F32)
    for h in range(GLA_HEADS):
        upd = upd + jnp.where(heads[h], kvt[h * LANES:(h + 1) * LANES], 0.0)
    st_new = st * w_pre[CHUNK - 1:CHUNK, :] + upd
    state[...] = st_new

    @pl.when(ci == n_chunks - 1)
    def _():
        st_ref[...] = st_new.T


def _gla(ug, tail, w2, b2, norm_g, s0, tables, *, valid_len=CHUNK):
    bsz, s, _ = ug.shape
    n_chunks = s // CHUNK
    mall, lv = tables
    w2pad = jnp.zeros((LANES, GLA_KW), F32).at[N_FOX_HEADS:N_FOX_HEADS + GLA_RANK].set(w2)
    const2 = lambda b, i: (0, 0)
    col = lambda c: pl.BlockSpec((None, CHUNK, TN), lambda b, i: (b, i, c))
    o, st = pl.pallas_call(
        functools.partial(_gla_kernel, valid_len=valid_len, n_chunks=n_chunks),
        grid=(bsz, n_chunks),
        in_specs=[col(2), col(3), col(4),
                  pl.BlockSpec((None, CHUNK, LANES), lambda b, i: (b, i, 0)),
                  pl.BlockSpec((LANES, GLA_KW), const2),
                  pl.BlockSpec((1, GLA_KW), const2),
                  pl.BlockSpec((1, GLA_WIDTH), const2),
                  pl.BlockSpec((None, GLA_KW, LANES), lambda b, i: (b, 0, 0)),
                  pl.BlockSpec(mall.shape, const2),
                  pl.BlockSpec(lv.shape, const2)],
        out_specs=[pl.BlockSpec((None, CHUNK, GLA_WIDTH), lambda b, i: (b, i, 0)),
                   pl.BlockSpec((None, GLA_KW, LANES), lambda b, i: (b, 0, 0))],
        out_shape=[jax.ShapeDtypeStruct((bsz, s, GLA_WIDTH), BF16),
                   jax.ShapeDtypeStruct((bsz, GLA_KW, LANES), F32)],
        scratch_shapes=[pltpu.VMEM((LANES, GLA_KW), F32)],
        compiler_params=_cparams("parallel", "arbitrary"),
        name="gla",
    )(ug, ug, ug, tail, w2pad, b2.reshape(1, GLA_KW), norm_g.reshape(1, GLA_WIDTH),
      s0.reshape(bsz, GLA_KW, LANES), mall, lv)
    return o, st.reshape(bsz, GLA_HEADS, GLA_DK, LANES)


PAGES_PER_STEP = 16
PAGE_ROWS = PAGE * N_FOX_HEADS


def _dec_tables():
    t = np.arange(PAGE)
    after = (t[:, None] > t[None, :]).astype(np.float32)
    spread = (t[:, None] == (np.arange(PAGE_ROWS) // N_FOX_HEADS)[None, :]).astype(np.float32)
    return jnp.asarray(after, BF16), jnp.asarray(spread, BF16)


def _dec_kernel(pt_ref, *refs, n_steps):
    n = PAGES_PER_STEP
    f_refs, k_refs, v_refs = refs[:n], refs[n:2 * n], refs[2 * n:3 * n]
    q_ref, kn_ref, vn_ref, fn_ref, after_ref, spread_ref, o_ref, m_sc, l_sc, acc_sc, carry = refs[3 * n:]
    i = pl.program_id(1)
    scale = HEAD_DIM ** -0.5
    q = q_ref[...]

    @pl.when(i == 0)
    def _():
        s_self = jnp.sum(q.astype(BF16).astype(F32) * kn_ref[...].astype(BF16).astype(F32), axis=-1, keepdims=True) * scale
        m_sc[...] = jnp.broadcast_to(s_self, m_sc.shape)
        l_sc[...] = jnp.ones(l_sc.shape, F32)
        acc_sc[...] = vn_ref[...].astype(BF16).astype(F32)
        carry[...] = fn_ref[...]

    lhs = jnp.concatenate([f_ref[...] for f_ref in f_refs], axis=0)
    suffix = _dot_exact(lhs, after_ref[...])
    total = _dot_exact(lhs, jnp.ones((PAGE, LANES), BF16))
    run = carry[...]
    bias = [None] * n
    for j in reversed(range(n)):
        sl = slice(j * N_FOX_HEADS, (j + 1) * N_FOX_HEADS)
        bias[j] = suffix[sl] + run
        run = run + total[sl]
    carry[...] = run
    bias_x = _dot_exact(jnp.concatenate(bias, axis=0), spread_ref[...])

    q_b = q.astype(BF16)
    valid = lax.broadcasted_iota(jnp.int32, (N_FOX_HEADS, PAGE_ROWS), 1) % N_FOX_HEADS == \
        lax.broadcasted_iota(jnp.int32, (N_FOX_HEADS, PAGE_ROWS), 0)
    scores = []
    for j, k_ref in enumerate(k_refs):
        s = _dot_nt(q_b, k_ref[...].astype(BF16)) * scale + bias_x[j * N_FOX_HEADS:(j + 1) * N_FOX_HEADS]
        scores.append(jnp.where(valid, s, -jnp.inf))
    m_prev = m_sc[...]
    m_cur = scores[0]
    for s in scores[1:]:
        m_cur = jnp.maximum(m_cur, s)
    m_new = jnp.maximum(m_prev, jnp.max(m_cur, axis=1, keepdims=True))
    alpha = jnp.exp(m_prev - m_new)
    m_wide = jnp.concatenate([m_new] * (PAGE_ROWS // LANES), axis=1)
    acc = alpha * acc_sc[...]
    l_part = jnp.zeros((N_FOX_HEADS, PAGE_ROWS), F32)
    for s, v_ref in zip(scores, v_refs):
        p = jnp.exp(s - m_wide)
        l_part = l_part + p
        acc = acc + _dot(p.astype(BF16), v_ref[...].astype(BF16))
    l_new = alpha * l_sc[...] + jnp.sum(l_part, axis=1, keepdims=True)
    m_sc[...] = m_new
    l_sc[...] = l_new
    acc_sc[...] = acc

    @pl.when(i == n_steps - 1)
    def _():
        o_ref[...] = acc / l_new


def _fox_decode(page_table, k_pool, v_pool, f_pool, layer, q, k_new, v_new, logf_new_b, tables):
    bd, n_pages = page_table.shape
    n_steps = n_pages // PAGES_PER_STEP
    depth, n_pool = k_pool.shape[:2]
    k_view = k_pool.reshape(depth, n_pool, PAGE_ROWS, HEAD_DIM)
    v_view = v_pool.reshape(depth, n_pool, PAGE_ROWS, HEAD_DIM)
    f_view = jnp.transpose(f_pool, (0, 1, 3, 2))
    after, spread = tables

    def page_of(b, i, pt, j):
        return pt[b, (n_steps - 1 - i) * PAGES_PER_STEP + j]

    def page_spec(j, rows):
        return pl.BlockSpec((None, None, rows, LANES), lambda b, i, pt: (layer, page_of(b, i, pt, j), 0, 0))

    vec = pl.BlockSpec((None, N_FOX_HEADS, HEAD_DIM), lambda b, i, pt: (b, 0, 0))
    const = lambda a: pl.BlockSpec(a.shape, lambda b, i, pt: (0, 0))
    pages = range(PAGES_PER_STEP)
    grid_spec = pltpu.PrefetchScalarGridSpec(
        num_scalar_prefetch=1,
        grid=(bd, n_steps),
        in_specs=[page_spec(j, N_FOX_HEADS) for j in pages] + [page_spec(j, PAGE_ROWS) for j in pages] * 2
        + [vec, vec, vec, vec, const(after), const(spread)],
        out_specs=vec,
        scratch_shapes=[pltpu.VMEM((N_FOX_HEADS, LANES), F32)] * 4,
    )
    return pl.pallas_call(
        functools.partial(_dec_kernel, n_steps=n_steps),
        grid_spec=grid_spec,
        out_shape=jax.ShapeDtypeStruct((bd, N_FOX_HEADS, HEAD_DIM), F32),
        compiler_params=_cparams("parallel", "arbitrary"),
        name="fox_decode",
    )(page_table, *([f_view] * PAGES_PER_STEP), *([k_view] * PAGES_PER_STEP), *([v_view] * PAGES_PER_STEP),
      q, k_new, v_new, logf_new_b, after, spread)


def _pad_rows(x, rows):
    return jnp.pad(x[:, None, :], ((0, 0), (0, rows - 1), (0, 0)))


def _rows16(x):
    return jnp.pad(x, ((0, SROWS - x.shape[0]), (0, 0)))


def kernel(x_prompt, x_sample, cache_k, cache_v, cache_logf, state_gla, page_table, c_prompt, c_sample,
           ln_in_g, ln_in_b, w_ada, b_ada, w_in, b_f, gm_ln_g, gm_ln_b, gm_ws, gm_bs, gla_w2, gla_b2,
           gla_norm_g, w_o, ln1_g, ln1_b, w_gu, w_down, ln2_g, ln2_b):
    bp, seq, d = x_prompt.shape
    bd = x_sample.shape[0]
    depth = w_ada.shape[0]
    assert d == 4 * GLA_WIDTH and x_sample.shape[1] == 1 and bp + bd <= 16 and bd <= SROWS
    assert w_in.shape[2] == IN_WIDTH
    alpha = (2 * depth) ** 0.25
    m_p = bp * seq

    gla_tables = _gla_tables()
    dec_tables = _dec_tables()
    ext_tables = _ext_tables()

    c16 = jnp.zeros((SROWS, d), F32).at[:bd].set(c_sample).at[bd:bd + bp].set(c_prompt)
    mods = _ada(c16, w_ada, b_ada)

    def mod_p(l, which):
        return mods[l, bd:bd + bp, which * d:(which + 1) * d][:, None, :]

    def mod_s(l, *which):
        return jnp.concatenate([mods[l, :, w * d:(w + 1) * d] for w in which], axis=-1)

    w_in_b = _pack_in(w_in)
    w_o_b = _cast(w_o)
    w_down_b = _cast(w_down)

    xp, hp = _ln_mod(x_prompt, ln_in_g, ln_in_b, mod=(mod_p(0, 1), mod_p(0, 0)), tr=256)
    xs, hs = _ln_mod(x_sample, ln_in_g, ln_in_b,
                     mod=(mods[0, :bd, d:2 * d][:, None, :], mods[0, :bd, 0:d][:, None, :]), tr=1)
    xp, hp = xp.reshape(m_p, d), hp.reshape(m_p, d)
    xs, hs = _rows16(xs[:, 0, :]), _rows16(hs[:, 0, :])

    zero_state = jnp.zeros((bp, GLA_HEADS, GLA_DK, LANES), F32)
    k_all = v_all = None
    fp, sp = [], []
    kd, vd, fd, sd, gd = [], [], [], [], []
    for l in range(depth):
        last = l == depth - 1
        q16, k_all, k16, v_all, vt16, ug, tail, samp = _proj(hp, hs, w_in_b, l, k_all, v_all, depth, tm=512)

        samp = samp[:bd]
        heads3 = lambda a: a.reshape(bd, N_FOX_HEADS, HEAD_DIM)
        q_s = heads3(samp[:, 0:FOX_WIDTH])
        k_s = heads3(samp[:, FOX_WIDTH:2 * FOX_WIDTH])
        v_s = heads3(samp[:, 2 * FOX_WIDTH:3 * FOX_WIDTH])
        ug_s = _pad_rows(samp[:, 6 * TN:11 * TN], CHUNK)
        tail_s = _pad_rows(samp[:, 11 * TN:11 * TN + LANES], CHUNK)
        logf_s, _, _ = _fox_cum(tail_s, b_f[l], ext_tables)
        logf_s = logf_s[:, 0, :]
        fox_s = _fox_decode(page_table, cache_k, cache_v, cache_logf, l, q_s, k_s, v_s,
                            jnp.broadcast_to(logf_s[:, :, None], (bd, N_FOX_HEADS, LANES)), dec_tables)
        gm_s, vn_s = _gmlp(ug_s, gm_ln_g[l], gm_ln_b[l], gm_ws[l], gm_bs[l], emit_vn=True)
        gla_s, st_s = _gla(ug_s, tail_s, gla_w2[l], gla_b2[l], gla_norm_g[l], state_gla[l], gla_tables, valid_len=1)
        mix_s = jnp.concatenate([fox_s.reshape(bd, -1).astype(BF16), gm_s[:, 0, :], gla_s[:, 0, :]], axis=-1)

        tail3 = tail.reshape(bp, seq, LANES)
        ug3 = ug.reshape(bp, seq, UG_WIDTH)
        logf, ext_q, ext_k = _fox_cum(tail3, b_f[l], ext_tables)
        fox_o = _fox_prompt(q16.reshape(bp, seq, FOX_WIDTH), k16.reshape(bp, seq, FOX_WIDTH), vt16, ext_q, ext_k)
        gm_o, _ = _gmlp(ug3, gm_ln_g[l], gm_ln_b[l], gm_ws[l], gm_bs[l], emit_vn=False)
        gla_o, st = _gla(ug3, tail3, gla_w2[l], gla_b2[l], gla_norm_g[l], zero_state, gla_tables)

        x1, h1, x1_s, h1_s = _dense_ln(
            [fox_o.reshape(m_p, -1), gm_o.reshape(m_p, -1), gla_o.reshape(m_p, -1)], _rows16(mix_s), w_o_b, l,
            xp, xs, mod_p(l, 2), ln1_g[l], ln1_b[l], (mod_p(l, 4), mod_p(l, 3)), mod_s(l, 2, 4, 3),
            alpha=alpha, rows_per_batch=seq, tm=512)
        act, act_s = _swiglu(h1, h1_s, w_gu, l, tm=2048, tn=512)
        nxt = None if last else (mod_p(l + 1, 1), mod_p(l + 1, 0))
        nxt_s = mod_s(l, 5, 5, 5) if last else jnp.concatenate([mod_s(l, 5), mod_s(l + 1, 1, 0)], axis=-1)
        xp, hp, xs, hs = _dense_ln([act], act_s, w_down_b, l, x1, x1_s, mod_p(l, 5), ln2_g[l], ln2_b[l], nxt, nxt_s,
                                   alpha=alpha, rows_per_batch=seq, tm=256)

        fp.append(logf)
        sp.append(st)
        kd.append(k_s[:, None])
        vd.append(v_s[:, None])
        fd.append(logf_s[:, None, :])
        sd.append(st_s)
        gd.append(vn_s[:, 0:1, :])

    kv_shape = (depth, bp, seq, N_FOX_HEADS, HEAD_DIM)
    return (xp.reshape(bp, seq, d), xs[:bd, None, :], k_all.reshape(kv_shape), v_all.reshape(kv_shape),
            jnp.stack(fp), jnp.stack(sp), jnp.stack(kd), jnp.stack(vd), jnp.stack(fd), jnp.stack(sd), jnp.stack(gd))
```

```python
import functools
import math

import numpy as np
import jax
import jax.numpy as jnp
from jax import lax
from jax.experimental import pallas as pl
from jax.experimental.pallas import tpu as pltpu

F32 = jnp.float32
BF16 = jnp.bfloat16

LANES = 128
HEAD_DIM = 128
N_FOX_HEADS = 8
FOX_WIDTH = N_FOX_HEADS * HEAD_DIM
GMLP_GROUPS = 4
GMLP_WIDTH = GMLP_GROUPS * LANES
GLA_HEADS = 4
GLA_DK = 64
GLA_KW = GLA_HEADS * GLA_DK
GLA_WIDTH = GLA_HEADS * LANES
GLA_RANK = 16
GLA_TAU = 16.0
CHUNK = 128
PAGE = 128
LN_EPS = 1e-5
LOG2E = math.log2(math.e)
VMEM_LIMIT = 56 * 1024 * 1024
RESIDENT_VMEM_LIMIT = 60 * 1024 * 1024
SROWS = 16

TN = 512
OFF_FF = 3 * FOX_WIDTH
OFF_MU = OFF_FF + N_FOX_HEADS
OFF_GR = OFF_MU + 2 * GMLP_WIDTH + 2 * GLA_KW + GLA_WIDTH
OFF_GG = OFF_GR + GLA_RANK
IN_WIDTH = OFF_GG + GLA_WIDTH
N_PROJ_TILES = 12
UG_WIDTH = 5 * TN


def _cparams(*sem):
    return pltpu.CompilerParams(dimension_semantics=sem, vmem_limit_bytes=VMEM_LIMIT)


def _log_sigmoid(x):
    return jnp.minimum(x, 0.0) - jnp.log1p(jnp.exp(-jnp.abs(x)))


def _silu(x):
    return x * (1.0 / (1.0 + jnp.exp(-x)))


def _split3(x):
    hi = x.astype(BF16)
    r = x - hi.astype(F32)
    mid = r.astype(BF16)
    lo = (r - mid.astype(F32)).astype(BF16)
    return hi, mid, lo


def _dot(a, b):
    return jnp.dot(a, b, preferred_element_type=F32)


def _dot_nt(a, b):
    return lax.dot_general(a, b, (((1,), (1,)), ((), ())), preferred_element_type=F32)


def _dot_exact(x, w):
    hi, mid, lo = _split3(x)
    return _dot(hi, w) + _dot(mid, w) + _dot(lo, w)


def _dot_exact_l(w, x):
    hi, mid, lo = _split3(x)
    return _dot(w, hi) + _dot(w, mid) + _dot(w, lo)


def _ada_kernel(c_ref, w_ref, b_ref, o_ref):
    a = _silu(c_ref[...]).astype(BF16)
    o_ref[...] = _dot(a, w_ref[...].astype(BF16)) + b_ref[...]


def _ada(c16, w_ada, b_ada):
    depth, d, n = w_ada.shape
    tn = 1024
    return pl.pallas_call(
        _ada_kernel,
        grid=(depth, n // tn),
        in_specs=[pl.BlockSpec((16, d), lambda l, j: (0, 0)),
                  pl.BlockSpec((None, d, tn), lambda l, j: (l, 0, j)),
                  pl.BlockSpec((None, 1, tn), lambda l, j: (l, 0, j))],
        out_specs=pl.BlockSpec((None, 16, tn), lambda l, j: (l, 0, j)),
        out_shape=jax.ShapeDtypeStruct((depth, 16, n), F32),
        compiler_params=_cparams("parallel", "parallel"),
        name="ada",
    )(c16, w_ada, b_ada.reshape(depth, 1, n))


def _ln_kernel(*refs, has_res, has_mod, alpha):
    it = iter(refs)
    x = next(it)[...]
    if has_res:
        y = next(it)[...]
        gate = next(it)[...]
        x = alpha * x + gate * y
    g = next(it)[...]
    b = next(it)[...]
    if has_mod:
        sc = next(it)[...]
        sh = next(it)[...]
    xn_ref = next(it)
    mu = jnp.mean(x, axis=-1, keepdims=True)
    xc = x - mu
    var = jnp.mean(xc * xc, axis=-1, keepdims=True)
    xn = xc * lax.rsqrt(var + LN_EPS) * g + b
    xn_ref[...] = xn
    if has_mod:
        h_ref = next(it)
        h_ref[...] = (xn * (1.0 + sc) + sh).astype(BF16)


def _ln_mod(x, g, b, *, res=None, mod=None, alpha=1.0, tr):
    bsz, s, d = x.shape
    tr = min(tr, s)
    row = pl.BlockSpec((None, tr, d), lambda i, j: (i, j, 0))
    per_b = pl.BlockSpec((None, 1, d), lambda i, j: (i, 0, 0))
    vec = pl.BlockSpec((1, d), lambda i, j: (0, 0))
    args, specs = [x], [row]
    if res is not None:
        y, gate = res
        args += [y, gate]
        specs += [row, per_b]
    args += [g.reshape(1, d), b.reshape(1, d)]
    specs += [vec, vec]
    out_shape = [jax.ShapeDtypeStruct((bsz, s, d), F32)]
    out_specs = [row]
    if mod is not None:
        sc, sh = mod
        args += [sc, sh]
        specs += [per_b, per_b]
        out_shape.append(jax.ShapeDtypeStruct((bsz, s, d), BF16))
        out_specs.append(row)
    out = pl.pallas_call(
        functools.partial(_ln_kernel, has_res=res is not None, has_mod=mod is not None, alpha=alpha),
        grid=(bsz, s // tr),
        in_specs=specs, out_specs=out_specs, out_shape=out_shape,
        compiler_params=_cparams("parallel", "parallel"),
        name="ln_mod",
    )(*args)
    return out if mod is not None else (out[0], None)


PACK_EXTRA = 32


def _pack_in_kernel(wm_ref, wn_ref, wg_ref, o_ref):
    j = pl.program_id(1)
    k_dim = wm_ref.shape[1]

    @pl.when(j < 6)
    def _():
        o_ref[...] = wm_ref[...].T.astype(BF16)

    def shifted(rows):
        return jnp.concatenate([wm_ref[rows:TN, :], wn_ref[0:rows, :]], axis=0).T.astype(BF16)

    @pl.when((j >= 6) & (j < 10))
    def _():
        o_ref[...] = shifted(OFF_MU - OFF_FF)

    @pl.when(j == 10)
    def _():
        o_ref[...] = shifted(OFF_GG - 10 * TN)

    @pl.when(j == N_PROJ_TILES - 1)
    def _():
        lo = OFF_GR - (OFF_GR // PACK_EXTRA) * PACK_EXTRA
        t = jnp.concatenate([wn_ref[0:N_FOX_HEADS, :], wg_ref[lo:lo + GLA_RANK, :],
                             jnp.zeros((TN - N_FOX_HEADS - GLA_RANK, k_dim), F32)], axis=0)
        o_ref[...] = t.T.astype(BF16)


def _pack_in(w_in):
    depth, k, _ = w_in.shape
    w_t = jnp.transpose(w_in, (0, 2, 1))
    per = TN // PACK_EXTRA
    nb = lambda j: jnp.where((j >= 6) & (j < N_PROJ_TILES - 1), per * (j + 1), OFF_FF // PACK_EXTRA)
    return pl.pallas_call(
        _pack_in_kernel,
        grid=(depth, N_PROJ_TILES),
        in_specs=[pl.BlockSpec((None, TN, k), lambda l, j: (l, jnp.minimum(j, 10), 0)),
                  pl.BlockSpec((None, PACK_EXTRA, k), lambda l, j: (l, nb(j), 0)),
                  pl.BlockSpec((None, PACK_EXTRA, k), lambda l, j: (l, OFF_GR // PACK_EXTRA, 0))],
        out_specs=pl.BlockSpec((None, k, TN), lambda l, j: (l, 0, j)),
        out_shape=jax.ShapeDtypeStruct((depth, k, N_PROJ_TILES * TN), BF16),
        compiler_params=_cparams("parallel", "parallel"),
        name="pack_in",
    )(w_t, w_t, w_t)


def _cast_kernel(w_ref, o_ref):
    o_ref[...] = w_ref[...].astype(BF16)


def _cast(w):
    depth, k, n = w.shape
    blk = pl.BlockSpec((None, k, TN), lambda l, j: (l, 0, j))
    return pl.pallas_call(
        _cast_kernel, grid=(depth, n // TN), in_specs=[blk], out_specs=blk,
        out_shape=jax.ShapeDtypeStruct(w.shape, BF16),
        compiler_params=_cparams("parallel", "parallel"), name="cast",
    )(w)


def _proj_kernel(*refs, aliased, q_scale, tm):
    x_ref, xs_ref, w_ref = refs[:3]
    q_ref, k32_ref, k16_ref, v32_ref, vt_ref, ug_ref, t_ref, s_ref = refs[3 + (2 if aliased else 0):]
    x = x_ref[...]
    q_ref[...] = (_dot(x, w_ref[:, 0:FOX_WIDTH]) * q_scale).astype(BF16)
    k = _dot(x, w_ref[:, FOX_WIDTH:2 * FOX_WIDTH])
    k16_ref[...] = k.astype(BF16)
    v = _dot(x, w_ref[:, 2 * FOX_WIDTH:3 * FOX_WIDTH])
    vt_ref[...] = v.T.astype(BF16)
    for h in range(N_FOX_HEADS):
        cols = slice(h * HEAD_DIM, (h + 1) * HEAD_DIM)
        k32_ref[pl.ds(h, tm, stride=N_FOX_HEADS), :] = k[:, cols]
        v32_ref[pl.ds(h, tm, stride=N_FOX_HEADS), :] = v[:, cols]
    ug_ref[...] = _dot(x, w_ref[:, 6 * TN:11 * TN])
    t_ref[...] = _dot(x, w_ref[:, 11 * TN:11 * TN + LANES])

    @pl.when(pl.program_id(0) == 0)
    def _():
        s_ref[...] = _dot(xs_ref[...], w_ref[...])


def _proj(x, xs, w_in, layer, k_prev, v_prev, depth, *, tm):
    m, k = x.shape
    tm = min(tm, m)
    n_all = w_in.shape[2]
    in_specs = [pl.BlockSpec((tm, k), lambda i: (i, 0)),
                pl.BlockSpec((SROWS, k), lambda i: (0, 0)),
                pl.BlockSpec((None, k, n_all), lambda i: (layer, 0, 0), pipeline_mode=pl.Buffered(1))]
    args = [x, xs, w_in]
    aliases = {}
    if k_prev is not None:
        in_specs += [pl.BlockSpec(memory_space=pl.ANY)] * 2
        args += [k_prev, v_prev]
        aliases = {3: 1, 4: 3}
    rows = pl.BlockSpec((tm, FOX_WIDTH), lambda i: (i, 0))
    pairs = pl.BlockSpec((None, tm * N_FOX_HEADS, HEAD_DIM), lambda i: (layer, i, 0))
    out_shape = [jax.ShapeDtypeStruct((m, FOX_WIDTH), BF16),
                 jax.ShapeDtypeStruct((depth, m * N_FOX_HEADS, HEAD_DIM), F32),
                 jax.ShapeDtypeStruct((m, FOX_WIDTH), BF16),
                 jax.ShapeDtypeStruct((depth, m * N_FOX_HEADS, HEAD_DIM), F32),
                 jax.ShapeDtypeStruct((FOX_WIDTH, m), BF16),
                 jax.ShapeDtypeStruct((m, UG_WIDTH), F32),
                 jax.ShapeDtypeStruct((m, LANES), F32),
                 jax.ShapeDtypeStruct((SROWS, n_all), F32)]
    out_specs = [rows, pairs, rows, pairs,
                 pl.BlockSpec((FOX_WIDTH, tm), lambda i: (0, i)),
                 pl.BlockSpec((tm, UG_WIDTH), lambda i: (i, 0)),
                 pl.BlockSpec((tm, LANES), lambda i: (i, 0)),
                 pl.BlockSpec((SROWS, n_all), lambda i: (0, 0))]
    return pl.pallas_call(
        functools.partial(_proj_kernel, aliased=k_prev is not None, q_scale=HEAD_DIM ** -0.5 * LOG2E, tm=tm),
        grid=(m // tm,),
        in_specs=in_specs, out_specs=out_specs, out_shape=out_shape,
        input_output_aliases=aliases,
        compiler_params=pltpu.CompilerParams(dimension_semantics=("arbitrary",), vmem_limit_bytes=RESIDENT_VMEM_LIMIT),
        name="proj",
    )(*args)


SWIGLU_SUB = 1024


def _swiglu_kernel(x_ref, xs_ref, wg_ref, wu_ref, o_ref, os_ref, wbg, wbu):
    def result(x):
        return (_silu(_dot(x, wbg[...])) * _dot(x, wbu[...])).astype(BF16)

    @pl.when(pl.program_id(1) == 0)
    def _():
        wbg[...] = wg_ref[...].astype(BF16)
        wbu[...] = wu_ref[...].astype(BF16)
        os_ref[...] = result(xs_ref[...])

    sub = min(SWIGLU_SUB, x_ref.shape[0])
    for r0 in range(0, x_ref.shape[0], sub):
        o_ref[r0:r0 + sub, :] = result(x_ref[r0:r0 + sub, :])


def _swiglu(x, x_sample, w_gu, layer, *, tm, tn):
    m, k = x.shape
    tm = min(tm, m)
    n = w_gu.shape[2] // 2
    up0 = n // tn
    return pl.pallas_call(
        _swiglu_kernel,
        grid=(n // tn, m // tm),
        in_specs=[pl.BlockSpec((tm, k), lambda j, i: (i, 0)),
                  pl.BlockSpec((SROWS, k), lambda j, i: (0, 0)),
                  pl.BlockSpec((None, k, tn), lambda j, i: (layer, 0, j)),
                  pl.BlockSpec((None, k, tn), lambda j, i: (layer, 0, up0 + j))],
        out_specs=[pl.BlockSpec((tm, tn), lambda j, i: (i, j)), pl.BlockSpec((SROWS, tn), lambda j, i: (0, j))],
        out_shape=[jax.ShapeDtypeStruct((m, n), BF16), jax.ShapeDtypeStruct((SROWS, n), BF16)],
        scratch_shapes=[pltpu.VMEM((k, tn), BF16)] * 2,
        compiler_params=_cparams("arbitrary", "arbitrary"),
        name="swiglu",
    )(x, x_sample, w_gu, w_gu)


DENSE_LN_SUB = 128


def _layer_norm(z, g, b):
    mu = jnp.mean(z, axis=-1, keepdims=True)
    zc = z - mu
    var = jnp.mean(zc * zc, axis=-1, keepdims=True)
    return zc * lax.rsqrt(var + LN_EPS) * g + b


def _dense_ln_kernel(*refs, n_lhs, has_mod, alpha):
    it = iter(refs)
    x_refs = [next(it) for _ in range(n_lhs)]
    xs_ref, w_ref, res_ref, gate_ref, g_ref, b_ref = [next(it) for _ in range(6)]
    sc_ref, sh_ref = (next(it), next(it)) if has_mod else (None, None)
    res_s_ref, mod_s_ref = next(it), next(it)
    xn_ref = next(it)
    h_ref = next(it) if has_mod else None
    xn_s_ref = next(it)
    h_s_ref = next(it) if has_mod else None
    d = xn_ref.shape[-1]

    def finish(lhs, res, gate, sc, sh):
        xn = _layer_norm(alpha * res + gate * _dot(lhs, w_ref[...]), g_ref[...], b_ref[...])
        return xn, ((xn * (1.0 + sc) + sh).astype(BF16) if has_mod else None)

    @pl.when(pl.program_id(0) == 0)
    def _():
        mod_s = mod_s_ref[...]
        xn, h = finish(xs_ref[...], res_s_ref[...], mod_s[:, 0:d], mod_s[:, d:2 * d], mod_s[:, 2 * d:3 * d])
        xn_s_ref[...] = xn
        if has_mod:
            h_s_ref[...] = h

    tm = xn_ref.shape[0]
    sub = min(tm, DENSE_LN_SUB)
    for r0 in range(0, tm, sub):
        rows = slice(r0, r0 + sub)
        parts = [x[rows, :] for x in x_refs]
        lhs = parts[0] if n_lhs == 1 else jnp.concatenate(parts, axis=1)
        xn, h = finish(lhs, res_ref[rows, :], gate_ref[...], sc_ref[...] if has_mod else None,
                       sh_ref[...] if has_mod else None)
        xn_ref[rows, :] = xn
        if has_mod:
            h_ref[rows, :] = h


def _dense_ln(xs_list, x_sample, w, layer, res, res_s, gate, ln_g, ln_b, mod, mod_s, *, alpha, rows_per_batch, tm):
    m = xs_list[0].shape[0]
    tm = min(tm, rows_per_batch)
    k, d = w.shape[1], w.shape[2]
    assert sum(x.shape[1] for x in xs_list) == k and x_sample.shape == (SROWS, k)
    steps_per_batch = rows_per_batch // tm
    has_mod = mod is not None
    row = pl.BlockSpec((tm, d), lambda i: (i, 0))
    per_b = pl.BlockSpec((None, 1, d), lambda i: (i // steps_per_batch, 0, 0))
    vec = pl.BlockSpec((1, d), lambda i: (0, 0))
    srow = pl.BlockSpec((SROWS, d), lambda i: (0, 0))
    in_specs = [pl.BlockSpec((tm, x.shape[1]), lambda i: (i, 0)) for x in xs_list]
    in_specs += [pl.BlockSpec((SROWS, k), lambda i: (0, 0)),
                 pl.BlockSpec((None, k, d), lambda i: (layer, 0, 0), pipeline_mode=pl.Buffered(1)),
                 row, per_b, vec, vec]
    args = list(xs_list) + [x_sample, w, res, gate, ln_g.reshape(1, d), ln_b.reshape(1, d)]
    if has_mod:
        in_specs += [per_b, per_b]
        args += list(mod)
    in_specs += [srow, pl.BlockSpec((SROWS, 3 * d), lambda i: (0, 0))]
    args += [res_s, mod_s]
    out_specs, out_shape = [row], [jax.ShapeDtypeStruct((m, d), F32)]
    if has_mod:
        out_specs.append(row)
        out_shape.append(jax.ShapeDtypeStruct((m, d), BF16))
    out_specs.append(srow)
    out_shape.append(jax.ShapeDtypeStruct((SROWS, d), F32))
    if has_mod:
        out_specs.append(srow)
        out_shape.append(jax.ShapeDtypeStruct((SROWS, d), BF16))
    out = pl.pallas_call(
        functools.partial(_dense_ln_kernel, n_lhs=len(xs_list), has_mod=has_mod, alpha=alpha),
        grid=(m // tm,),
        in_specs=in_specs, out_specs=out_specs, out_shape=out_shape,
        compiler_params=pltpu.CompilerParams(dimension_semantics=("arbitrary",), vmem_limit_bytes=RESIDENT_VMEM_LIMIT),
        name="dense_ln",
    )(*args)
    return out if has_mod else (out[0], None, out[1], None)


EXT_PER_HEAD = 6


def _ext_tables():
    pq = np.zeros((3 * LANES, LANES), np.float32)
    pk = np.zeros((3 * LANES, LANES), np.float32)
    oq = np.zeros((1, LANES), np.float32)
    ok = np.zeros((1, LANES), np.float32)
    for h in range(N_FOX_HEADS):
        for part in range(3):
            pq[part * LANES + h, EXT_PER_HEAD * h + part] = 1.0
            pk[part * LANES + h, EXT_PER_HEAD * h + 3 + part] = -1.0
            oq[0, EXT_PER_HEAD * h + 3 + part] = 1.0
            ok[0, EXT_PER_HEAD * h + part] = 1.0
    return jnp.asarray(pq, BF16), jnp.asarray(pk, BF16), jnp.asarray(oq), jnp.asarray(ok)


def _fox_cum_kernel(t_ref, bf_ref, pq_ref, pk_ref, oq_ref, ok_ref, lf_ref, eq_ref, ek_ref, carry, *, tb):
    @pl.when(pl.program_id(1) == 0)
    def _():
        carry[...] = jnp.zeros(carry.shape, F32)

    lf = _log_sigmoid(t_ref[...] + bf_ref[...])
    lf_ref[...] = lf[:, 0:N_FOX_HEADS]
    r = lax.broadcasted_iota(jnp.int32, (tb, tb), 0)
    c = lax.broadcasted_iota(jnp.int32, (tb, tb), 1)
    lower = (c <= r).astype(BF16)
    cum = _dot_exact_l(lower, lf) + carry[0:1, :]
    carry[...] = jnp.broadcast_to(cum[tb - 1:tb, :], carry.shape)
    hi, mid, lo = _split3(cum * LOG2E)
    cat = jnp.concatenate([hi, mid, lo], axis=1)
    eq_ref[...] = (_dot(cat, pq_ref[...]) + oq_ref[...]).astype(BF16)
    ek_ref[...] = (_dot(cat, pk_ref[...]) + ok_ref[...]).astype(BF16)


def _fox_cum(tail, b_f, tables):
    bsz, s, _ = tail.shape
    tb = min(512, s)
    pq, pk, oq, ok = tables
    bf = jnp.zeros((1, LANES), F32).at[0, :N_FOX_HEADS].set(b_f)
    blk = pl.BlockSpec((None, tb, LANES), lambda b, i: (b, i, 0))
    const = lambda shape: pl.BlockSpec(shape, lambda b, i: (0, 0))
    return pl.pallas_call(
        functools.partial(_fox_cum_kernel, tb=tb),
        grid=(bsz, s // tb),
        in_specs=[blk, const((1, LANES)), const(pq.shape), const(pk.shape), const((1, LANES)), const((1, LANES))],
        out_specs=[pl.BlockSpec((None, tb, N_FOX_HEADS), lambda b, i: (b, i, 0)), blk, blk],
        out_shape=[jax.ShapeDtypeStruct((bsz, s, N_FOX_HEADS), F32),
                   jax.ShapeDtypeStruct((bsz, s, LANES), BF16),
                   jax.ShapeDtypeStruct((bsz, s, LANES), BF16)],
        scratch_shapes=[pltpu.VMEM((8, LANES), F32)],
        compiler_params=_cparams("parallel", "arbitrary"),
        name="fox_cum",
    )(tail, bf, pq, pk, oq, ok)


HEADS_PER_STEP = 8
ATTN_TILE = 1024


def _fox_kernel(qi_ref, kj_ref, q_ref, k_ref, vt_ref, eq_ref, ek_ref, o_ref, m_sc, l_sc, acc_sc, qx_sc, *, tq, tk):
    grp = pl.program_id(1)
    p = pl.program_id(2)
    qi = qi_ref[p]
    kj = kj_ref[p]

    @pl.when(kj == 0)
    def _():
        m_sc[...] = jnp.full(m_sc.shape, -jnp.inf, F32)
        l_sc[...] = jnp.zeros(l_sc.shape, F32)
        acc_sc[...] = jnp.zeros(acc_sc.shape, F32)
        lane = lax.broadcasted_iota(jnp.int32, (tq, LANES), 1)
        eq = eq_ref[...]
        for hh in range(HEADS_PER_STEP):
            lo = (grp * HEADS_PER_STEP + hh) * EXT_PER_HEAD
            own = (lane >= lo) & (lane < lo + EXT_PER_HEAD)
            qx_sc[hh] = jnp.concatenate([q_ref[:, hh * LANES:(hh + 1) * LANES],
                                         jnp.where(own, eq, jnp.zeros_like(eq))], axis=1)

    def step(diagonal):
        ek = ek_ref[...]
        for hh in range(HEADS_PER_STEP):
            cols = slice(hh * LANES, (hh + 1) * LANES)
            kx = jnp.concatenate([k_ref[:, cols], ek], axis=1)
            s = _dot_nt(kx, qx_sc[hh])
            if diagonal:
                key = lax.broadcasted_iota(jnp.int32, (tk, tq), 0)
                qry = lax.broadcasted_iota(jnp.int32, (tk, tq), 1)
                s = jnp.where(key <= qry, s, -jnp.inf)
            m_prev = m_sc[hh][0:1, :]
            m_new = jnp.maximum(m_prev, jnp.max(s, axis=0, keepdims=True))
            alpha = jnp.exp2(m_prev - m_new)
            pr = jnp.exp2(s - m_new)
            l_new = alpha * l_sc[hh][0:1, :] + jnp.sum(pr, axis=0, keepdims=True)
            acc = alpha * acc_sc[hh] + _dot(vt_ref[cols, :], pr.astype(BF16))
            m_sc[hh] = jnp.broadcast_to(m_new, (8, tq))
            l_sc[hh] = jnp.broadcast_to(l_new, (8, tq))
            acc_sc[hh] = acc
            if diagonal:
                o_ref[:, cols] = (acc / l_new).T.astype(o_ref.dtype)

    @pl.when(kj == qi)
    def _():
        step(True)

    @pl.when(kj != qi)
    def _():
        step(False)


def _fox_prompt(q, k, vt, ext_q, ext_k):
    bsz, s, _ = q.shape
    tq = tk = min(ATTN_TILE, s)
    nq = s // tq
    width = HEADS_PER_STEP * LANES
    pairs = [(i, j) for i in range(nq) for j in range(i + 1)]
    qi_tbl = jnp.asarray([pr[0] for pr in pairs], jnp.int32)
    kj_tbl = jnp.asarray([pr[1] for pr in pairs], jnp.int32)
    grid_spec = pltpu.PrefetchScalarGridSpec(
        num_scalar_prefetch=2,
        grid=(bsz, N_FOX_HEADS // HEADS_PER_STEP, len(pairs)),
        in_specs=[pl.BlockSpec((None, tq, width), lambda b, g, p, qi, kj: (b, qi[p], g)),
                  pl.BlockSpec((None, tk, width), lambda b, g, p, qi, kj: (b, kj[p], g)),
                  pl.BlockSpec((width, tk), lambda b, g, p, qi, kj: (g, b * nq + kj[p])),
                  pl.BlockSpec((None, tq, LANES), lambda b, g, p, qi, kj: (b, qi[p], 0)),
                  pl.BlockSpec((None, tk, LANES), lambda b, g, p, qi, kj: (b, kj[p], 0))],
        out_specs=pl.BlockSpec((None, tq, width), lambda b, g, p, qi, kj: (b, qi[p], g)),
        scratch_shapes=[pltpu.VMEM((HEADS_PER_STEP, 8, tq), F32),
                        pltpu.VMEM((HEADS_PER_STEP, 8, tq), F32),
                        pltpu.VMEM((HEADS_PER_STEP, LANES, tq), F32),
                        pltpu.VMEM((HEADS_PER_STEP, tq, 2 * LANES), BF16)],
    )
    return pl.pallas_call(
        functools.partial(_fox_kernel, tq=tq, tk=tk),
        grid_spec=grid_spec,
        out_shape=jax.ShapeDtypeStruct((bsz, s, FOX_WIDTH), BF16),
        compiler_params=_cparams("parallel", "parallel", "arbitrary"),
        name="fox_prompt",
    )(qi_tbl, kj_tbl, q, k, vt, ext_q, ext_k)


GMLP_CHUNKS_PER_STEP = 8


def _gmlp_kernel(uv_ref, g_ref, b_ref, ws_ref, bs_ref, o_ref, *vn_ref):
    r = lax.broadcasted_iota(jnp.int32, (CHUNK, CHUNK), 0)
    c = lax.broadcasted_iota(jnp.int32, (CHUNK, CHUNK), 1)
    for grp in range(GMLP_GROUPS):
        lo = grp * LANES
        w = jnp.where(c <= r, ws_ref[grp], 0.0).astype(BF16)
        for r0 in range(0, uv_ref.shape[0], CHUNK):
            rows = slice(r0, r0 + CHUNK)
            u = jax.nn.gelu(uv_ref[rows, lo:lo + LANES])
            v = jax.nn.gelu(uv_ref[rows, GMLP_WIDTH + lo:GMLP_WIDTH + lo + LANES])
            mu = jnp.mean(v, axis=-1, keepdims=True)
            vc = v - mu
            var = jnp.mean(vc * vc, axis=-1, keepdims=True)
            vn = vc * lax.rsqrt(var + LN_EPS) * g_ref[:, lo:lo + LANES] + b_ref[:, lo:lo + LANES]
            mixed = _dot(w, vn.astype(BF16)) + bs_ref[grp]
            o_ref[rows, lo:lo + LANES] = (u * mixed).astype(o_ref.dtype)
            if vn_ref:
                vn_ref[0][rows, lo:lo + LANES] = vn


def _gmlp(ug, ln_g, ln_b, ws, bs, *, emit_vn):
    bsz, s, _ = ug.shape
    rows = CHUNK * min(GMLP_CHUNKS_PER_STEP, s // CHUNK)
    blk = pl.BlockSpec((None, rows, GMLP_WIDTH), lambda b, i: (b, i, 0))
    vec = pl.BlockSpec((1, GMLP_WIDTH), lambda b, i: (0, 0))
    cube = pl.BlockSpec((GMLP_GROUPS, CHUNK, CHUNK), lambda b, i: (0, 0, 0))
    out_shape = [jax.ShapeDtypeStruct((bsz, s, GMLP_WIDTH), BF16)]
    if emit_vn:
        out_shape.append(jax.ShapeDtypeStruct((bsz, s, GMLP_WIDTH), F32))
    bs_b = jnp.broadcast_to(bs[:, :, None], (GMLP_GROUPS, CHUNK, LANES))
    out = pl.pallas_call(
        _gmlp_kernel,
        grid=(bsz, s // rows),
        in_specs=[pl.BlockSpec((None, rows, 2 * GMLP_WIDTH), lambda b, i: (b, i, 0)), vec, vec, cube, cube],
        out_specs=[blk] * len(out_shape),
        out_shape=out_shape,
        compiler_params=_cparams("parallel", "parallel"),
        name="gmlp",
    )(ug, ln_g.reshape(1, GMLP_WIDTH), ln_b.reshape(1, GMLP_WIDTH), ws, bs_b)
    return out if emit_vn else (out[0], None)


_GLA_LEVELS = (64, 32, 16, 8, 4, 2, 1)
_N_LEVEL = len(_GLA_LEVELS)


def _gla_tables():
    x = np.arange(CHUNK)[:, None]
    j = np.arange(CHUNK)[None, :]
    groups = []
    lv = np.full((CHUNK, CHUNK), -1, np.int32)
    for i, half in enumerate(_GLA_LEVELS):
        anchor = (x // (2 * half)) * (2 * half) + half - 1
        groups.append((j > np.minimum(x, anchor)) & (j <= np.maximum(x, anchor)))
        same = (x // (2 * half)) == (j // (2 * half))
        upper_t = (x % (2 * half)) >= half
        lower_s = (j % (2 * half)) < half
        lv[same & upper_t & lower_s] = i
    lv[np.arange(CHUNK), np.arange(CHUNK)] = _N_LEVEL
    groups.append(j <= x)
    groups.append(j > x)
    mall = np.concatenate(groups, axis=0).astype(np.float32)
    return jnp.asarray(mall, BF16), jnp.asarray(np.tile(lv, (GLA_HEADS, 1)))


def _gla_kernel(qk_ref, v_ref, g_ref, t_ref, w2_ref, b2_ref, ng_ref, s0_ref, mall_ref, lv_ref, o_ref, st_ref, state,
                *, valid_len, n_chunks):
    ci = pl.program_id(1)

    @pl.when(ci == 0)
    def _():
        state[...] = s0_ref[...].T

    qk = qk_ref[...]
    q = qk[:, 0:GLA_KW] * (GLA_DK ** -0.5)
    k = qk[:, GLA_KW:2 * GLA_KW]
    v = v_ref[...]
    gate = g_ref[...]
    la = _log_sigmoid(_dot(t_ref[...].astype(BF16), w2_ref[...].astype(BF16)) + b2_ref[...]) * (1.0 / GLA_TAU)
    if valid_len < CHUNK:
        live = lax.broadcasted_iota(jnp.int32, (CHUNK, GLA_KW), 0) < valid_len
        la = jnp.where(live, la, 0.0)
        k = jnp.where(live, k, 0.0)

    la_hi = la.astype(BF16)
    la_lo = (la - la_hi.astype(F32)).astype(BF16)
    single = valid_len == 1
    mall = mall_ref[_N_LEVEL * CHUNK:, :] if single else mall_ref[...]
    w = jnp.exp(_dot(mall, la_hi) + _dot(mall, la_lo))

    lane_head = lax.broadcasted_iota(jnp.int32, (CHUNK, GLA_KW), 1) // GLA_DK
    heads = [lane_head == h for h in range(GLA_HEADS)]

    q_stack = jnp.concatenate([jnp.where(hm, q, 0.0) for hm in heads], axis=0)
    n_lv = 0 if single else _N_LEVEL

    def level_w(i):
        return w[i * CHUNK:(i + 1) * CHUNK]

    if single:
        a = _dot_nt(q_stack.astype(BF16), k.astype(BF16))
    else:
        lv = lv_ref[...]
        a = jnp.zeros((GLA_HEADS * CHUNK, CHUNK), F32)
        for i in range(_N_LEVEL):
            wi = level_w(i)
            qi = q_stack * jnp.concatenate([wi] * GLA_HEADS, axis=0)
            a = jnp.where(lv == i, _dot_nt(qi.astype(BF16), (k * wi).astype(BF16)), a)
        a = jnp.where(lv == _N_LEVEL, _dot_nt(q_stack.astype(BF16), k.astype(BF16)), a)

    w_pre = level_w(n_lv)
    w_suf = level_w(n_lv + 1)
    st = state[...]
    q_pre = q_stack * jnp.concatenate([w_pre] * GLA_HEADS, axis=0)
    o_inter = _dot_nt(q_pre.astype(BF16), st.astype(BF16))
    v_b = v.astype(BF16)
    for h in range(GLA_HEADS):
        rows = slice(h * CHUNK, (h + 1) * CHUNK)
        cols = slice(h * LANES, (h + 1) * LANES)
        o = _dot(a[rows].astype(BF16), v_b[:, cols]) + o_inter[rows]
        o = o * lax.rsqrt(jnp.mean(o * o, axis=-1, keepdims=True) + LN_EPS)
        o_ref[:, cols] = (o * ng_ref[:, cols] * _silu(gate[:, cols])).astype(o_ref.dtype)

    kvt = _dot(v.T.astype(BF16), (k * w_suf).astype(BF16))
    upd = jnp.zeros((LANES, GLA_KW), F32)
    for h in range(GLA_HEADS):
        upd = upd + jnp.where(heads[h], kvt[h * LANES:(h + 1) * LANES], 0.0)
    st_new = st * w_pre[CHUNK - 1:CHUNK, :] + upd
    state[...] = st_new

    @pl.when(ci == n_chunks - 1)
    def _():
        st_ref[...] = st_new.T


def _gla(ug, tail, w2, b2, norm_g, s0, tables, *, valid_len=CHUNK):
    bsz, s, _ = ug.shape
    n_chunks = s // CHUNK
    mall, lv = tables
    w2pad = jnp.zeros((LANES, GLA_KW), F32).at[N_FOX_HEADS:N_FOX_HEADS + GLA_RANK].set(w2)
    const2 = lambda b, i: (0, 0)
    col = lambda c: pl.BlockSpec((None, CHUNK, TN), lambda b, i: (b, i, c))
    o, st = pl.pallas_call(
        functools.partial(_gla_kernel, valid_len=valid_len, n_chunks=n_chunks),
        grid=(bsz, n_chunks),
        in_specs=[col(2), col(3), col(4),
                  pl.BlockSpec((None, CHUNK, LANES), lambda b, i: (b, i, 0)),
                  pl.BlockSpec((LANES, GLA_KW), const2),
                  pl.BlockSpec((1, GLA_KW), const2),
                  pl.BlockSpec((1, GLA_WIDTH), const2),
                  pl.BlockSpec((None, GLA_KW, LANES), lambda b, i: (b, 0, 0)),
                  pl.BlockSpec(mall.shape, const2),
                  pl.BlockSpec(lv.shape, const2)],
        out_specs=[pl.BlockSpec((None, CHUNK, GLA_WIDTH), lambda b, i: (b, i, 0)),
                   pl.BlockSpec((None, GLA_KW, LANES), lambda b, i: (b, 0, 0))],
        out_shape=[jax.ShapeDtypeStruct((bsz, s, GLA_WIDTH), BF16),
                   jax.ShapeDtypeStruct((bsz, GLA_KW, LANES), F32)],
        scratch_shapes=[pltpu.VMEM((LANES, GLA_KW), F32)],
        compiler_params=_cparams("parallel", "arbitrary"),
        name="gla",
    )(ug, ug, ug, tail, w2pad, b2.reshape(1, GLA_KW), norm_g.reshape(1, GLA_WIDTH),
      s0.reshape(bsz, GLA_KW, LANES), mall, lv)
    return o, st.reshape(bsz, GLA_HEADS, GLA_DK, LANES)


PAGES_PER_STEP = 16
PAGE_ROWS = PAGE * N_FOX_HEADS


def _dec_tables():
    t = np.arange(PAGE)
    after = (t[:, None] > t[None, :]).astype(np.float32)
    spread = (t[:, None] == (np.arange(PAGE_ROWS) // N_FOX_HEADS)[None, :]).astype(np.float32)
    return jnp.asarray(after, BF16), jnp.asarray(spread, BF16)


def _dec_kernel(pt_ref, *refs, n_steps):
    n = PAGES_PER_STEP
    f_refs, k_refs, v_refs = refs[:n], refs[n:2 * n], refs[2 * n:3 * n]
    q_ref, kn_ref, vn_ref, fn_ref, after_ref, spread_ref, o_ref, m_sc, l_sc, acc_sc, carry = refs[3 * n:]
    i = pl.program_id(1)
    scale = HEAD_DIM ** -0.5
    q = q_ref[...]

    @pl.when(i == 0)
    def _():
        s_self = jnp.sum(q.astype(BF16).astype(F32) * kn_ref[...].astype(BF16).astype(F32), axis=-1, keepdims=True) * scale
        m_sc[...] = jnp.broadcast_to(s_self, m_sc.shape)
        l_sc[...] = jnp.ones(l_sc.shape, F32)
        acc_sc[...] = vn_ref[...].astype(BF16).astype(F32)
        carry[...] = fn_ref[...]

    lhs = jnp.concatenate([f_ref[...] for f_ref in f_refs], axis=0)
    suffix = _dot_exact(lhs, after_ref[...])
    total = _dot_exact(lhs, jnp.ones((PAGE, LANES), BF16))
    run = carry[...]
    bias = [None] * n
    for j in reversed(range(n)):
        sl = slice(j * N_FOX_HEADS, (j + 1) * N_FOX_HEADS)
        bias[j] = suffix[sl] + run
        run = run + total[sl]
    carry[...] = run
    bias_x = _dot_exact(jnp.concatenate(bias, axis=0), spread_ref[...])

    q_b = q.astype(BF16)
    valid = lax.broadcasted_iota(jnp.int32, (N_FOX_HEADS, PAGE_ROWS), 1) % N_FOX_HEADS == \
        lax.broadcasted_iota(jnp.int32, (N_FOX_HEADS, PAGE_ROWS), 0)
    scores = []
    for j, k_ref in enumerate(k_refs):
        s = _dot_nt(q_b, k_ref[...].astype(BF16)) * scale + bias_x[j * N_FOX_HEADS:(j + 1) * N_FOX_HEADS]
        scores.append(jnp.where(valid, s, -jnp.inf))
    m_prev = m_sc[...]
    m_cur = scores[0]
    for s in scores[1:]:
        m_cur = jnp.maximum(m_cur, s)
    m_new = jnp.maximum(m_prev, jnp.max(m_cur, axis=1, keepdims=True))
    alpha = jnp.exp(m_prev - m_new)
    m_wide = jnp.concatenate([m_new] * (PAGE_ROWS // LANES), axis=1)
    acc = alpha * acc_sc[...]
    l_part = jnp.zeros((N_FOX_HEADS, PAGE_ROWS), F32)
    for s, v_ref in zip(scores, v_refs):
        p = jnp.exp(s - m_wide)
        l_part = l_part + p
        acc = acc + _dot(p.astype(BF16), v_ref[...].astype(BF16))
    l_new = alpha * l_sc[...] + jnp.sum(l_part, axis=1, keepdims=True)
    m_sc[...] = m_new
    l_sc[...] = l_new
    acc_sc[...] = acc

    @pl.when(i == n_steps - 1)
    def _():
        o_ref[...] = acc / l_new


def _fox_decode(page_table, k_pool, v_pool, f_pool, layer, q, k_new, v_new, logf_new_b, tables):
    bd, n_pages = page_table.shape
    n_steps = n_pages // PAGES_PER_STEP
    depth, n_pool = k_pool.shape[:2]
    k_view = k_pool.reshape(depth, n_pool, PAGE_ROWS, HEAD_DIM)
    v_view = v_pool.reshape(depth, n_pool, PAGE_ROWS, HEAD_DIM)
    f_view = jnp.transpose(f_pool, (0, 1, 3, 2))
    after, spread = tables

    def page_of(b, i, pt, j):
        return pt[b, (n_steps - 1 - i) * PAGES_PER_STEP + j]

    def page_spec(j, rows):
        return pl.BlockSpec((None, None, rows, LANES), lambda b, i, pt: (layer, page_of(b, i, pt, j), 0, 0))

    vec = pl.BlockSpec((None, N_FOX_HEADS, HEAD_DIM), lambda b, i, pt: (b, 0, 0))
    const = lambda a: pl.BlockSpec(a.shape, lambda b, i, pt: (0, 0))
    pages = range(PAGES_PER_STEP)
    grid_spec = pltpu.PrefetchScalarGridSpec(
        num_scalar_prefetch=1,
        grid=(bd, n_steps),
        in_specs=[page_spec(j, N_FOX_HEADS) for j in pages] + [page_spec(j, PAGE_ROWS) for j in pages] * 2
        + [vec, vec, vec, vec, const(after), const(spread)],
        out_specs=vec,
        scratch_shapes=[pltpu.VMEM((N_FOX_HEADS, LANES), F32)] * 4,
    )
    return pl.pallas_call(
        functools.partial(_dec_kernel, n_steps=n_steps),
        grid_spec=grid_spec,
        out_shape=jax.ShapeDtypeStruct((bd, N_FOX_HEADS, HEAD_DIM), F32),
        compiler_params=_cparams("parallel", "arbitrary"),
        name="fox_decode",
    )(page_table, *([f_view] * PAGES_PER_STEP), *([k_view] * PAGES_PER_STEP), *([v_view] * PAGES_PER_STEP),
      q, k_new, v_new, logf_new_b, after, spread)


def _pad_rows(x, rows):
    return jnp.pad(x[:, None, :], ((0, 0), (0, rows - 1), (0, 0)))


def _rows16(x):
    return jnp.pad(x, ((0, SROWS - x.shape[0]), (0, 0)))


def kernel(x_prompt, x_sample, cache_k, cache_v, cache_logf, state_gla, page_table, c_prompt, c_sample,
           ln_in_g, ln_in_b, w_ada, b_ada, w_in, b_f, gm_ln_g, gm_ln_b, gm_ws, gm_bs, gla_w2, gla_b2,
           gla_norm_g, w_o, ln1_g, ln1_b, w_gu, w_down, ln2_g, ln2_b):
    bp, seq, d = x_prompt.shape
    bd = x_sample.shape[0]
    depth = w_ada.shape[0]
    assert d == 4 * GLA_WIDTH and x_sample.shape[1] == 1 and bp + bd <= 16 and bd <= SROWS
    assert w_in.shape[2] == IN_WIDTH
    alpha = (2 * depth) ** 0.25
    m_p = bp * seq

    gla_tables = _gla_tables()
    dec_tables = _dec_tables()
    ext_tables = _ext_tables()

    c16 = jnp.zeros((SROWS, d), F32).at[:bd].set(c_sample).at[bd:bd + bp].set(c_prompt)
    mods = _ada(c16, w_ada, b_ada)

    def mod_p(l, which):
        return mods[l, bd:bd + bp, which * d:(which + 1) * d][:, None, :]

    def mod_s(l, *which):
        return jnp.concatenate([mods[l, :, w * d:(w + 1) * d] for w in which], axis=-1)

    w_in_b = _pack_in(w_in)
    w_o_b = _cast(w_o)
    w_down_b = _cast(w_down)

    xp, hp = _ln_mod(x_prompt, ln_in_g, ln_in_b, mod=(mod_p(0, 1), mod_p(0, 0)), tr=256)
    xs, hs = _ln_mod(x_sample, ln_in_g, ln_in_b,
                     mod=(mods[0, :bd, d:2 * d][:, None, :], mods[0, :bd, 0:d][:, None, :]), tr=1)
    xp, hp = xp.reshape(m_p, d), hp.reshape(m_p, d)
    xs, hs = _rows16(xs[:, 0, :]), _rows16(hs[:, 0, :])

    zero_state = jnp.zeros((bp, GLA_HEADS, GLA_DK, LANES), F32)
    k_all = v_all = None
    fp, sp = [], []
    kd, vd, fd, sd, gd = [], [], [], [], []
    for l in range(depth):
        last = l == depth - 1
        q16, k_all, k16, v_all, vt16, ug, tail, samp = _proj(hp, hs, w_in_b, l, k_all, v_all, depth, tm=512)

        samp = samp[:bd]
        heads3 = lambda a: a.reshape(bd, N_FOX_HEADS, HEAD_DIM)
        q_s = heads3(samp[:, 0:FOX_WIDTH])
        k_s = heads3(samp[:, FOX_WIDTH:2 * FOX_WIDTH])
        v_s = heads3(samp[:, 2 * FOX_WIDTH:3 * FOX_WIDTH])
        ug_s = _pad_rows(samp[:, 6 * TN:11 * TN], CHUNK)
        tail_s = _pad_rows(samp[:, 11 * TN:11 * TN + LANES], CHUNK)
        logf_s, _, _ = _fox_cum(tail_s, b_f[l], ext_tables)
        logf_s = logf_s[:, 0, :]
        fox_s = _fox_decode(page_table, cache_k, cache_v, cache_logf, l, q_s, k_s, v_s,
                            jnp.broadcast_to(logf_s[:, :, None], (bd, N_FOX_HEADS, LANES)), dec_tables)
        gm_s, vn_s = _gmlp(ug_s, gm_ln_g[l], gm_ln_b[l], gm_ws[l], gm_bs[l], emit_vn=True)
        gla_s, st_s = _gla(ug_s, tail_s, gla_w2[l], gla_b2[l], gla_norm_g[l], state_gla[l], gla_tables, valid_len=1)
        mix_s = jnp.concatenate([fox_s.reshape(bd, -1).astype(BF16), gm_s[:, 0, :], gla_s[:, 0, :]], axis=-1)

        tail3 = tail.reshape(bp, seq, LANES)
        ug3 = ug.reshape(bp, seq, UG_WIDTH)
        logf, ext_q, ext_k = _fox_cum(tail3, b_f[l], ext_tables)
        fox_o = _fox_prompt(q16.reshape(bp, seq, FOX_WIDTH), k16.reshape(bp, seq, FOX_WIDTH), vt16, ext_q, ext_k)
        gm_o, _ = _gmlp(ug3, gm_ln_g[l], gm_ln_b[l], gm_ws[l], gm_bs[l], emit_vn=False)
        gla_o, st = _gla(ug3, tail3, gla_w2[l], gla_b2[l], gla_norm_g[l], zero_state, gla_tables)

        x1, h1, x1_s, h1_s = _dense_ln(
            [fox_o.reshape(m_p, -1), gm_o.reshape(m_p, -1), gla_o.reshape(m_p, -1)], _rows16(mix_s), w_o_b, l,
            xp, xs, mod_p(l, 2), ln1_g[l], ln1_b[l], (mod_p(l, 4), mod_p(l, 3)), mod_s(l, 2, 4, 3),
            alpha=alpha, rows_per_batch=seq, tm=512)
        act, act_s = _swiglu(h1, h1_s, w_gu, l, tm=2048, tn=512)
        nxt = None if last else (mod_p(l + 1, 1), mod_p(l + 1, 0))
        nxt_s = mod_s(l, 5, 5, 5) if last else jnp.concatenate([mod_s(l, 5), mod_s(l + 1, 1, 0)], axis=-1)
        xp, hp, xs, hs = _dense_ln([act], act_s, w_down_b, l, x1, x1_s, mod_p(l, 5), ln2_g[l], ln2_b[l], nxt, nxt_s,
                                   alpha=alpha, rows_per_batch=seq, tm=512)

        fp.append(logf)
        sp.append(st)
        kd.append(k_s[:, None])
        vd.append(v_s[:, None])
        fd.append(logf_s[:, None, :])
        sd.append(st_s)
        gd.append(vn_s[:, 0:1, :])

    kv_shape = (depth, bp, seq, N_FOX_HEADS, HEAD_DIM)
    return (xp.reshape(bp, seq, d), xs[:bd, None, :], k_all.reshape(kv_shape), v_all.reshape(kv_shape),
            jnp.stack(fp), jnp.stack(sp), jnp.stack(kd), jnp.stack(vd), jnp.stack(fd), jnp.stack(sd), jnp.stack(gd))
```

```python
import functools
import math

import numpy as np
import jax
import jax.numpy as jnp
from jax import lax
from jax.experimental import pallas as pl
from jax.experimental.pallas import tpu as pltpu

F32 = jnp.float32
BF16 = jnp.bfloat16

LANES = 128
HEAD_DIM = 128
N_FOX_HEADS = 8
FOX_WIDTH = N_FOX_HEADS * HEAD_DIM
GMLP_GROUPS = 4
GMLP_WIDTH = GMLP_GROUPS * LANES
GLA_HEADS = 4
GLA_DK = 64
GLA_KW = GLA_HEADS * GLA_DK
GLA_WIDTH = GLA_HEADS * LANES
GLA_RANK = 16
GLA_TAU = 16.0
CHUNK = 128
PAGE = 128
LN_EPS = 1e-5
LOG2E = math.log2(math.e)
VMEM_LIMIT = 56 * 1024 * 1024
PROJ_VMEM_LIMIT = 60 * 1024 * 1024
SROWS = 16

TN = 512
OFF_FF = 3 * FOX_WIDTH
OFF_MU = OFF_FF + N_FOX_HEADS
OFF_GR = OFF_MU + 2 * GMLP_WIDTH + 2 * GLA_KW + GLA_WIDTH
OFF_GG = OFF_GR + GLA_RANK
IN_WIDTH = OFF_GG + GLA_WIDTH
N_PROJ_TILES = 12
UG_WIDTH = 5 * TN


def _cparams(*sem):
    return pltpu.CompilerParams(dimension_semantics=sem, vmem_limit_bytes=VMEM_LIMIT)


def _log_sigmoid(x):
    return jnp.minimum(x, 0.0) - jnp.log1p(jnp.exp(-jnp.abs(x)))


def _silu(x):
    return x * (1.0 / (1.0 + jnp.exp(-x)))


def _split3(x):
    hi = x.astype(BF16)
    r = x - hi.astype(F32)
    mid = r.astype(BF16)
    lo = (r - mid.astype(F32)).astype(BF16)
    return hi, mid, lo


def _dot(a, b):
    return jnp.dot(a, b, preferred_element_type=F32)


def _dot_nt(a, b):
    return lax.dot_general(a, b, (((1,), (1,)), ((), ())), preferred_element_type=F32)


def _dot_exact(x, w):
    hi, mid, lo = _split3(x)
    return _dot(hi, w) + _dot(mid, w) + _dot(lo, w)


def _dot_exact_l(w, x):
    hi, mid, lo = _split3(x)
    return _dot(w, hi) + _dot(w, mid) + _dot(w, lo)


def _ada_kernel(c_ref, w_ref, b_ref, o_ref):
    a = _silu(c_ref[...]).astype(BF16)
    o_ref[...] = _dot(a, w_ref[...].astype(BF16)) + b_ref[...]


def _ada(c16, w_ada, b_ada):
    depth, d, n = w_ada.shape
    tn = 1024
    return pl.pallas_call(
        _ada_kernel,
        grid=(depth, n // tn),
        in_specs=[pl.BlockSpec((16, d), lambda l, j: (0, 0)),
                  pl.BlockSpec((None, d, tn), lambda l, j: (l, 0, j)),
                  pl.BlockSpec((None, 1, tn), lambda l, j: (l, 0, j))],
        out_specs=pl.BlockSpec((None, 16, tn), lambda l, j: (l, 0, j)),
        out_shape=jax.ShapeDtypeStruct((depth, 16, n), F32),
        compiler_params=_cparams("parallel", "parallel"),
        name="ada",
    )(c16, w_ada, b_ada.reshape(depth, 1, n))


def _ln_kernel(*refs, has_res, has_mod, alpha):
    it = iter(refs)
    x = next(it)[...]
    if has_res:
        y = next(it)[...]
        gate = next(it)[...]
        x = alpha * x + gate * y
    g = next(it)[...]
    b = next(it)[...]
    if has_mod:
        sc = next(it)[...]
        sh = next(it)[...]
    xn_ref = next(it)
    mu = jnp.mean(x, axis=-1, keepdims=True)
    xc = x - mu
    var = jnp.mean(xc * xc, axis=-1, keepdims=True)
    xn = xc * lax.rsqrt(var + LN_EPS) * g + b
    xn_ref[...] = xn
    if has_mod:
        h_ref = next(it)
        h_ref[...] = (xn * (1.0 + sc) + sh).astype(BF16)


def _ln_mod(x, g, b, *, res=None, mod=None, alpha=1.0, tr):
    bsz, s, d = x.shape
    tr = min(tr, s)
    row = pl.BlockSpec((None, tr, d), lambda i, j: (i, j, 0))
    per_b = pl.BlockSpec((None, 1, d), lambda i, j: (i, 0, 0))
    vec = pl.BlockSpec((1, d), lambda i, j: (0, 0))
    args, specs = [x], [row]
    if res is not None:
        y, gate = res
        args += [y, gate]
        specs += [row, per_b]
    args += [g.reshape(1, d), b.reshape(1, d)]
    specs += [vec, vec]
    out_shape = [jax.ShapeDtypeStruct((bsz, s, d), F32)]
    out_specs = [row]
    if mod is not None:
        sc, sh = mod
        args += [sc, sh]
        specs += [per_b, per_b]
        out_shape.append(jax.ShapeDtypeStruct((bsz, s, d), BF16))
        out_specs.append(row)
    out = pl.pallas_call(
        functools.partial(_ln_kernel, has_res=res is not None, has_mod=mod is not None, alpha=alpha),
        grid=(bsz, s // tr),
        in_specs=specs, out_specs=out_specs, out_shape=out_shape,
        compiler_params=_cparams("parallel", "parallel"),
        name="ln_mod",
    )(*args)
    return out if mod is not None else (out[0], None)


PACK_EXTRA = 32


def _pack_in_kernel(wm_ref, wn_ref, wg_ref, o_ref):
    j = pl.program_id(1)
    k_dim = wm_ref.shape[1]

    @pl.when(j < 6)
    def _():
        o_ref[...] = wm_ref[...].T.astype(BF16)

    def shifted(rows):
        return jnp.concatenate([wm_ref[rows:TN, :], wn_ref[0:rows, :]], axis=0).T.astype(BF16)

    @pl.when((j >= 6) & (j < 10))
    def _():
        o_ref[...] = shifted(OFF_MU - OFF_FF)

    @pl.when(j == 10)
    def _():
        o_ref[...] = shifted(OFF_GG - 10 * TN)

    @pl.when(j == N_PROJ_TILES - 1)
    def _():
        lo = OFF_GR - (OFF_GR // PACK_EXTRA) * PACK_EXTRA
        t = jnp.concatenate([wn_ref[0:N_FOX_HEADS, :], wg_ref[lo:lo + GLA_RANK, :],
                             jnp.zeros((TN - N_FOX_HEADS - GLA_RANK, k_dim), F32)], axis=0)
        o_ref[...] = t.T.astype(BF16)


def _pack_in(w_in):
    depth, k, _ = w_in.shape
    w_t = jnp.transpose(w_in, (0, 2, 1))
    per = TN // PACK_EXTRA
    nb = lambda j: jnp.where((j >= 6) & (j < N_PROJ_TILES - 1), per * (j + 1), OFF_FF // PACK_EXTRA)
    return pl.pallas_call(
        _pack_in_kernel,
        grid=(depth, N_PROJ_TILES),
        in_specs=[pl.BlockSpec((None, TN, k), lambda l, j: (l, jnp.minimum(j, 10), 0)),
                  pl.BlockSpec((None, PACK_EXTRA, k), lambda l, j: (l, nb(j), 0)),
                  pl.BlockSpec((None, PACK_EXTRA, k), lambda l, j: (l, OFF_GR // PACK_EXTRA, 0))],
        out_specs=pl.BlockSpec((None, k, TN), lambda l, j: (l, 0, j)),
        out_shape=jax.ShapeDtypeStruct((depth, k, N_PROJ_TILES * TN), BF16),
        compiler_params=_cparams("parallel", "parallel"),
        name="pack_in",
    )(w_t, w_t, w_t)


def _cast_kernel(w_ref, o_ref):
    o_ref[...] = w_ref[...].astype(BF16)


def _cast(w):
    depth, k, n = w.shape
    blk = pl.BlockSpec((None, k, TN), lambda l, j: (l, 0, j))
    return pl.pallas_call(
        _cast_kernel, grid=(depth, n // TN), in_specs=[blk], out_specs=blk,
        out_shape=jax.ShapeDtypeStruct(w.shape, BF16),
        compiler_params=_cparams("parallel", "parallel"), name="cast",
    )(w)


def _proj_kernel(*refs, aliased, q_scale, tm):
    x_ref, xs_ref, w_ref = refs[:3]
    q_ref, k32_ref, k16_ref, v32_ref, vt_ref, ug_ref, t_ref, s_ref = refs[3 + (2 if aliased else 0):]
    x = x_ref[...]
    q_ref[...] = (_dot(x, w_ref[:, 0:FOX_WIDTH]) * q_scale).astype(BF16)
    k = _dot(x, w_ref[:, FOX_WIDTH:2 * FOX_WIDTH])
    k16_ref[...] = k.astype(BF16)
    v = _dot(x, w_ref[:, 2 * FOX_WIDTH:3 * FOX_WIDTH])
    vt_ref[...] = v.T.astype(BF16)
    for h in range(N_FOX_HEADS):
        cols = slice(h * HEAD_DIM, (h + 1) * HEAD_DIM)
        k32_ref[pl.ds(h, tm, stride=N_FOX_HEADS), :] = k[:, cols]
        v32_ref[pl.ds(h, tm, stride=N_FOX_HEADS), :] = v[:, cols]
    ug_ref[...] = _dot(x, w_ref[:, 6 * TN:11 * TN])
    t_ref[...] = _dot(x, w_ref[:, 11 * TN:11 * TN + LANES])

    @pl.when(pl.program_id(0) == 0)
    def _():
        s_ref[...] = _dot(xs_ref[...], w_ref[...])


def _proj(x, xs, w_in, layer, k_prev, v_prev, depth, *, tm):
    m, k = x.shape
    tm = min(tm, m)
    n_all = w_in.shape[2]
    in_specs = [pl.BlockSpec((tm, k), lambda i: (i, 0)),
                pl.BlockSpec((SROWS, k), lambda i: (0, 0)),
                pl.BlockSpec((None, k, n_all), lambda i: (layer, 0, 0), pipeline_mode=pl.Buffered(1))]
    args = [x, xs, w_in]
    aliases = {}
    if k_prev is not None:
        in_specs += [pl.BlockSpec(memory_space=pl.ANY)] * 2
        args += [k_prev, v_prev]
        aliases = {3: 1, 4: 3}
    rows = pl.BlockSpec((tm, FOX_WIDTH), lambda i: (i, 0))
    pairs = pl.BlockSpec((None, tm * N_FOX_HEADS, HEAD_DIM), lambda i: (layer, i, 0))
    out_shape = [jax.ShapeDtypeStruct((m, FOX_WIDTH), BF16),
                 jax.ShapeDtypeStruct((depth, m * N_FOX_HEADS, HEAD_DIM), F32),
                 jax.ShapeDtypeStruct((m, FOX_WIDTH), BF16),
                 jax.ShapeDtypeStruct((depth, m * N_FOX_HEADS, HEAD_DIM), F32),
                 jax.ShapeDtypeStruct((FOX_WIDTH, m), BF16),
                 jax.ShapeDtypeStruct((m, UG_WIDTH), F32),
                 jax.ShapeDtypeStruct((m, LANES), F32),
                 jax.ShapeDtypeStruct((SROWS, n_all), F32)]
    out_specs = [rows, pairs, rows, pairs,
                 pl.BlockSpec((FOX_WIDTH, tm), lambda i: (0, i)),
                 pl.BlockSpec((tm, UG_WIDTH), lambda i: (i, 0)),
                 pl.BlockSpec((tm, LANES), lambda i: (i, 0)),
                 pl.BlockSpec((SROWS, n_all), lambda i: (0, 0))]
    return pl.pallas_call(
        functools.partial(_proj_kernel, aliased=k_prev is not None, q_scale=HEAD_DIM ** -0.5 * LOG2E, tm=tm),
        grid=(m // tm,),
        in_specs=in_specs, out_specs=out_specs, out_shape=out_shape,
        input_output_aliases=aliases,
        compiler_params=pltpu.CompilerParams(dimension_semantics=("arbitrary",), vmem_limit_bytes=PROJ_VMEM_LIMIT),
        name="proj",
    )(*args)


SWIGLU_SUB = 1024


def _swiglu_kernel(x_ref, xs_ref, wg_ref, wu_ref, o_ref, os_ref, wbg, wbu):
    def result(x):
        return (_silu(_dot(x, wbg[...])) * _dot(x, wbu[...])).astype(BF16)

    @pl.when(pl.program_id(1) == 0)
    def _():
        wbg[...] = wg_ref[...].astype(BF16)
        wbu[...] = wu_ref[...].astype(BF16)
        os_ref[...] = result(xs_ref[...])

    sub = min(SWIGLU_SUB, x_ref.shape[0])
    for r0 in range(0, x_ref.shape[0], sub):
        o_ref[r0:r0 + sub, :] = result(x_ref[r0:r0 + sub, :])


def _swiglu(x, x_sample, w_gu, layer, *, tm, tn):
    m, k = x.shape
    tm = min(tm, m)
    n = w_gu.shape[2] // 2
    up0 = n // tn
    return pl.pallas_call(
        _swiglu_kernel,
        grid=(n // tn, m // tm),
        in_specs=[pl.BlockSpec((tm, k), lambda j, i: (i, 0)),
                  pl.BlockSpec((SROWS, k), lambda j, i: (0, 0)),
                  pl.BlockSpec((None, k, tn), lambda j, i: (layer, 0, j)),
                  pl.BlockSpec((None, k, tn), lambda j, i: (layer, 0, up0 + j))],
        out_specs=[pl.BlockSpec((tm, tn), lambda j, i: (i, j)), pl.BlockSpec((SROWS, tn), lambda j, i: (0, j))],
        out_shape=[jax.ShapeDtypeStruct((m, n), BF16), jax.ShapeDtypeStruct((SROWS, n), BF16)],
        scratch_shapes=[pltpu.VMEM((k, tn), BF16)] * 2,
        compiler_params=_cparams("arbitrary", "arbitrary"),
        name="swiglu",
    )(x, x_sample, w_gu, w_gu)


DENSE_LN_SUB = 128


def _layer_norm(z, g, b):
    mu = jnp.mean(z, axis=-1, keepdims=True)
    zc = z - mu
    var = jnp.mean(zc * zc, axis=-1, keepdims=True)
    return zc * lax.rsqrt(var + LN_EPS) * g + b


def _dense_ln_kernel(*refs, n_lhs, has_mod, alpha):
    it = iter(refs)
    x_refs = [next(it) for _ in range(n_lhs)]
    xs_ref, w_ref, res_ref, gate_ref, g_ref, b_ref = [next(it) for _ in range(6)]
    sc_ref, sh_ref = (next(it), next(it)) if has_mod else (None, None)
    res_s_ref, mod_s_ref = next(it), next(it)
    xn_ref = next(it)
    h_ref = next(it) if has_mod else None
    xn_s_ref = next(it)
    h_s_ref = next(it) if has_mod else None
    d = xn_ref.shape[-1]

    def finish(lhs, res, gate, sc, sh):
        xn = _layer_norm(alpha * res + gate * _dot(lhs, w_ref[...]), g_ref[...], b_ref[...])
        return xn, ((xn * (1.0 + sc) + sh).astype(BF16) if has_mod else None)

    @pl.when(pl.program_id(0) == 0)
    def _():
        mod_s = mod_s_ref[...]
        xn, h = finish(xs_ref[...], res_s_ref[...], mod_s[:, 0:d], mod_s[:, d:2 * d], mod_s[:, 2 * d:3 * d])
        xn_s_ref[...] = xn
        if has_mod:
            h_s_ref[...] = h

    tm = xn_ref.shape[0]
    sub = min(tm, DENSE_LN_SUB)
    for r0 in range(0, tm, sub):
        rows = slice(r0, r0 + sub)
        parts = [x[rows, :] for x in x_refs]
        lhs = parts[0] if n_lhs == 1 else jnp.concatenate(parts, axis=1)
        xn, h = finish(lhs, res_ref[rows, :], gate_ref[...], sc_ref[...] if has_mod else None,
                       sh_ref[...] if has_mod else None)
        xn_ref[rows, :] = xn
        if has_mod:
            h_ref[rows, :] = h


def _dense_ln(xs_list, x_sample, w, layer, res, res_s, gate, ln_g, ln_b, mod, mod_s, *, alpha, rows_per_batch, tm):
    m = xs_list[0].shape[0]
    tm = min(tm, rows_per_batch)
    k, d = w.shape[1], w.shape[2]
    assert sum(x.shape[1] for x in xs_list) == k and x_sample.shape == (SROWS, k)
    steps_per_batch = rows_per_batch // tm
    has_mod = mod is not None
    row = pl.BlockSpec((tm, d), lambda i: (i, 0))
    per_b = pl.BlockSpec((None, 1, d), lambda i: (i // steps_per_batch, 0, 0))
    vec = pl.BlockSpec((1, d), lambda i: (0, 0))
    srow = pl.BlockSpec((SROWS, d), lambda i: (0, 0))
    in_specs = [pl.BlockSpec((tm, x.shape[1]), lambda i: (i, 0)) for x in xs_list]
    in_specs += [pl.BlockSpec((SROWS, k), lambda i: (0, 0)),
                 pl.BlockSpec((None, k, d), lambda i: (layer, 0, 0), pipeline_mode=pl.Buffered(1)),
                 row, per_b, vec, vec]
    args = list(xs_list) + [x_sample, w, res, gate, ln_g.reshape(1, d), ln_b.reshape(1, d)]
    if has_mod:
        in_specs += [per_b, per_b]
        args += list(mod)
    in_specs += [srow, pl.BlockSpec((SROWS, 3 * d), lambda i: (0, 0))]
    args += [res_s, mod_s]
    out_specs, out_shape = [row], [jax.ShapeDtypeStruct((m, d), F32)]
    if has_mod:
        out_specs.append(row)
        out_shape.append(jax.ShapeDtypeStruct((m, d), BF16))
    out_specs.append(srow)
    out_shape.append(jax.ShapeDtypeStruct((SROWS, d), F32))
    if has_mod:
        out_specs.append(srow)
        out_shape.append(jax.ShapeDtypeStruct((SROWS, d), BF16))
    out = pl.pallas_call(
        functools.partial(_dense_ln_kernel, n_lhs=len(xs_list), has_mod=has_mod, alpha=alpha),
        grid=(m // tm,),
        in_specs=in_specs, out_specs=out_specs, out_shape=out_shape,
        compiler_params=_cparams("arbitrary"),
        name="dense_ln",
    )(*args)
    return out if has_mod else (out[0], None, out[1], None)


EXT_PER_HEAD = 6


def _ext_tables():
    pq = np.zeros((3 * LANES, LANES), np.float32)
    pk = np.zeros((3 * LANES, LANES), np.float32)
    oq = np.zeros((1, LANES), np.float32)
    ok = np.zeros((1, LANES), np.float32)
    for h in range(N_FOX_HEADS):
        for part in range(3):
            pq[part * LANES + h, EXT_PER_HEAD * h + part] = 1.0
            pk[part * LANES + h, EXT_PER_HEAD * h + 3 + part] = -1.0
            oq[0, EXT_PER_HEAD * h + 3 + part] = 1.0
            ok[0, EXT_PER_HEAD * h + part] = 1.0
    return jnp.asarray(pq, BF16), jnp.asarray(pk, BF16), jnp.asarray(oq), jnp.asarray(ok)


def _fox_cum_kernel(t_ref, bf_ref, pq_ref, pk_ref, oq_ref, ok_ref, lf_ref, eq_ref, ek_ref, carry, *, tb):
    @pl.when(pl.program_id(1) == 0)
    def _():
        carry[...] = jnp.zeros(carry.shape, F32)

    lf = _log_sigmoid(t_ref[...] + bf_ref[...])
    lf_ref[...] = lf[:, 0:N_FOX_HEADS]
    r = lax.broadcasted_iota(jnp.int32, (tb, tb), 0)
    c = lax.broadcasted_iota(jnp.int32, (tb, tb), 1)
    lower = (c <= r).astype(BF16)
    cum = _dot_exact_l(lower, lf) + carry[0:1, :]
    carry[...] = jnp.broadcast_to(cum[tb - 1:tb, :], carry.shape)
    hi, mid, lo = _split3(cum * LOG2E)
    cat = jnp.concatenate([hi, mid, lo], axis=1)
    eq_ref[...] = (_dot(cat, pq_ref[...]) + oq_ref[...]).astype(BF16)
    ek_ref[...] = (_dot(cat, pk_ref[...]) + ok_ref[...]).astype(BF16)


def _fox_cum(tail, b_f, tables):
    bsz, s, _ = tail.shape
    tb = min(512, s)
    pq, pk, oq, ok = tables
    bf = jnp.zeros((1, LANES), F32).at[0, :N_FOX_HEADS].set(b_f)
    blk = pl.BlockSpec((None, tb, LANES), lambda b, i: (b, i, 0))
    const = lambda shape: pl.BlockSpec(shape, lambda b, i: (0, 0))
    return pl.pallas_call(
        functools.partial(_fox_cum_kernel, tb=tb),
        grid=(bsz, s // tb),
        in_specs=[blk, const((1, LANES)), const(pq.shape), const(pk.shape), const((1, LANES)), const((1, LANES))],
        out_specs=[pl.BlockSpec((None, tb, N_FOX_HEADS), lambda b, i: (b, i, 0)), blk, blk],
        out_shape=[jax.ShapeDtypeStruct((bsz, s, N_FOX_HEADS), F32),
                   jax.ShapeDtypeStruct((bsz, s, LANES), BF16),
                   jax.ShapeDtypeStruct((bsz, s, LANES), BF16)],
        scratch_shapes=[pltpu.VMEM((8, LANES), F32)],
        compiler_params=_cparams("parallel", "arbitrary"),
        name="fox_cum",
    )(tail, bf, pq, pk, oq, ok)


HEADS_PER_STEP = 8
ATTN_TILE = 1024


def _fox_kernel(qi_ref, kj_ref, q_ref, k_ref, vt_ref, eq_ref, ek_ref, o_ref, m_sc, l_sc, acc_sc, qx_sc, *, tq, tk):
    grp = pl.program_id(1)
    p = pl.program_id(2)
    qi = qi_ref[p]
    kj = kj_ref[p]

    @pl.when(kj == 0)
    def _():
        m_sc[...] = jnp.full(m_sc.shape, -jnp.inf, F32)
        l_sc[...] = jnp.zeros(l_sc.shape, F32)
        acc_sc[...] = jnp.zeros(acc_sc.shape, F32)
        lane = lax.broadcasted_iota(jnp.int32, (tq, LANES), 1)
        eq = eq_ref[...]
        for hh in range(HEADS_PER_STEP):
            lo = (grp * HEADS_PER_STEP + hh) * EXT_PER_HEAD
            own = (lane >= lo) & (lane < lo + EXT_PER_HEAD)
            qx_sc[hh] = jnp.concatenate([q_ref[:, hh * LANES:(hh + 1) * LANES],
                                         jnp.where(own, eq, jnp.zeros_like(eq))], axis=1)

    def step(diagonal):
        ek = ek_ref[...]
        for hh in range(HEADS_PER_STEP):
            cols = slice(hh * LANES, (hh + 1) * LANES)
            kx = jnp.concatenate([k_ref[:, cols], ek], axis=1)
            s = _dot_nt(kx, qx_sc[hh])
            if diagonal:
                key = lax.broadcasted_iota(jnp.int32, (tk, tq), 0)
                qry = lax.broadcasted_iota(jnp.int32, (tk, tq), 1)
                s = jnp.where(key <= qry, s, -jnp.inf)
            m_prev = m_sc[hh][0:1, :]
            m_new = jnp.maximum(m_prev, jnp.max(s, axis=0, keepdims=True))
            alpha = jnp.exp2(m_prev - m_new)
            pr = jnp.exp2(s - m_new)
            l_new = alpha * l_sc[hh][0:1, :] + jnp.sum(pr, axis=0, keepdims=True)
            acc = alpha * acc_sc[hh] + _dot(vt_ref[cols, :], pr.astype(BF16))
            m_sc[hh] = jnp.broadcast_to(m_new, (8, tq))
            l_sc[hh] = jnp.broadcast_to(l_new, (8, tq))
            acc_sc[hh] = acc
            if diagonal:
                o_ref[:, cols] = (acc / l_new).T.astype(o_ref.dtype)

    @pl.when(kj == qi)
    def _():
        step(True)

    @pl.when(kj != qi)
    def _():
        step(False)


def _fox_prompt(q, k, vt, ext_q, ext_k):
    bsz, s, _ = q.shape
    tq = tk = min(ATTN_TILE, s)
    nq = s // tq
    width = HEADS_PER_STEP * LANES
    pairs = [(i, j) for i in range(nq) for j in range(i + 1)]
    qi_tbl = jnp.asarray([pr[0] for pr in pairs], jnp.int32)
    kj_tbl = jnp.asarray([pr[1] for pr in pairs], jnp.int32)
    grid_spec = pltpu.PrefetchScalarGridSpec(
        num_scalar_prefetch=2,
        grid=(bsz, N_FOX_HEADS // HEADS_PER_STEP, len(pairs)),
        in_specs=[pl.BlockSpec((None, tq, width), lambda b, g, p, qi, kj: (b, qi[p], g)),
                  pl.BlockSpec((None, tk, width), lambda b, g, p, qi, kj: (b, kj[p], g)),
                  pl.BlockSpec((width, tk), lambda b, g, p, qi, kj: (g, b * nq + kj[p])),
                  pl.BlockSpec((None, tq, LANES), lambda b, g, p, qi, kj: (b, qi[p], 0)),
                  pl.BlockSpec((None, tk, LANES), lambda b, g, p, qi, kj: (b, kj[p], 0))],
        out_specs=pl.BlockSpec((None, tq, width), lambda b, g, p, qi, kj: (b, qi[p], g)),
        scratch_shapes=[pltpu.VMEM((HEADS_PER_STEP, 8, tq), F32),
                        pltpu.VMEM((HEADS_PER_STEP, 8, tq), F32),
                        pltpu.VMEM((HEADS_PER_STEP, LANES, tq), F32),
                        pltpu.VMEM((HEADS_PER_STEP, tq, 2 * LANES), BF16)],
    )
    return pl.pallas_call(
        functools.partial(_fox_kernel, tq=tq, tk=tk),
        grid_spec=grid_spec,
        out_shape=jax.ShapeDtypeStruct((bsz, s, FOX_WIDTH), BF16),
        compiler_params=_cparams("parallel", "parallel", "arbitrary"),
        name="fox_prompt",
    )(qi_tbl, kj_tbl, q, k, vt, ext_q, ext_k)


GMLP_CHUNKS_PER_STEP = 4


def _gmlp_kernel(uv_ref, g_ref, b_ref, ws_ref, bs_ref, o_ref, *vn_ref):
    r = lax.broadcasted_iota(jnp.int32, (CHUNK, CHUNK), 0)
    c = lax.broadcasted_iota(jnp.int32, (CHUNK, CHUNK), 1)
    for grp in range(GMLP_GROUPS):
        lo = grp * LANES
        w = jnp.where(c <= r, ws_ref[grp], 0.0).astype(BF16)
        for r0 in range(0, uv_ref.shape[0], CHUNK):
            rows = slice(r0, r0 + CHUNK)
            u = jax.nn.gelu(uv_ref[rows, lo:lo + LANES])
            v = jax.nn.gelu(uv_ref[rows, GMLP_WIDTH + lo:GMLP_WIDTH + lo + LANES])
            mu = jnp.mean(v, axis=-1, keepdims=True)
            vc = v - mu
            var = jnp.mean(vc * vc, axis=-1, keepdims=True)
            vn = vc * lax.rsqrt(var + LN_EPS) * g_ref[:, lo:lo + LANES] + b_ref[:, lo:lo + LANES]
            mixed = _dot(w, vn.astype(BF16)) + bs_ref[grp]
            o_ref[rows, lo:lo + LANES] = (u * mixed).astype(o_ref.dtype)
            if vn_ref:
                vn_ref[0][rows, lo:lo + LANES] = vn


def _gmlp(ug, ln_g, ln_b, ws, bs, *, emit_vn):
    bsz, s, _ = ug.shape
    rows = CHUNK * min(GMLP_CHUNKS_PER_STEP, s // CHUNK)
    blk = pl.BlockSpec((None, rows, GMLP_WIDTH), lambda b, i: (b, i, 0))
    vec = pl.BlockSpec((1, GMLP_WIDTH), lambda b, i: (0, 0))
    cube = pl.BlockSpec((GMLP_GROUPS, CHUNK, CHUNK), lambda b, i: (0, 0, 0))
    out_shape = [jax.ShapeDtypeStruct((bsz, s, GMLP_WIDTH), BF16)]
    if emit_vn:
        out_shape.append(jax.ShapeDtypeStruct((bsz, s, GMLP_WIDTH), F32))
    bs_b = jnp.broadcast_to(bs[:, :, None], (GMLP_GROUPS, CHUNK, LANES))
    out = pl.pallas_call(
        _gmlp_kernel,
        grid=(bsz, s // rows),
        in_specs=[pl.BlockSpec((None, rows, 2 * GMLP_WIDTH), lambda b, i: (b, i, 0)), vec, vec, cube, cube],
        out_specs=[blk] * len(out_shape),
        out_shape=out_shape,
        compiler_params=_cparams("parallel", "parallel"),
        name="gmlp",
    )(ug, ln_g.reshape(1, GMLP_WIDTH), ln_b.reshape(1, GMLP_WIDTH), ws, bs_b)
    return out if emit_vn else (out[0], None)


_GLA_LEVELS = (64, 32, 16, 8, 4, 2, 1)
_N_LEVEL = len(_GLA_LEVELS)


def _gla_tables():
    x = np.arange(CHUNK)[:, None]
    j = np.arange(CHUNK)[None, :]
    groups = []
    lv = np.full((CHUNK, CHUNK), -1, np.int32)
    for i, half in enumerate(_GLA_LEVELS):
        anchor = (x // (2 * half)) * (2 * half) + half - 1
        groups.append((j > np.minimum(x, anchor)) & (j <= np.maximum(x, anchor)))
        same = (x // (2 * half)) == (j // (2 * half))
        upper_t = (x % (2 * half)) >= half
        lower_s = (j % (2 * half)) < half
        lv[same & upper_t & lower_s] = i
    lv[np.arange(CHUNK), np.arange(CHUNK)] = _N_LEVEL
    groups.append(j <= x)
    groups.append(j > x)
    mall = np.concatenate(groups, axis=0).astype(np.float32)
    return jnp.asarray(mall, BF16), jnp.asarray(np.tile(lv, (GLA_HEADS, 1)))


def _gla_kernel(qk_ref, v_ref, g_ref, t_ref, w2_ref, b2_ref, ng_ref, s0_ref, mall_ref, lv_ref, o_ref, st_ref, state,
                *, valid_len, n_chunks):
    ci = pl.program_id(1)

    @pl.when(ci == 0)
    def _():
        state[...] = s0_ref[...].T

    qk = qk_ref[...]
    q = qk[:, 0:GLA_KW] * (GLA_DK ** -0.5)
    k = qk[:, GLA_KW:2 * GLA_KW]
    v = v_ref[...]
    gate = g_ref[...]
    la = _log_sigmoid(_dot(t_ref[...].astype(BF16), w2_ref[...].astype(BF16)) + b2_ref[...]) * (1.0 / GLA_TAU)
    if valid_len < CHUNK:
        live = lax.broadcasted_iota(jnp.int32, (CHUNK, GLA_KW), 0) < valid_len
        la = jnp.where(live, la, 0.0)
        k = jnp.where(live, k, 0.0)

    la_hi = la.astype(BF16)
    la_lo = (la - la_hi.astype(F32)).astype(BF16)
    single = valid_len == 1
    mall = mall_ref[_N_LEVEL * CHUNK:, :] if single else mall_ref[...]
    w = jnp.exp(_dot(mall, la_hi) + _dot(mall, la_lo))

    lane_head = lax.broadcasted_iota(jnp.int32, (CHUNK, GLA_KW), 1) // GLA_DK
    heads = [lane_head == h for h in range(GLA_HEADS)]

    q_stack = jnp.concatenate([jnp.where(hm, q, 0.0) for hm in heads], axis=0)
    n_lv = 0 if single else _N_LEVEL

    def level_w(i):
        return w[i * CHUNK:(i + 1) * CHUNK]

    if single:
        a = _dot_nt(q_stack.astype(BF16), k.astype(BF16))
    else:
        lv = lv_ref[...]
        a = jnp.zeros((GLA_HEADS * CHUNK, CHUNK), F32)
        for i in range(_N_LEVEL):
            wi = level_w(i)
            qi = q_stack * jnp.concatenate([wi] * GLA_HEADS, axis=0)
            a = jnp.where(lv == i, _dot_nt(qi.astype(BF16), (k * wi).astype(BF16)), a)
        a = jnp.where(lv == _N_LEVEL, _dot_nt(q_stack.astype(BF16), k.astype(BF16)), a)

    w_pre = level_w(n_lv)
    w_suf = level_w(n_lv + 1)
    st = state[...]
    q_pre = q_stack * jnp.concatenate([w_pre] * GLA_HEADS, axis=0)
    o_inter = _dot_nt(q_pre.astype(BF16), st.astype(BF16))
    v_b = v.astype(BF16)
    for h in range(GLA_HEADS):
        rows = slice(h * CHUNK, (h + 1) * CHUNK)
        cols = slice(h * LANES, (h + 1) * LANES)
        o = _dot(a[rows].astype(BF16), v_b[:, cols]) + o_inter[rows]
        o = o * lax.rsqrt(jnp.mean(o * o, axis=-1, keepdims=True) + LN_EPS)
        o_ref[:, cols] = (o * ng_ref[:, cols] * _silu(gate[:, cols])).astype(o_ref.dtype)

    kvt = _dot(v.T.astype(BF16), (k * w_suf).astype(BF16))
    upd = jnp.zeros((LANES, GLA_KW), F32)
    for h in range(GLA_HEADS):
        upd = upd + jnp.where(heads[h], kvt[h * LANES:(h + 1) * LANES], 0.0)
    st_new = st * w_pre[CHUNK - 1:CHUNK, :] + upd
    state[...] = st_new

    @pl.when(ci == n_chunks - 1)
    def _():
        st_ref[...] = st_new.T


def _gla(ug, tail, w2, b2, norm_g, s0, tables, *, valid_len=CHUNK):
    bsz, s, _ = ug.shape
    n_chunks = s // CHUNK
    mall, lv = tables
    w2pad = jnp.zeros((LANES, GLA_KW), F32).at[N_FOX_HEADS:N_FOX_HEADS + GLA_RANK].set(w2)
    const2 = lambda b, i: (0, 0)
    col = lambda c: pl.BlockSpec((None, CHUNK, TN), lambda b, i: (b, i, c))
    o, st = pl.pallas_call(
        functools.partial(_gla_kernel, valid_len=valid_len, n_chunks=n_chunks),
        grid=(bsz, n_chunks),
        in_specs=[col(2), col(3), col(4),
                  pl.BlockSpec((None, CHUNK, LANES), lambda b, i: (b, i, 0)),
                  pl.BlockSpec((LANES, GLA_KW), const2),
                  pl.BlockSpec((1, GLA_KW), const2),
                  pl.BlockSpec((1, GLA_WIDTH), const2),
                  pl.BlockSpec((None, GLA_KW, LANES), lambda b, i: (b, 0, 0)),
                  pl.BlockSpec(mall.shape, const2),
                  pl.BlockSpec(lv.shape, const2)],
        out_specs=[pl.BlockSpec((None, CHUNK, GLA_WIDTH), lambda b, i: (b, i, 0)),
                   pl.BlockSpec((None, GLA_KW, LANES), lambda b, i: (b, 0, 0))],
        out_shape=[jax.ShapeDtypeStruct((bsz, s, GLA_WIDTH), BF16),
                   jax.ShapeDtypeStruct((bsz, GLA_KW, LANES), F32)],
        scratch_shapes=[pltpu.VMEM((LANES, GLA_KW), F32)],
        compiler_params=_cparams("parallel", "arbitrary"),
        name="gla",
    )(ug, ug, ug, tail, w2pad, b2.reshape(1, GLA_KW), norm_g.reshape(1, GLA_WIDTH),
      s0.reshape(bsz, GLA_KW, LANES), mall, lv)
    return o, st.reshape(bsz, GLA_HEADS, GLA_DK, LANES)


PAGES_PER_STEP = 16
PAGE_ROWS = PAGE * N_FOX_HEADS


def _dec_tables():
    t = np.arange(PAGE)
    after = (t[:, None] > t[None, :]).astype(np.float32)
    spread = (t[:, None] == (np.arange(PAGE_ROWS) // N_FOX_HEADS)[None, :]).astype(np.float32)
    return jnp.asarray(after, BF16), jnp.asarray(spread, BF16)


def _dec_kernel(pt_ref, *refs, n_steps):
    n = PAGES_PER_STEP
    f_refs, k_refs, v_refs = refs[:n], refs[n:2 * n], refs[2 * n:3 * n]
    q_ref, kn_ref, vn_ref, fn_ref, after_ref, spread_ref, o_ref, m_sc, l_sc, acc_sc, carry = refs[3 * n:]
    i = pl.program_id(1)
    scale = HEAD_DIM ** -0.5
    q = q_ref[...]

    @pl.when(i == 0)
    def _():
        s_self = jnp.sum(q.astype(BF16).astype(F32) * kn_ref[...].astype(BF16).astype(F32), axis=-1, keepdims=True) * scale
        m_sc[...] = jnp.broadcast_to(s_self, m_sc.shape)
        l_sc[...] = jnp.ones(l_sc.shape, F32)
        acc_sc[...] = vn_ref[...].astype(BF16).astype(F32)
        carry[...] = fn_ref[...]

    lhs = jnp.concatenate([f_ref[...] for f_ref in f_refs], axis=0)
    suffix = _dot_exact(lhs, after_ref[...])
    total = _dot_exact(lhs, jnp.ones((PAGE, LANES), BF16))
    run = carry[...]
    bias = [None] * n
    for j in reversed(range(n)):
        sl = slice(j * N_FOX_HEADS, (j + 1) * N_FOX_HEADS)
        bias[j] = suffix[sl] + run
        run = run + total[sl]
    carry[...] = run
    bias_x = _dot_exact(jnp.concatenate(bias, axis=0), spread_ref[...])

    q_b = q.astype(BF16)
    valid = lax.broadcasted_iota(jnp.int32, (N_FOX_HEADS, PAGE_ROWS), 1) % N_FOX_HEADS == \
        lax.broadcasted_iota(jnp.int32, (N_FOX_HEADS, PAGE_ROWS), 0)
    scores = []
    for j, k_ref in enumerate(k_refs):
        s = _dot_nt(q_b, k_ref[...].astype(BF16)) * scale + bias_x[j * N_FOX_HEADS:(j + 1) * N_FOX_HEADS]
        scores.append(jnp.where(valid, s, -jnp.inf))
    m_prev = m_sc[...]
    m_cur = scores[0]
    for s in scores[1:]:
        m_cur = jnp.maximum(m_cur, s)
    m_new = jnp.maximum(m_prev, jnp.max(m_cur, axis=1, keepdims=True))
    alpha = jnp.exp(m_prev - m_new)
    m_wide = jnp.concatenate([m_new] * (PAGE_ROWS // LANES), axis=1)
    acc = alpha * acc_sc[...]
    l_part = jnp.zeros((N_FOX_HEADS, PAGE_ROWS), F32)
    for s, v_ref in zip(scores, v_refs):
        p = jnp.exp(s - m_wide)
        l_part = l_part + p
        acc = acc + _dot(p.astype(BF16), v_ref[...].astype(BF16))
    l_new = alpha * l_sc[...] + jnp.sum(l_part, axis=1, keepdims=True)
    m_sc[...] = m_new
    l_sc[...] = l_new
    acc_sc[...] = acc

    @pl.when(i == n_steps - 1)
    def _():
        o_ref[...] = acc / l_new


def _fox_decode(page_table, k_pool, v_pool, f_pool, layer, q, k_new, v_new, logf_new_b, tables):
    bd, n_pages = page_table.shape
    n_steps = n_pages // PAGES_PER_STEP
    depth, n_pool = k_pool.shape[:2]
    k_view = k_pool.reshape(depth, n_pool, PAGE_ROWS, HEAD_DIM)
    v_view = v_pool.reshape(depth, n_pool, PAGE_ROWS, HEAD_DIM)
    f_view = jnp.transpose(f_pool, (0, 1, 3, 2))
    after, spread = tables

    def page_of(b, i, pt, j):
        return pt[b, (n_steps - 1 - i) * PAGES_PER_STEP + j]

    def page_spec(j, rows):
        return pl.BlockSpec((None, None, rows, LANES), lambda b, i, pt: (layer, page_of(b, i, pt, j), 0, 0))

    vec = pl.BlockSpec((None, N_FOX_HEADS, HEAD_DIM), lambda b, i, pt: (b, 0, 0))
    const = lambda a: pl.BlockSpec(a.shape, lambda b, i, pt: (0, 0))
    pages = range(PAGES_PER_STEP)
    grid_spec = pltpu.PrefetchScalarGridSpec(
        num_scalar_prefetch=1,
        grid=(bd, n_steps),
        in_specs=[page_spec(j, N_FOX_HEADS) for j in pages] + [page_spec(j, PAGE_ROWS) for j in pages] * 2
        + [vec, vec, vec, vec, const(after), const(spread)],
        out_specs=vec,
        scratch_shapes=[pltpu.VMEM((N_FOX_HEADS, LANES), F32)] * 4,
    )
    return pl.pallas_call(
        functools.partial(_dec_kernel, n_steps=n_steps),
        grid_spec=grid_spec,
        out_shape=jax.ShapeDtypeStruct((bd, N_FOX_HEADS, HEAD_DIM), F32),
        compiler_params=_cparams("parallel", "arbitrary"),
        name="fox_decode",
    )(page_table, *([f_view] * PAGES_PER_STEP), *([k_view] * PAGES_PER_STEP), *([v_view] * PAGES_PER_STEP),
      q, k_new, v_new, logf_new_b, after, spread)


def _pad_rows(x, rows):
    return jnp.pad(x[:, None, :], ((0, 0), (0, rows - 1), (0, 0)))


def _rows16(x):
    return jnp.pad(x, ((0, SROWS - x.shape[0]), (0, 0)))


def kernel(x_prompt, x_sample, cache_k, cache_v, cache_logf, state_gla, page_table, c_prompt, c_sample,
           ln_in_g, ln_in_b, w_ada, b_ada, w_in, b_f, gm_ln_g, gm_ln_b, gm_ws, gm_bs, gla_w2, gla_b2,
           gla_norm_g, w_o, ln1_g, ln1_b, w_gu, w_down, ln2_g, ln2_b):
    bp, seq, d = x_prompt.shape
    bd = x_sample.shape[0]
    depth = w_ada.shape[0]
    assert d == 4 * GLA_WIDTH and x_sample.shape[1] == 1 and bp + bd <= 16 and bd <= SROWS
    assert w_in.shape[2] == IN_WIDTH
    alpha = (2 * depth) ** 0.25
    m_p = bp * seq

    gla_tables = _gla_tables()
    dec_tables = _dec_tables()
    ext_tables = _ext_tables()

    c16 = jnp.zeros((SROWS, d), F32).at[:bd].set(c_sample).at[bd:bd + bp].set(c_prompt)
    mods = _ada(c16, w_ada, b_ada)

    def mod_p(l, which):
        return mods[l, bd:bd + bp, which * d:(which + 1) * d][:, None, :]

    def mod_s(l, *which):
        return jnp.concatenate([mods[l, :, w * d:(w + 1) * d] for w in which], axis=-1)

    w_in_b = _pack_in(w_in)
    w_o_b = _cast(w_o)
    w_down_b = _cast(w_down)

    xp, hp = _ln_mod(x_prompt, ln_in_g, ln_in_b, mod=(mod_p(0, 1), mod_p(0, 0)), tr=256)
    xs, hs = _ln_mod(x_sample, ln_in_g, ln_in_b,
                     mod=(mods[0, :bd, d:2 * d][:, None, :], mods[0, :bd, 0:d][:, None, :]), tr=1)
    xp, hp = xp.reshape(m_p, d), hp.reshape(m_p, d)
    xs, hs = _rows16(xs[:, 0, :]), _rows16(hs[:, 0, :])

    zero_state = jnp.zeros((bp, GLA_HEADS, GLA_DK, LANES), F32)
    k_all = jnp.zeros((depth, m_p * N_FOX_HEADS, HEAD_DIM), F32)
    v_all = jnp.zeros((depth, m_p * N_FOX_HEADS, HEAD_DIM), F32)
    fp, sp = [], []
    kd, vd, fd, sd, gd = [], [], [], [], []
    for l in range(depth):
        last = l == depth - 1
        q16, k_all, k16, v_all, vt16, ug, tail, samp = _proj(hp, hs, w_in_b, l, k_all, v_all, depth, tm=512)

        samp = samp[:bd]
        heads3 = lambda a: a.reshape(bd, N_FOX_HEADS, HEAD_DIM)
        q_s = heads3(samp[:, 0:FOX_WIDTH])
        k_s = heads3(samp[:, FOX_WIDTH:2 * FOX_WIDTH])
        v_s = heads3(samp[:, 2 * FOX_WIDTH:3 * FOX_WIDTH])
        ug_s = _pad_rows(samp[:, 6 * TN:11 * TN], CHUNK)
        tail_s = _pad_rows(samp[:, 11 * TN:11 * TN + LANES], CHUNK)
        logf_s, _, _ = _fox_cum(tail_s, b_f[l], ext_tables)
        logf_s = logf_s[:, 0, :]
        fox_s = _fox_decode(page_table, cache_k, cache_v, cache_logf, l, q_s, k_s, v_s,
                            jnp.broadcast_to(logf_s[:, :, None], (bd, N_FOX_HEADS, LANES)), dec_tables)
        gm_s, vn_s = _gmlp(ug_s, gm_ln_g[l], gm_ln_b[l], gm_ws[l], gm_bs[l], emit_vn=True)
        gla_s, st_s = _gla(ug_s, tail_s, gla_w2[l], gla_b2[l], gla_norm_g[l], state_gla[l], gla_tables, valid_len=1)
        mix_s = jnp.concatenate([fox_s.reshape(bd, -1).astype(BF16), gm_s[:, 0, :], gla_s[:, 0, :]], axis=-1)

        tail3 = tail.reshape(bp, seq, LANES)
        ug3 = ug.reshape(bp, seq, UG_WIDTH)
        logf, ext_q, ext_k = _fox_cum(tail3, b_f[l], ext_tables)
        fox_o = _fox_prompt(q16.reshape(bp, seq, FOX_WIDTH), k16.reshape(bp, seq, FOX_WIDTH), vt16, ext_q, ext_k)
        gm_o, _ = _gmlp(ug3, gm_ln_g[l], gm_ln_b[l], gm_ws[l], gm_bs[l], emit_vn=False)
        gla_o, st = _gla(ug3, tail3, gla_w2[l], gla_b2[l], gla_norm_g[l], zero_state, gla_tables)

        x1, h1, x1_s, h1_s = _dense_ln(
            [fox_o.reshape(m_p, -1), gm_o.reshape(m_p, -1), gla_o.reshape(m_p, -1)], _rows16(mix_s), w_o_b, l,
            xp, xs, mod_p(l, 2), ln1_g[l], ln1_b[l], (mod_p(l, 4), mod_p(l, 3)), mod_s(l, 2, 4, 3),
            alpha=alpha, rows_per_batch=seq, tm=512)
        act, act_s = _swiglu(h1, h1_s, w_gu, l, tm=2048, tn=512)
        nxt = None if last else (mod_p(l + 1, 1), mod_p(l + 1, 0))
        nxt_s = mod_s(l, 5, 5, 5) if last else jnp.concatenate([mod_s(l, 5), mod_s(l + 1, 1, 0)], axis=-1)
        xp, hp, xs, hs = _dense_ln([act], act_s, w_down_b, l, x1, x1_s, mod_p(l, 5), ln2_g[l], ln2_b[l], nxt, nxt_s,
                                   alpha=alpha, rows_per_batch=seq, tm=256)

        fp.append(logf)
        sp.append(st)
        kd.append(k_s[:, None])
        vd.append(v_s[:, None])
        fd.append(logf_s[:, None, :])
        sd.append(st_s)
        gd.append(vn_s[:, 0:1, :])

    kv_shape = (depth, bp, seq, N_FOX_HEADS, HEAD_DIM)
    return (xp.reshape(bp, seq, d), xs[:bd, None, :], k_all.reshape(kv_shape), v_all.reshape(kv_shape),
            jnp.stack(fp), jnp.stack(sp), jnp.stack(kd), jnp.stack(vd), jnp.stack(fd), jnp.stack(sd), jnp.stack(gd))
```
